```python
import math
import jax, jax.numpy as jnp
from jax import lax
import numpy as np

D_MODEL = 1024
BATCH = 16
SEQ = 256
DEPTH = 4
DEC_BATCH = 8
DEC_SEQ = 1024
PAST_LEN = 256

GRID_W = 64
D_CONV = 512
CONV_WIDTH = 31
N_HEADS = 4
HEAD_DIM = 64
V_DIM = 2 * HEAD_DIM
D_QK = N_HEADS * 2 * HEAD_DIM
D_ATTN = N_HEADS * V_DIM
D_MIX = D_CONV + D_ATTN
D_IN = 2 * D_CONV + 2 * D_QK + D_ATTN
N_EXPERTS = 16
EC_FACTOR = 2
D_EXPERT = 1024
ROPE_BASE = 10000.0
ROPE_AXIS = HEAD_DIM // 2
EPS = 1e-6
Q_BLOCK = 128

kernel_name = "hymba_conformer_diffattn_ec_dit_step"


def rms_norm(x, g):
    x32 = x.astype(jnp.float32)
    y = x32 * lax.rsqrt(jnp.mean(x32 * x32, axis=-1, keepdims=True) + EPS)
    return y.astype(x.dtype) * g


def modulation(cvec, w_mod, b_mod):
    m = jax.nn.silu(cvec) @ w_mod + b_mod
    return jnp.split(m, 6, axis=-1)


def axial_rope_tables(T):
    rows = T // GRID_W
    row = jnp.repeat(jnp.arange(rows, dtype=jnp.float32), GRID_W)
    col = jnp.tile(jnp.arange(GRID_W, dtype=jnp.float32), rows)
    inv = ROPE_BASE ** (-jnp.arange(0, ROPE_AXIS, 2, dtype=jnp.float32) / ROPE_AXIS)
    ar = row[:, None] * inv
    ac = col[:, None] * inv
    shp = (1, T, 1, 1, ROPE_AXIS // 2)
    return (jnp.cos(ar).reshape(shp), jnp.sin(ar).reshape(shp),
            jnp.cos(ac).reshape(shp), jnp.sin(ac).reshape(shp))


def rotate_half_pairs(x, cos, sin):
    n = x.shape[-1] // 2
    x1, x2 = x[..., :n], x[..., n:]
    cos = cos.astype(x.dtype)
    sin = sin.astype(x.dtype)
    return jnp.concatenate([x1 * cos - x2 * sin, x2 * cos + x1 * sin], axis=-1)


def apply_axial_rope(x, rope):
    cos_r, sin_r, cos_c, sin_c = rope
    return jnp.concatenate([rotate_half_pairs(x[..., :ROPE_AXIS], cos_r, sin_r),
                            rotate_half_pairs(x[..., ROPE_AXIS:], cos_c, sin_c)], axis=-1)


def diff_attention(q, k, v, lam):
    B, S = q.shape[0], q.shape[1]
    nb = S // Q_BLOCK
    qb = q.reshape(B, nb, Q_BLOCK, N_HEADS, 2, HEAD_DIM).transpose(1, 0, 2, 3, 4, 5)
    scale = HEAD_DIM ** -0.5

    def one_block(qi):
        s = jnp.einsum('bqhcd,bkhcd->bchqk', qi, k).astype(jnp.float32) * scale
        p = jax.nn.softmax(s, axis=-1)
        a = p[:, 0] - lam * p[:, 1]
        return jnp.einsum('bhqk,bkhv->bqhv', a.astype(v.dtype), v)

    o = lax.map(one_block, qb)
    return o.transpose(1, 0, 2, 3, 4).reshape(B, S, N_HEADS, V_DIM)


def conv_module(a, g, conv_w, conv_b, ln_g, ln_b):
    u = a * jax.nn.sigmoid(g)
    u = lax.conv_general_dilated(
        u, conv_w[:, None, :].astype(u.dtype), window_strides=(1,),
        padding=[(CONV_WIDTH // 2, CONV_WIDTH // 2)],
        dimension_numbers=('NWC', 'WIO', 'NWC'), feature_group_count=D_CONV) + conv_b
    u32 = u.astype(jnp.float32)
    mu = jnp.mean(u32, axis=-1, keepdims=True)
    var = jnp.mean((u32 - mu) ** 2, axis=-1, keepdims=True)
    un = ((u32 - mu) * lax.rsqrt(var + EPS)).astype(u.dtype) * ln_g + ln_b
    return jax.nn.silu(un)


def expert_choice(h, w_router, w_gate, w_up, w_down):
    B, T, D = h.shape
    C = EC_FACTOR * T // N_EXPERTS
    aff = jax.nn.softmax(jnp.einsum('btd,de->bte', h, w_router).astype(jnp.float32), axis=-1)
    gsel, idx = lax.top_k(aff.transpose(0, 2, 1), C)
    xs = jax.vmap(lambda hb, ib: hb[ib])(h, idx)
    hid = jax.nn.silu(jnp.einsum('becd,edf->becf', xs, w_gate)) * jnp.einsum('becd,edf->becf', xs, w_up)
    ye = jnp.einsum('becf,efd->becd', hid, w_down) * gsel[..., None].astype(h.dtype)
    return jax.vmap(lambda yb, ib: jnp.zeros((T, D), yb.dtype).at[ib.reshape(-1)].add(yb.reshape(-1, D)))(ye, idx)


def trunk_layer(x, cvec, layer_idx, rope, ctx_k, ctx_v, norm1_g, w_mod, b_mod, w_in, conv_w, conv_b,
                conv_ln_g, conv_ln_b, lambda_q1, lambda_k1, lambda_q2, lambda_k2, subln_g, w_out,
                norm2_g, w_router, w_gate, w_up, w_down):
    B, T, _ = x.shape
    sh1, sc1, g1, sh2, sc2, g2 = modulation(cvec, w_mod, b_mod)
    h = rms_norm(x, norm1_g) * (1 + sc1) + sh1
    z = h @ w_in
    conv_a, conv_g, q, k, v = jnp.split(
        z, [D_CONV, 2 * D_CONV, 2 * D_CONV + D_QK, 2 * D_CONV + 2 * D_QK], axis=-1)
    q = q.reshape(B, T, N_HEADS, 2, HEAD_DIM)
    k = k.reshape(B, T, N_HEADS, 2, HEAD_DIM)
    v = v.reshape(B, T, N_HEADS, V_DIM)
    if rope is None:
        keys, vals = k, v
    else:
        q = apply_axial_rope(q, rope)
        keys = jnp.concatenate([apply_axial_rope(k, rope), ctx_k.astype(k.dtype)], axis=1)
        vals = jnp.concatenate([v, ctx_v.astype(v.dtype)], axis=1)
    lam_init = 0.8 - 0.6 * math.exp(-0.3 * layer_idx)
    lam = (jnp.exp(jnp.sum(lambda_q1.astype(jnp.float32) * lambda_k1.astype(jnp.float32)))
           - jnp.exp(jnp.sum(lambda_q2.astype(jnp.float32) * lambda_k2.astype(jnp.float32))) + lam_init)
    o = diff_attention(q, keys, vals, lam)
    o = rms_norm(o, subln_g) * (1.0 - lam_init)
    mix = jnp.concatenate([conv_module(conv_a, conv_g, conv_w, conv_b, conv_ln_g, conv_ln_b),
                           o.reshape(B, T, D_ATTN)], axis=-1)
    x = x + g1 * (mix @ w_out)
    h2 = rms_norm(x, norm2_g) * (1 + sc2) + sh2
    x = x + g2 * expert_choice(h2, w_router, w_gate, w_up, w_down)
    return x, k, v


def setup_inputs(seed: int = 0) -> dict:
    key = jax.random.key(seed)
    ks = jax.random.split(key, 32)

    def nrm(k, shape, scale):
        return jax.random.normal(k, shape, jnp.float32) * scale

    gain = lambda k, shape: 1.0 + nrm(k, shape, 0.01)
    return {
        "x_prompt": nrm(ks[0], (BATCH, SEQ, D_MODEL), 1.0),
        "x_sample": nrm(ks[1], (DEC_BATCH, DEC_SEQ, D_MODEL), 1.0),
        "c": nrm(ks[2], (DEC_BATCH, D_MODEL), 1.0),
        "cache_k": nrm(ks[3], (DEC_BATCH, DEPTH, PAST_LEN, N_HEADS, 2, HEAD_DIM), 1.0),
        "cache_v": nrm(ks[4], (DEC_BATCH, DEPTH, PAST_LEN, N_HEADS, V_DIM), 1.0),
        "c_ctx": nrm(ks[5], (D_MODEL,), 1.0),
        "norm1_g": gain(ks[6], (DEPTH, D_MODEL)),
        "w_mod": nrm(ks[7], (DEPTH, D_MODEL, 6 * D_MODEL), 0.5 * D_MODEL ** -0.5),
        "b_mod": nrm(ks[8], (DEPTH, 6 * D_MODEL), 0.02),
        "w_in": nrm(ks[9], (DEPTH, D_MODEL, D_IN), D_MODEL ** -0.5),
        "conv_w": nrm(ks[10], (DEPTH, CONV_WIDTH, D_CONV), CONV_WIDTH ** -0.5),
        "conv_b": nrm(ks[11], (DEPTH, D_CONV), 0.02),
        "conv_ln_g": gain(ks[12], (DEPTH, D_CONV)),
        "conv_ln_b": nrm(ks[13], (DEPTH, D_CONV), 0.02),
        "lambda_q1": nrm(ks[14], (DEPTH, HEAD_DIM), 0.1),
        "lambda_k1": nrm(ks[15], (DEPTH, HEAD_DIM), 0.1),
        "lambda_q2": nrm(ks[16], (DEPTH, HEAD_DIM), 0.1),
        "lambda_k2": nrm(ks[17], (DEPTH, HEAD_DIM), 0.1),
        "subln_g": gain(ks[18], (DEPTH, V_DIM)),
        "w_out": nrm(ks[19], (DEPTH, D_MIX, D_MODEL), D_MIX ** -0.5),
        "norm2_g": gain(ks[20], (DEPTH, D_MODEL)),
        "w_router": nrm(ks[21], (DEPTH, D_MODEL, N_EXPERTS), D_MODEL ** -0.5),
        "w_gate": nrm(ks[22], (DEPTH, N_EXPERTS, D_MODEL, D_EXPERT), D_MODEL ** -0.5),
        "w_up": nrm(ks[23], (DEPTH, N_EXPERTS, D_MODEL, D_EXPERT), D_MODEL ** -0.5),
        "w_down": nrm(ks[24], (DEPTH, N_EXPERTS, D_EXPERT, D_MODEL), D_EXPERT ** -0.5),
        "final_norm_g": gain(ks[25], (D_MODEL,)),
    }


def reference(x_prompt, x_sample, c, cache_k, cache_v, c_ctx, norm1_g, w_mod, b_mod, w_in, conv_w,
              conv_b, conv_ln_g, conv_ln_b, lambda_q1, lambda_k1, lambda_q2, lambda_k2, subln_g,
              w_out, norm2_g, w_router, w_gate, w_up, w_down, final_norm_g):
    rope = axial_rope_tables(x_sample.shape[1])
    c_ctx_vec = c_ctx[None, None, :]
    c_lat = c[:, None, :]
    xp, xs = x_prompt, x_sample
    new_k, new_v = [], []
    for l in range(DEPTH):
        w = (norm1_g[l], w_mod[l], b_mod[l], w_in[l], conv_w[l], conv_b[l], conv_ln_g[l], conv_ln_b[l],
             lambda_q1[l], lambda_k1[l], lambda_q2[l], lambda_k2[l], subln_g[l], w_out[l], norm2_g[l],
             w_router[l], w_gate[l], w_up[l], w_down[l])
        xp, kp, vp = trunk_layer(xp, c_ctx_vec, l, None, None, None, *w)
        new_k.append(kp)
        new_v.append(vp)
        xs, _, _ = trunk_layer(xs, c_lat, l, rope, cache_k[:, l], cache_v[:, l], *w)
    y_prompt = rms_norm(xp, final_norm_g)
    y_sample = rms_norm(xs, final_norm_g)
    new_cache_k = jnp.stack(new_k, axis=1)
    new_cache_v = jnp.stack(new_v, axis=1)
    return (y_prompt, y_sample, new_cache_k, new_cache_v)
```

```python
import functools
import math

import jax
import jax.numpy as jnp
from jax import lax
from jax.experimental import pallas as pl
from jax.experimental.pallas import tpu as pltpu

D_MODEL = 1024
DEPTH = 4
GRID_W = 64
D_CONV = 512
CONV_WIDTH = 31
N_HEADS = 4
HEAD_DIM = 64
V_DIM = 2 * HEAD_DIM
D_QK = N_HEADS * 2 * HEAD_DIM
D_ATTN = N_HEADS * V_DIM
D_MIX = D_CONV + D_ATTN
D_IN = 2 * D_CONV + 2 * D_QK + D_ATTN
N_EXPERTS = 16
EC_FACTOR = 2
D_EXPERT = 1024
ROPE_BASE = 10000.0
ROPE_AXIS = HEAD_DIM // 2
EPS = 1e-6

LANES = 128
MOD_ROWS = 16
TOKEN_TILE = 256
CONV_CHUNK = 64
CONV_PAD = 16
EXPERT_ROW_TILE = 512
EXPERT_SPLIT = 2
RANK_CLAMP = 255.0
VMEM_LIMIT = 56 * 1024 * 1024

F32 = jnp.float32
BF16 = jnp.bfloat16


def _sigmoid(x):
    return 1.0 / (1.0 + jnp.exp(-x))


def _params(*sem):
    return pltpu.CompilerParams(dimension_semantics=sem, vmem_limit_bytes=VMEM_LIMIT)


def _mod_kernel(c_ref, w_ref, b_ref, o_ref):
    c = c_ref[...]
    s = c * _sigmoid(c)
    o_ref[...] = jnp.dot(s.astype(BF16), w_ref[...].astype(BF16),
                         preferred_element_type=F32) + b_ref[...]


def _modulation(cvec, w_mod, b_mod):
    tn = 1536
    out = pl.pallas_call(
        _mod_kernel,
        grid=(DEPTH, 6 * D_MODEL // tn),
        in_specs=[
            pl.BlockSpec((MOD_ROWS, D_MODEL), lambda l, j: (0, 0)),
            pl.BlockSpec((None, D_MODEL, tn), lambda l, j: (l, 0, j)),
            pl.BlockSpec((None, 1, tn), lambda l, j: (l, 0, j)),
        ],
        out_specs=pl.BlockSpec((None, MOD_ROWS, tn), lambda l, j: (l, 0, j)),
        out_shape=jax.ShapeDtypeStruct((DEPTH, MOD_ROWS, 6 * D_MODEL), F32),
        compiler_params=_params("parallel", "parallel"),
        name="modulation",
    )(cvec, w_mod, b_mod.reshape(DEPTH, 1, 6 * D_MODEL))
    return out.reshape(DEPTH, MOD_ROWS, 6, D_MODEL)


def _inproj_kernel(*refs, rope, cache_out):
    x_ref, mod_ref, g_ref, w_ref = refs[:4]
    refs = refs[4:]
    if rope:
        cos_ref, sin_ref = refs[:2]
        refs = refs[2:]
    u_ref, q_ref, k_ref, v_ref = refs[:4]
    x = x_ref[...]
    ms = jnp.mean(x * x, axis=-1, keepdims=True)
    y = x * lax.rsqrt(ms + EPS) * g_ref[...]
    h = y * (1.0 + mod_ref[1:2, :]) + mod_ref[0:1, :]
    z = jnp.dot(h.astype(BF16), w_ref[...], preferred_element_type=F32)
    u_ref[...] = z[:, :D_CONV] * _sigmoid(z[:, D_CONV:2 * D_CONV])
    q = z[:, 2 * D_CONV:2 * D_CONV + D_QK]
    k = z[:, 2 * D_CONV + D_QK:2 * D_CONV + 2 * D_QK]
    v = z[:, 2 * D_CONV + 2 * D_QK:]
    if cache_out:
        kf_ref, vf_ref = refs[4:6]
        kf_ref[...] = k
        vf_ref[...] = v
    if rope:
        lane = lax.broadcasted_iota(jnp.int32, q.shape, 1)
        first = (lane % ROPE_AXIS) < (ROPE_AXIS // 2)
        cos = cos_ref[...]
        sin = sin_ref[...]

        def rot(t):
            partner = jnp.where(first, pltpu.roll(t, D_QK - ROPE_AXIS // 2, 1),
                                pltpu.roll(t, ROPE_AXIS // 2, 1))
            return t * cos + partner * sin

        q = rot(q)
        k = rot(k)
    q_ref[...] = (q * (HEAD_DIM ** -0.5)).astype(BF16)
    k_ref[...] = k.astype(BF16)
    v_ref[...] = v.astype(BF16)


def _inproj(x, mods, layer, norm_g, w_in_b, mod_row, rope_tabs, seq_len, cache_out):
    n = x.shape[0]
    tm = TOKEN_TILE
    tiles_per_seq = seq_len // tm
    rope = rope_tabs is not None
    in_specs = [
        pl.BlockSpec((tm, D_MODEL), lambda i: (i, 0)),
        pl.BlockSpec((None, None, 6, D_MODEL), lambda i: (layer, mod_row(i), 0, 0)),
        pl.BlockSpec((None, 1, D_MODEL), lambda i: (layer, 0, 0)),
        pl.BlockSpec((None, D_MODEL, D_IN), lambda i: (layer, 0, 0)),
    ]
    args = [x, mods, norm_g, w_in_b]
    if rope:
        in_specs += [pl.BlockSpec((tm, D_QK), lambda i: (i % tiles_per_seq, 0))] * 2
        args += list(rope_tabs)
    half = pl.BlockSpec((tm, D_CONV), lambda i: (i, 0))
    out_specs = [half, half, half, half]
    out_shape = [jax.ShapeDtypeStruct((n, D_CONV), F32)] + [jax.ShapeDtypeStruct((n, D_QK), BF16)] * 3
    if cache_out:
        out_specs += [half, half]
        out_shape += [jax.ShapeDtypeStruct((n, D_QK), F32)] * 2
    return pl.pallas_call(
        functools.partial(_inproj_kernel, rope=rope, cache_out=cache_out),
        grid=(n // tm,),
        in_specs=in_specs,
        out_specs=out_specs,
        out_shape=out_shape,
        compiler_params=_params("parallel"),
        name="inproj",
    )(*args)


def _conv_kernel(u_ref, w_ref, b_ref, lg_ref, lb_ref, o_ref, pad_ref, sh_ref):
    T = u_ref.shape[0]
    tail = CONV_PAD + T
    pad_ref[0:CONV_PAD, :] = jnp.zeros((CONV_PAD, D_CONV), F32)
    pad_ref[tail:tail + CONV_PAD, :] = jnp.zeros((CONV_PAD, D_CONV), F32)
    pad_ref[CONV_PAD:tail, :] = u_ref[...]
    rows = sh_ref.shape[1]
    for r in range(8):
        sh_ref[r] = pad_ref[r:r + rows, :]
    first = CONV_PAD - CONV_WIDTH // 2

    def chunk(i, carry):
        base = pl.multiple_of(i * CONV_CHUNK, CONV_CHUNK)
        acc = jnp.zeros((CONV_CHUNK, D_CONV), F32) + b_ref[...]
        for k in range(CONV_WIDTH):
            m, r = divmod(first + k, 8)
            acc = acc + w_ref[k:k + 1, :] * sh_ref[r, pl.ds(base + 8 * m, CONV_CHUNK), :]
        mu = jnp.mean(acc, axis=-1, keepdims=True)
        d = acc - mu
        var = jnp.mean(d * d, axis=-1, keepdims=True)
        un = d * lax.rsqrt(var + EPS) * lg_ref[...] + lb_ref[...]
        o_ref[pl.ds(base, CONV_CHUNK), :] = (un * _sigmoid(un)).astype(BF16)
        return carry

    lax.fori_loop(0, T // CONV_CHUNK, chunk, 0)


def _conv(u, layer, conv_w, conv_b, ln_g, ln_b, seq_len):
    n = u.shape[0]
    T = seq_len
    rows = T + 2 * CONV_PAD - 8
    vec = pl.BlockSpec((None, 1, D_CONV), lambda b: (layer, 0, 0))
    return pl.pallas_call(
        _conv_kernel,
        grid=(n // T,),
        in_specs=[
            pl.BlockSpec((T, D_CONV), lambda b: (b, 0)),
            pl.BlockSpec((None, CONV_WIDTH, D_CONV), lambda b: (layer, 0, 0)),
            vec, vec, vec,
        ],
        out_specs=pl.BlockSpec((T, D_CONV), lambda b: (b, 0)),
        out_shape=jax.ShapeDtypeStruct((n, D_CONV), BF16),
        scratch_shapes=[pltpu.VMEM((T + 2 * CONV_PAD, D_CONV), F32),
                        pltpu.VMEM((8, rows, D_CONV), F32)],
        compiler_params=_params("parallel"),
        name="conv",
    )(u, conv_w, conv_b, ln_g, ln_b)


def _attn_kernel(*refs, lam_init, has_ctx):
    lam_ref, sg_ref, q_ref, k_ref, v_ref = refs[:5]
    if has_ctx:
        ck_ref, cv_ref = refs[5:7]
    o_ref = refs[-1]
    l = lam_ref[...]
    lam = (jnp.exp(jnp.sum(l[0:1, :] * l[1:2, :], axis=-1, keepdims=True))
           - jnp.exp(jnp.sum(l[2:3, :] * l[3:4, :], axis=-1, keepdims=True)) + lam_init)
    tq = q_ref.shape[0]
    lane = lax.broadcasted_iota(jnp.int32, (tq, V_DIM), 1)
    nt = (((1,), (1,)), ((), ()))
    for h in range(N_HEADS):
        sl = slice(h * V_DIM, (h + 1) * V_DIM)
        qh = q_ref[:, sl]
        kh = k_ref[:, sl]
        if has_ctx:
            ckh = ck_ref[:, sl].astype(BF16)
        probs = []
        for c in range(2):
            keep = (lane < HEAD_DIM) if c == 0 else (lane >= HEAD_DIM)
            qc = jnp.where(keep, qh, jnp.zeros_like(qh))
            s = lax.dot_general(qc, kh, nt, preferred_element_type=F32)
            m = jnp.max(s, axis=-1, keepdims=True)
            if has_ctx:
                s2 = lax.dot_general(qc, ckh, nt, preferred_element_type=F32)
                m = jnp.maximum(m, jnp.max(s2, axis=-1, keepdims=True))
            e = jnp.exp(s - m)
            den = jnp.sum(e, axis=-1, keepdims=True)
            if has_ctx:
                e2 = jnp.exp(s2 - m)
                den = den + jnp.sum(e2, axis=-1, keepdims=True)
                probs.append((e / den, e2 / den))
            else:
                probs.append((e / den, None))
        a = probs[0][0] - lam * probs[1][0]
        o = jnp.dot(a.astype(BF16), v_ref[:, sl], preferred_element_type=F32)
        if has_ctx:
            a2 = probs[0][1] - lam * probs[1][1]
            o = o + jnp.dot(a2.astype(BF16), cv_ref[:, sl].astype(BF16), preferred_element_type=F32)
        o = o * lax.rsqrt(jnp.mean(o * o, axis=-1, keepdims=True) + EPS)
        o = o * sg_ref[...] * (1.0 - lam_init)
        o_ref[:, sl] = o.astype(BF16)


def _attention(q, k, v, lam_vecs, subln_g, layer, seq_len, ctx):
    n = q.shape[0]
    T = seq_len
    tq = TOKEN_TILE
    nq = T // tq
    lam_init = 0.8 - 0.6 * math.exp(-0.3 * layer)
    has_ctx = ctx is not None
    kv = pl.BlockSpec((T, D_QK), lambda b, i: (b, 0))
    in_specs = [
        pl.BlockSpec((None, 4, HEAD_DIM), lambda b, i: (layer, 0, 0)),
        pl.BlockSpec((None, 1, V_DIM), lambda b, i: (layer, 0, 0)),
        pl.BlockSpec((tq, D_QK), lambda b, i: (b * nq + i, 0)),
        kv, kv,
    ]
    args = [lam_vecs, subln_g, q, k, v]
    if has_ctx:
        past = ctx[0].shape[2]
        cspec = pl.BlockSpec((None, None, past, D_QK), lambda b, i: (b, layer, 0, 0))
        in_specs += [cspec, cspec]
        args += list(ctx)
    return pl.pallas_call(
        functools.partial(_attn_kernel, lam_init=lam_init, has_ctx=has_ctx),
        grid=(n // T, nq),
        in_specs=in_specs,
        out_specs=pl.BlockSpec((tq, D_ATTN), lambda b, i: (b * nq + i, 0)),
        out_shape=jax.ShapeDtypeStruct((n, D_ATTN), BF16),
        compiler_params=_params("parallel", "parallel"),
        name="attention",
    )(*args)


def _outproj_kernel(x_ref, mc_ref, ma_ref, mod_ref, w_ref, g_ref, wr_ref, xn_ref, h2_ref, aff_ref):
    acc = jnp.dot(mc_ref[...], w_ref[0:D_CONV, :], preferred_element_type=F32)
    acc = acc + jnp.dot(ma_ref[...], w_ref[D_CONV:D_MIX, :], preferred_element_type=F32)
    x = x_ref[...] + mod_ref[2:3, :] * acc
    xn_ref[...] = x
    ms = jnp.mean(x * x, axis=-1, keepdims=True)
    y = x * lax.rsqrt(ms + EPS) * g_ref[...]
    h2 = (y * (1.0 + mod_ref[4:5, :]) + mod_ref[3:4, :]).astype(BF16)
    h2_ref[...] = h2
    logits = jnp.dot(h2, wr_ref[...], preferred_element_type=F32)
    lane = lax.broadcasted_iota(jnp.int32, logits.shape, 1)
    logits = jnp.where(lane < N_EXPERTS, logits, -jnp.inf)
    e = jnp.exp(logits - jnp.max(logits, axis=-1, keepdims=True))
    aff_ref[...] = e / jnp.sum(e, axis=-1, keepdims=True)


def _outproj(x, mix_conv, mix_attn, mods, layer, w_out_b, norm_g, w_router_b, mod_row):
    n = x.shape[0]
    tm = TOKEN_TILE
    row = lambda i: (i, 0)
    return pl.pallas_call(
        _outproj_kernel,
        grid=(n // tm,),
        in_specs=[
            pl.BlockSpec((tm, D_MODEL), row),
            pl.BlockSpec((tm, D_CONV), row),
            pl.BlockSpec((tm, D_ATTN), row),
            pl.BlockSpec((None, None, 6, D_MODEL), lambda i: (layer, mod_row(i), 0, 0)),
            pl.BlockSpec((None, D_MIX, D_MODEL), lambda i: (layer, 0, 0)),
            pl.BlockSpec((None, 1, D_MODEL), lambda i: (layer, 0, 0)),
            pl.BlockSpec((None, D_MODEL, LANES), lambda i: (layer, 0, 0)),
        ],
        out_specs=[pl.BlockSpec((tm, D_MODEL), row), pl.BlockSpec((tm, D_MODEL), row),
                   pl.BlockSpec((tm, LANES), row)],
        out_shape=[jax.ShapeDtypeStruct((n, D_MODEL), F32), jax.ShapeDtypeStruct((n, D_MODEL), BF16),
                   jax.ShapeDtypeStruct((n, LANES), F32)],
        compiler_params=_params("parallel"),
        name="outproj",
    )(x, mix_conv, mix_attn, mods, w_out_b, norm_g, w_router_b)


def _route_kernel(aff_ref, h2_ref, xs_ref, gs_ref, rc_ref, afft_ref, colb_ref, rank_ref, *, cap):
    T = aff_ref.shape[0]
    nb = T // LANES
    aff_t = aff_ref[...].T
    for e in range(N_EXPERTS):
        afft_ref[e] = aff_t[e:e + 1, :]
        colb_ref[e] = jnp.broadcast_to(aff_ref[:, e:e + 1], (T, LANES))
    sub = lax.broadcasted_iota(jnp.int32, (LANES, LANES), 0)
    lan = lax.broadcasted_iota(jnp.int32, (LANES, LANES), 1)
    earlier = sub < lan

    def count(e, carry):
        for J in range(nb):
            row = afft_ref[e, :, J * LANES:(J + 1) * LANES]
            acc = jnp.zeros((LANES, LANES), F32)
            for I in range(nb):
                cb = colb_ref[e, I * LANES:(I + 1) * LANES, :]
                if I < J:
                    acc = acc + jnp.where(cb >= row, 1.0, 0.0)
                elif I > J:
                    acc = acc + jnp.where(cb > row, 1.0, 0.0)
                else:
                    acc = acc + jnp.where(earlier, jnp.where(cb >= row, 1.0, 0.0),
                                          jnp.where(cb > row, 1.0, 0.0))
            rank_ref[e, :, J * LANES:(J + 1) * LANES] = jnp.sum(acc, axis=0, keepdims=True)
        return carry

    lax.fori_loop(0, N_EXPERTS, count, 0)

    slot = lax.broadcasted_iota(jnp.int32, (cap, T), 0).astype(F32)

    def gather(e, carry):
        onehot = jnp.where(rank_ref[e] == slot, 1.0, 0.0)
        xs_ref[e] = jnp.dot(onehot.astype(BF16), h2_ref[...],
                            preferred_element_type=F32).astype(BF16)
        gs_ref[e] = jnp.sum(onehot * afft_ref[e], axis=-1, keepdims=True)
        return carry

    lax.fori_loop(0, N_EXPERTS, gather, 0)
    rank_rows = jnp.concatenate([rank_ref[e] for e in range(N_EXPERTS)]
                                + [jnp.zeros((LANES - N_EXPERTS, T), F32)], axis=0)
    rc_ref[...] = jnp.minimum(rank_rows.T, RANK_CLAMP)


def _route(aff, h2, seq_len):
    n = aff.shape[0]
    T = seq_len
    nreq = n // T
    cap = EC_FACTOR * T // N_EXPERTS
    return pl.pallas_call(
        functools.partial(_route_kernel, cap=cap),
        grid=(nreq,),
        in_specs=[pl.BlockSpec((T, LANES), lambda b: (b, 0)),
                  pl.BlockSpec((T, D_MODEL), lambda b: (b, 0))],
        out_specs=[pl.BlockSpec((N_EXPERTS, cap, D_MODEL), lambda b: (0, b, 0)),
                   pl.BlockSpec((N_EXPERTS, cap, 1), lambda b: (0, b, 0)),
                   pl.BlockSpec((T, LANES), lambda b: (b, 0))],
        out_shape=[jax.ShapeDtypeStruct((N_EXPERTS, nreq * cap, D_MODEL), BF16),
                   jax.ShapeDtypeStruct((N_EXPERTS, nreq * cap, 1), F32),
                   jax.ShapeDtypeStruct((n, LANES), F32)],
        scratch_shapes=[pltpu.VMEM((N_EXPERTS, 1, T), F32),
                        pltpu.VMEM((N_EXPERTS, T, LANES), F32),
                        pltpu.VMEM((N_EXPERTS, 1, T), F32)],
        compiler_params=_params("parallel"),
        name="route",
    )(aff, h2)


def _expert_kernel(xp_ref, xs_ref, gp_ref, gsm_ref, wg_ref, wu_ref, wd_ref, yp_ref, ys_ref,
                   accp_ref, accs_ref):
    j = pl.program_id(1)
    wg = wg_ref[...].astype(BF16)
    wu = wu_ref[...].astype(BF16)
    wd = wd_ref[...].astype(BF16)
    for x_ref, g_ref, y_ref, acc_ref in ((xp_ref, gp_ref, yp_ref, accp_ref),
                                         (xs_ref, gsm_ref, ys_ref, accs_ref)):
        for t in range(x_ref.shape[0] // EXPERT_ROW_TILE):
            rows = slice(t * EXPERT_ROW_TILE, (t + 1) * EXPERT_ROW_TILE)
            x = x_ref[rows, :]
            hg = jnp.dot(x, wg, preferred_element_type=F32)
            hu = jnp.dot(x, wu, preferred_element_type=F32)
            hid = (hg * _sigmoid(hg) * hu).astype(BF16)
            y = jnp.dot(hid, wd, preferred_element_type=F32)

            @pl.when(j == 0)
            def _():
                acc_ref[rows, :] = y

            @pl.when(j == EXPERT_SPLIT - 1)
            def _():
                y_ref[rows, :] = ((acc_ref[rows, :] + y) * g_ref[rows, :]).astype(BF16)


def _experts(xs_p, xs_s, gs_p, gs_s, layer, w_gate, w_up, w_down):
    np_, ns_ = xs_p.shape[1], xs_s.shape[1]
    fe = D_EXPERT // EXPERT_SPLIT
    slots = lambda n, w: pl.BlockSpec((None, n, w), lambda e, j: (e, 0, 0))
    return pl.pallas_call(
        _expert_kernel,
        grid=(N_EXPERTS, EXPERT_SPLIT),
        in_specs=[
            slots(np_, D_MODEL), slots(ns_, D_MODEL), slots(np_, 1), slots(ns_, 1),
            pl.BlockSpec((None, None, D_MODEL, fe), lambda e, j: (layer, e, 0, j)),
            pl.BlockSpec((None, None, D_MODEL, fe), lambda e, j: (layer, e, 0, j)),
            pl.BlockSpec((None, None, fe, D_MODEL), lambda e, j: (layer, e, j, 0)),
        ],
        out_specs=[slots(np_, D_MODEL), slots(ns_, D_MODEL)],
        out_shape=[jax.ShapeDtypeStruct(xs_p.shape, BF16), jax.ShapeDtypeStruct(xs_s.shape, BF16)],
        scratch_shapes=[pltpu.VMEM((np_, D_MODEL), F32), pltpu.VMEM((ns_, D_MODEL), F32)],
        compiler_params=_params("parallel", "arbitrary"),
        name="experts",
    )(xs_p, xs_s, gs_p, gs_s, w_gate, w_up, w_down)


def _combine_kernel(*refs, cap, final):
    xn_ref, rc_ref, ye_ref, mod_ref, ex_ref = refs[:5]
    if final:
        fg_ref = refs[5]
    o_ref = refs[-1]
    ranks = jnp.dot(rc_ref[...].astype(BF16), ex_ref[...], preferred_element_type=F32)
    lane = lax.broadcasted_iota(jnp.int32, ranks.shape, 1)
    slot = (lane % cap).astype(F32)
    onehot = jnp.where(ranks == slot, 1.0, 0.0).astype(BF16)
    ye = ye_ref[...].reshape(N_EXPERTS * cap, D_MODEL)
    moe = jnp.dot(onehot, ye, preferred_element_type=F32)
    x = xn_ref[...] + mod_ref[5:6, :] * moe
    if final:
        x = x * lax.rsqrt(jnp.mean(x * x, axis=-1, keepdims=True) + EPS) * fg_ref[...]
    o_ref[...] = x


def _combine(xn, rank_col, ye, mods, layer, mod_row, seq_len, final_g):
    n = xn.shape[0]
    T = seq_len
    tm = TOKEN_TILE
    tiles = T // tm
    cap = EC_FACTOR * T // N_EXPERTS
    final = final_g is not None
    expander = (jnp.arange(LANES)[:, None] == (jnp.arange(N_EXPERTS * cap) // cap)[None, :]).astype(BF16)
    in_specs = [
        pl.BlockSpec((tm, D_MODEL), lambda i: (i, 0)),
        pl.BlockSpec((tm, LANES), lambda i: (i, 0)),
        pl.BlockSpec((N_EXPERTS, cap, D_MODEL), lambda i: (0, i // tiles, 0)),
        pl.BlockSpec((None, None, 6, D_MODEL), lambda i: (layer, mod_row(i), 0, 0)),
        pl.BlockSpec((LANES, N_EXPERTS * cap), lambda i: (0, 0)),
    ]
    args = [xn, rank_col, ye, mods, expander]
    if final:
        in_specs.append(pl.BlockSpec((1, D_MODEL), lambda i: (0, 0)))
        args.append(final_g)
    return pl.pallas_call(
        functools.partial(_combine_kernel, cap=cap, final=final),
        grid=(n // tm,),
        in_specs=in_specs,
        out_specs=pl.BlockSpec((tm, D_MODEL), lambda i: (i, 0)),
        out_shape=jax.ShapeDtypeStruct((n, D_MODEL), F32),
        compiler_params=_params("parallel"),
        name="combine",
    )(*args)


def _rope_tables(T):
    rows = T // GRID_W
    row = jnp.repeat(jnp.arange(rows, dtype=F32), GRID_W)
    col = jnp.tile(jnp.arange(GRID_W, dtype=F32), rows)
    inv = ROPE_BASE ** (-jnp.arange(0, ROPE_AXIS, 2, dtype=F32) / ROPE_AXIS)
    ar = row[:, None] * inv
    ac = col[:, None] * inv
    cos = jnp.concatenate([jnp.cos(ar), jnp.cos(ar), jnp.cos(ac), jnp.cos(ac)], axis=-1)
    sin = jnp.concatenate([-jnp.sin(ar), jnp.sin(ar), -jnp.sin(ac), jnp.sin(ac)], axis=-1)
    reps = D_QK // HEAD_DIM
    return jnp.tile(cos, (1, reps)), jnp.tile(sin, (1, reps))


def kernel(x_prompt, x_sample, c, cache_k, cache_v, c_ctx, norm1_g, w_mod, b_mod, w_in, conv_w, conv_b,
           conv_ln_g, conv_ln_b, lambda_q1, lambda_k1, lambda_q2, lambda_k2, subln_g, w_out, norm2_g,
           w_router, w_gate, w_up, w_down, final_norm_g):
    batch, seq, _ = x_prompt.shape
    dec_batch, dec_seq, _ = x_sample.shape
    past = cache_k.shape[2]
    assert 1 + dec_batch <= MOD_ROWS

    cvec = jnp.concatenate([c_ctx[None, :], c, jnp.zeros((MOD_ROWS - 1 - dec_batch, D_MODEL), F32)], axis=0)
    mods = _modulation(cvec, w_mod, b_mod)

    w_in_b = w_in.astype(BF16)
    w_out_b = w_out.astype(BF16)
    w_router_b = jnp.pad(w_router, ((0, 0), (0, 0), (0, LANES - N_EXPERTS))).astype(BF16)
    lam_vecs = jnp.stack([lambda_q1, lambda_k1, lambda_q2, lambda_k2], axis=1)
    vec = lambda a: a.reshape(DEPTH, 1, a.shape[-1])
    norm1_v, norm2_v, subln_v = vec(norm1_g), vec(norm2_g), vec(subln_g)
    conv_b_v, ln_g_v, ln_b_v = vec(conv_b), vec(conv_ln_g), vec(conv_ln_b)
    ctx_k = cache_k.reshape(dec_batch, DEPTH, past, D_QK)
    ctx_v = cache_v.reshape(dec_batch, DEPTH, past, D_ATTN)
    rope_tabs = _rope_tables(dec_seq)
    final_g = final_norm_g.reshape(1, D_MODEL)

    prompt_row = lambda i: 0
    tiles_per_sample = dec_seq // TOKEN_TILE
    sample_row = lambda i: 1 + i // tiles_per_sample

    xp = x_prompt.reshape(batch * seq, D_MODEL)
    xs = x_sample.reshape(dec_batch * dec_seq, D_MODEL)
    new_k, new_v = [], []
    for l in range(DEPTH):
        last = l == DEPTH - 1
        up, qp, kp, vp, kf, vf = _inproj(xp, mods, l, norm1_v, w_in_b, prompt_row, None, seq, True)
        us, qs, ks, vs = _inproj(xs, mods, l, norm1_v, w_in_b, sample_row, rope_tabs, dec_seq, False)
        new_k.append(kf.reshape(batch, seq, N_HEADS, 2, HEAD_DIM))
        new_v.append(vf.reshape(batch, seq, N_HEADS, V_DIM))
        cp = _conv(up, l, conv_w, conv_b_v, ln_g_v, ln_b_v, seq)
        cs = _conv(us, l, conv_w, conv_b_v, ln_g_v, ln_b_v, dec_seq)
        op = _attention(qp, kp, vp, lam_vecs, subln_v, l, seq, None)
        os_ = _attention(qs, ks, vs, lam_vecs, subln_v, l, dec_seq, (ctx_k, ctx_v))
        xnp, h2p, affp = _outproj(xp, cp, op, mods, l, w_out_b, norm2_v, w_router_b, prompt_row)
        xns, h2s, affs = _outproj(xs, cs, os_, mods, l, w_out_b, norm2_v, w_router_b, sample_row)
        slots_p, gates_p, rank_p = _route(affp, h2p, seq)
        slots_s, gates_s, rank_s = _route(affs, h2s, dec_seq)
        ye_p, ye_s = _experts(slots_p, slots_s, gates_p, gates_s, l, w_gate, w_up, w_down)
        xp = _combine(xnp, rank_p, ye_p, mods, l, prompt_row, seq, final_g if last else None)
        xs = _combine(xns, rank_s, ye_s, mods, l, sample_row, dec_seq, final_g if last else None)

    y_prompt = xp.reshape(batch, seq, D_MODEL)
    y_sample = xs.reshape(dec_batch, dec_seq, D_MODEL)
    return (y_prompt, y_sample, jnp.stack(new_k, axis=1), jnp.stack(new_v, axis=1))
```

```python
import functools
import math

import jax
import jax.numpy as jnp
from jax import lax
from jax.experimental import pallas as pl
from jax.experimental.pallas import tpu as pltpu

D_MODEL = 1024
DEPTH = 4
GRID_W = 64
D_CONV = 512
CONV_WIDTH = 31
N_HEADS = 4
HEAD_DIM = 64
V_DIM = 2 * HEAD_DIM
D_QK = N_HEADS * 2 * HEAD_DIM
D_ATTN = N_HEADS * V_DIM
D_MIX = D_CONV + D_ATTN
D_IN = 2 * D_CONV + 2 * D_QK + D_ATTN
N_EXPERTS = 16
EC_FACTOR = 2
D_EXPERT = 1024
ROPE_BASE = 10000.0
ROPE_AXIS = HEAD_DIM // 2
EPS = 1e-6
LOG2_E = math.log2(math.e)

LANES = 128
MOD_ROWS = 16
TOKEN_TILE = 256
MAX_ROW_TILE = 512
CONV_CHUNK = 64
NORM_CHUNK = 256
CONV_PAD = 16
EXPERT_ROW_TILE = 512
EXPERT_SPLIT = 2
MANTISSA_BITS = 23
VMEM_LIMIT = 56 * 1024 * 1024

F32 = jnp.float32
BF16 = jnp.bfloat16


def _sigmoid(x):
    return 1.0 / (1.0 + jnp.exp(-x))


def _row_tile(seq_len):
    return min(MAX_ROW_TILE, seq_len)


def _params(*sem):
    return pltpu.CompilerParams(dimension_semantics=sem, vmem_limit_bytes=VMEM_LIMIT)


def _mod_kernel(c_ref, w_ref, b_ref, o_ref):
    c = c_ref[...]
    s = c * _sigmoid(c)
    o_ref[...] = jnp.dot(s.astype(BF16), w_ref[...].astype(BF16),
                         preferred_element_type=F32) + b_ref[...]


def _modulation(cvec, w_mod, b_mod):
    tn = 1536
    out = pl.pallas_call(
        _mod_kernel,
        grid=(DEPTH, 6 * D_MODEL // tn),
        in_specs=[
            pl.BlockSpec((MOD_ROWS, D_MODEL), lambda l, j: (0, 0)),
            pl.BlockSpec((None, D_MODEL, tn), lambda l, j: (l, 0, j)),
            pl.BlockSpec((None, 1, tn), lambda l, j: (l, 0, j)),
        ],
        out_specs=pl.BlockSpec((None, MOD_ROWS, tn), lambda l, j: (l, 0, j)),
        out_shape=jax.ShapeDtypeStruct((DEPTH, MOD_ROWS, 6 * D_MODEL), F32),
        compiler_params=_params("parallel", "parallel"),
        name="modulation",
    )(cvec, w_mod, b_mod.reshape(DEPTH, 1, 6 * D_MODEL))
    return out.reshape(DEPTH, MOD_ROWS, 6, D_MODEL)


def _inproj_kernel(*refs, rope, cache_out):
    x_ref, mod_ref, g_ref, w_ref = refs[:4]
    refs = refs[4:]
    if rope:
        cos_ref, sin_ref = refs[:2]
        refs = refs[2:]
    u_ref, q_ref, k_ref, v_ref = refs[:4]
    x = x_ref[...]
    ms = jnp.mean(x * x, axis=-1, keepdims=True)
    y = x * lax.rsqrt(ms + EPS) * g_ref[...]
    h = y * (1.0 + mod_ref[1:2, :]) + mod_ref[0:1, :]
    z = jnp.dot(h.astype(BF16), w_ref[...], preferred_element_type=F32)
    u_ref[...] = z[:, :D_CONV] * _sigmoid(z[:, D_CONV:2 * D_CONV])
    q = z[:, 2 * D_CONV:2 * D_CONV + D_QK]
    k = z[:, 2 * D_CONV + D_QK:2 * D_CONV + 2 * D_QK]
    v = z[:, 2 * D_CONV + 2 * D_QK:]
    if cache_out:
        kf_ref, vf_ref = refs[4:6]
        kf_ref[...] = k
        vf_ref[...] = v
    if rope:
        lane = lax.broadcasted_iota(jnp.int32, q.shape, 1)
        first = (lane % ROPE_AXIS) < (ROPE_AXIS // 2)
        cos = cos_ref[...]
        sin = sin_ref[...]

        def rot(t):
            partner = jnp.where(first, pltpu.roll(t, D_QK - ROPE_AXIS // 2, 1),
                                pltpu.roll(t, ROPE_AXIS // 2, 1))
            return t * cos + partner * sin

        q = rot(q)
        k = rot(k)
    q_ref[...] = (q * (HEAD_DIM ** -0.5 * LOG2_E)).astype(BF16)
    k_ref[...] = k.astype(BF16)
    v_ref[...] = v.astype(BF16)


def _inproj(x, mods, layer, norm_g, w_in_b, mod_row, rope_tabs, seq_len, cache_out):
    n = x.shape[0]
    tm = _row_tile(seq_len)
    tiles_per_seq = seq_len // tm
    rope = rope_tabs is not None
    in_specs = [
        pl.BlockSpec((tm, D_MODEL), lambda i: (i, 0)),
        pl.BlockSpec((None, None, 6, D_MODEL), lambda i: (layer, mod_row(i), 0, 0)),
        pl.BlockSpec((None, 1, D_MODEL), lambda i: (layer, 0, 0)),
        pl.BlockSpec((None, D_MODEL, D_IN), lambda i: (layer, 0, 0)),
    ]
    args = [x, mods, norm_g, w_in_b]
    if rope:
        in_specs += [pl.BlockSpec((tm, D_QK), lambda i: (i % tiles_per_seq, 0))] * 2
        args += list(rope_tabs)
    half = pl.BlockSpec((tm, D_CONV), lambda i: (i, 0))
    out_specs = [half, half, half, half]
    out_shape = [jax.ShapeDtypeStruct((n, D_CONV), F32)] + [jax.ShapeDtypeStruct((n, D_QK), BF16)] * 3
    if cache_out:
        out_specs += [half, half]
        out_shape += [jax.ShapeDtypeStruct((n, D_QK), F32)] * 2
    return pl.pallas_call(
        functools.partial(_inproj_kernel, rope=rope, cache_out=cache_out),
        grid=(n // tm,),
        in_specs=in_specs,
        out_specs=out_specs,
        out_shape=out_shape,
        compiler_params=_params("parallel"),
        name="inproj",
    )(*args)


def _conv_kernel(u_ref, w_ref, b_ref, lg_ref, lb_ref, o_ref, pad_ref, sh_ref, wb_ref, acc_ref):
    T = u_ref.shape[0]
    tail = CONV_PAD + T
    for k in range(CONV_WIDTH):
        wb_ref[k] = jnp.broadcast_to(w_ref[k:k + 1, :], (8, D_CONV))
    pad_ref[0:CONV_PAD, :] = jnp.zeros((CONV_PAD, D_CONV), F32)
    pad_ref[tail:tail + CONV_PAD, :] = jnp.zeros((CONV_PAD, D_CONV), F32)
    pad_ref[CONV_PAD:tail, :] = u_ref[...]
    rows = sh_ref.shape[1]
    for r in range(8):
        sh_ref[r] = pad_ref[r:r + rows, :]
    first = CONV_PAD - CONV_WIDTH // 2

    groups = CONV_CHUNK // 8

    for q in range(D_CONV // LANES):
        lanes = slice(q * LANES, (q + 1) * LANES)
        w = [wb_ref[k, :, lanes] for k in range(CONV_WIDTH)]

        def taps_chunk(i, carry, lanes=lanes, w=w):
            base = pl.multiple_of(i * CONV_CHUNK, CONV_CHUNK)
            acc = jnp.zeros((groups, 8, LANES), F32)
            for r in range(8):
                taps = [(k, (first + k) // 8) for k in range(CONV_WIDTH) if (first + k) % 8 == r]
                m0 = min(m for _, m in taps)
                span = max(m for _, m in taps) - m0 + groups
                x = sh_ref[r, pl.ds(base + 8 * m0, 8 * span), lanes].reshape(span, 8, LANES)
                for k, m in taps:
                    acc = acc + w[k] * x[m - m0:m - m0 + groups]
            acc_ref[pl.ds(base, CONV_CHUNK), lanes] = acc.reshape(CONV_CHUNK, LANES)
            return carry

        lax.fori_loop(0, T // CONV_CHUNK, taps_chunk, 0)

    def norm_chunk(i, carry):
        base = pl.multiple_of(i * NORM_CHUNK, NORM_CHUNK)
        acc = acc_ref[pl.ds(base, NORM_CHUNK), :] + b_ref[...]
        mu = jnp.mean(acc, axis=-1, keepdims=True)
        d = acc - mu
        var = jnp.mean(d * d, axis=-1, keepdims=True)
        un = d * lax.rsqrt(var + EPS) * lg_ref[...] + lb_ref[...]
        o_ref[pl.ds(base, NORM_CHUNK), :] = (un * _sigmoid(un)).astype(BF16)
        return carry

    lax.fori_loop(0, T // NORM_CHUNK, norm_chunk, 0)


def _conv(u, layer, conv_w, conv_b, ln_g, ln_b, seq_len):
    n = u.shape[0]
    T = seq_len
    rows = T + 2 * CONV_PAD - 8
    vec = pl.BlockSpec((None, 1, D_CONV), lambda b: (layer, 0, 0))
    return pl.pallas_call(
        _conv_kernel,
        grid=(n // T,),
        in_specs=[
            pl.BlockSpec((T, D_CONV), lambda b: (b, 0)),
            pl.BlockSpec((None, CONV_WIDTH, D_CONV), lambda b: (layer, 0, 0)),
            vec, vec, vec,
        ],
        out_specs=pl.BlockSpec((T, D_CONV), lambda b: (b, 0)),
        out_shape=jax.ShapeDtypeStruct((n, D_CONV), BF16),
        scratch_shapes=[pltpu.VMEM((T + 2 * CONV_PAD, D_CONV), F32),
                        pltpu.VMEM((8, rows, D_CONV), F32),
                        pltpu.VMEM((CONV_WIDTH, 8, D_CONV), F32),
                        pltpu.VMEM((T, D_CONV), F32)],
        compiler_params=_params("parallel"),
        name="conv",
    )(u, conv_w, conv_b, ln_g, ln_b)


def _attn_kernel(*refs, lam_init, has_ctx):
    lam_ref, sg_ref, q_ref, k_ref, v_ref = refs[:5]
    T = k_ref.shape[0]
    if has_ctx:
        ck_ref, cv_ref, o_ref, vals_ref, keys_ref = refs[5:]
    else:
        o_ref, vals_ref = refs[5:]
        keys_ref = k_ref

    @pl.when(pl.program_id(1) == 0)
    def _():
        vals_ref[...] = jnp.ones(vals_ref.shape, BF16)
        for h in range(N_HEADS):
            vals_ref[0:T, 2 * h * V_DIM:(2 * h + 1) * V_DIM] = v_ref[:, h * V_DIM:(h + 1) * V_DIM]
        if has_ctx:
            keys_ref[0:T, :] = k_ref[...]
            keys_ref[T:, :] = ck_ref[...].astype(BF16)
            cv = cv_ref[...].astype(BF16)
            for h in range(N_HEADS):
                vals_ref[T:, 2 * h * V_DIM:(2 * h + 1) * V_DIM] = cv[:, h * V_DIM:(h + 1) * V_DIM]

    l = lam_ref[...]
    lam = (jnp.exp(jnp.sum(l[0:1, :] * l[1:2, :], axis=-1, keepdims=True))
           - jnp.exp(jnp.sum(l[2:3, :] * l[3:4, :], axis=-1, keepdims=True)) + lam_init)
    tq = q_ref.shape[0]
    lane = lax.broadcasted_iota(jnp.int32, (tq, V_DIM), 1)
    nt = (((1,), (1,)), ((), ()))
    for h in range(N_HEADS):
        sl = slice(h * V_DIM, (h + 1) * V_DIM)
        qh = q_ref[:, sl]
        kh = keys_ref[:, sl]
        heads = []
        for c in range(2):
            keep = (lane < HEAD_DIM) if c == 0 else (lane >= HEAD_DIM)
            qc = jnp.where(keep, qh, jnp.zeros_like(qh))
            s = lax.dot_general(qc, kh, nt, preferred_element_type=F32)
            e = jnp.exp2(s - jnp.max(s, axis=-1, keepdims=True)).astype(BF16)
            pv = jnp.dot(e, vals_ref[:, 2 * h * V_DIM:(2 * h + 2) * V_DIM], preferred_element_type=F32)
            heads.append(pv[:, :V_DIM] / pv[:, V_DIM:])
        o = heads[0] - lam * heads[1]
        o = o * lax.rsqrt(jnp.mean(o * o, axis=-1, keepdims=True) + EPS)
        o = o * sg_ref[...] * (1.0 - lam_init)
        o_ref[:, sl] = o.astype(BF16)


def _attention(q, k, v, lam_vecs, subln_g, layer, seq_len, ctx):
    n = q.shape[0]
    T = seq_len
    tq = TOKEN_TILE
    nq = T // tq
    lam_init = 0.8 - 0.6 * math.exp(-0.3 * layer)
    has_ctx = ctx is not None
    kv = pl.BlockSpec((T, D_QK), lambda b, i: (b, 0))
    in_specs = [
        pl.BlockSpec((None, 4, HEAD_DIM), lambda b, i: (layer, 0, 0)),
        pl.BlockSpec((None, 1, V_DIM), lambda b, i: (layer, 0, 0)),
        pl.BlockSpec((tq, D_QK), lambda b, i: (b * nq + i, 0)),
        kv, kv,
    ]
    args = [lam_vecs, subln_g, q, k, v]
    past = ctx[0].shape[2] if has_ctx else 0
    scratch = [pltpu.VMEM((T + past, 2 * D_ATTN), BF16)]
    if has_ctx:
        cspec = pl.BlockSpec((None, None, past, D_QK), lambda b, i: (b, layer, 0, 0))
        in_specs += [cspec, cspec]
        args += list(ctx)
        scratch.append(pltpu.VMEM((T + past, D_QK), BF16))
    return pl.pallas_call(
        functools.partial(_attn_kernel, lam_init=lam_init, has_ctx=has_ctx),
        grid=(n // T, nq),
        in_specs=in_specs,
        out_specs=pl.BlockSpec((tq, D_ATTN), lambda b, i: (b * nq + i, 0)),
        out_shape=jax.ShapeDtypeStruct((n, D_ATTN), BF16),
        scratch_shapes=scratch,
        compiler_params=_params("parallel", "arbitrary"),
        name="attention",
    )(*args)


def _outproj_kernel(x_ref, mc_ref, ma_ref, mod_ref, w_ref, g_ref, wr_ref, xn_ref, h2_ref, aff_ref):
    acc = jnp.dot(mc_ref[...], w_ref[0:D_CONV, :], preferred_element_type=F32)
    acc = acc + jnp.dot(ma_ref[...], w_ref[D_CONV:D_MIX, :], preferred_element_type=F32)
    x = x_ref[...] + mod_ref[2:3, :] * acc
    xn_ref[...] = x
    ms = jnp.mean(x * x, axis=-1, keepdims=True)
    y = x * lax.rsqrt(ms + EPS) * g_ref[...]
    h2 = (y * (1.0 + mod_ref[4:5, :]) + mod_ref[3:4, :]).astype(BF16)
    h2_ref[...] = h2
    logits = jnp.dot(h2, wr_ref[...], preferred_element_type=F32)
    lane = lax.broadcasted_iota(jnp.int32, logits.shape, 1)
    logits = jnp.where(lane < N_EXPERTS, logits, -jnp.inf)
    e = jnp.exp(logits - jnp.max(logits, axis=-1, keepdims=True))
    aff = e / jnp.sum(e, axis=-1, keepdims=True)
    aff_ref[...] = aff.T[0:N_EXPERTS, :]


def _outproj(x, mix_conv, mix_attn, mods, layer, w_out_b, norm_g, w_router_b, mod_row, seq_len):
    n = x.shape[0]
    tm = _row_tile(seq_len)
    tiles = seq_len // tm
    row = lambda i: (i, 0)
    return pl.pallas_call(
        _outproj_kernel,
        grid=(n // tm,),
        in_specs=[
            pl.BlockSpec((tm, D_MODEL), row),
            pl.BlockSpec((tm, D_CONV), row),
            pl.BlockSpec((tm, D_ATTN), row),
            pl.BlockSpec((None, None, 6, D_MODEL), lambda i: (layer, mod_row(i), 0, 0)),
            pl.BlockSpec((None, D_MIX, D_MODEL), lambda i: (layer, 0, 0)),
            pl.BlockSpec((None, 1, D_MODEL), lambda i: (layer, 0, 0)),
            pl.BlockSpec((None, D_MODEL, LANES), lambda i: (layer, 0, 0)),
        ],
        out_specs=[pl.BlockSpec((tm, D_MODEL), row), pl.BlockSpec((tm, D_MODEL), row),
                   pl.BlockSpec((N_EXPERTS, tm), lambda i: (i // tiles, i % tiles))],
        out_shape=[jax.ShapeDtypeStruct((n, D_MODEL), F32), jax.ShapeDtypeStruct((n, D_MODEL), BF16),
                   jax.ShapeDtypeStruct((n // seq_len * N_EXPERTS, seq_len), F32)],
        compiler_params=_params("parallel"),
        name="outproj",
    )(x, mix_conv, mix_attn, mods, w_out_b, norm_g, w_router_b)


def _select_kernel(a_ref, tri_ref, pos_ref, posc_ref, *, cap):
    a = a_ref[...]
    rows = a.shape[0]
    capf = float(cap)

    def count_ge(v):
        return jnp.sum(jnp.where(a >= v, 1.0, 0.0), axis=-1, keepdims=True)

    hi = jnp.full((rows, 1), 2.0, F32)
    for j in range(6, -1, -1):
        cand = hi * (2.0 ** -(2 ** j))
        hi = jnp.where(count_ge(cand) < capf, cand, hi)
    lo = hi * 0.5
    step = lo * 0.5
    for _ in range(MANTISSA_BITS):
        cand = lo + step
        lo = jnp.where(count_ge(cand) >= capf, cand, lo)
        step = step * 0.5
    upper = jnp.where(step > 0.0, lo + step * 2.0, hi)
    above = jnp.where(a >= upper, 1.0, 0.0)
    tie = jnp.where(a >= lo, 1.0, 0.0) - above
    need = capf - jnp.sum(above, axis=-1, keepdims=True)
    tri = tri_ref[...]
    tie_rank = jnp.dot(tie.astype(BF16), tri, preferred_element_type=F32)
    sel = above + tie * jnp.where(tie_rank <= need, 1.0, 0.0)
    slot = jnp.dot(sel.astype(BF16), tri, preferred_element_type=F32) - 1.0
    pos = jnp.where(sel > 0.5, slot, -1.0)
    pos_ref[...] = pos
    posc_ref[...] = pos.T


def _select(aff_t, tri, cap):
    rows, T = aff_t.shape
    return pl.pallas_call(
        functools.partial(_select_kernel, cap=cap),
        grid=(1,),
        in_specs=[pl.BlockSpec((rows, T), lambda i: (0, 0)), pl.BlockSpec((T, T), lambda i: (0, 0))],
        out_specs=[pl.BlockSpec((rows, T), lambda i: (0, 0)), pl.BlockSpec((T, rows), lambda i: (0, 0))],
        out_shape=[jax.ShapeDtypeStruct((rows, T), F32), jax.ShapeDtypeStruct((T, rows), F32)],
        compiler_params=_params("arbitrary"),
        name="select",
    )(aff_t, tri)


def _gather_kernel(pos_ref, a_ref, h2_ref, xs_ref, gs_ref, oh_ref, *, cap):
    T = h2_ref.shape[0]
    slot = lax.broadcasted_iota(jnp.int32, (cap, T), 0).astype(F32)
    for e in range(N_EXPERTS):
        onehot = jnp.where(pos_ref[e:e + 1, :] == slot, 1.0, 0.0)
        oh_ref[e * cap:(e + 1) * cap, :] = onehot.astype(BF16)
        gs_ref[e] = jnp.sum(onehot * a_ref[e:e + 1, :], axis=-1, keepdims=True)
    xs = jnp.dot(oh_ref[...], h2_ref[...], preferred_element_type=F32)
    xs_ref[...] = xs.reshape(N_EXPERTS, cap, D_MODEL).astype(BF16)


def _gather(pos, aff_t, h2, seq_len):
    T = seq_len
    nreq = h2.shape[0] // T
    cap = EC_FACTOR * T // N_EXPERTS
    rows = pl.BlockSpec((N_EXPERTS, T), lambda b: (b, 0))
    return pl.pallas_call(
        functools.partial(_gather_kernel, cap=cap),
        grid=(nreq,),
        in_specs=[rows, rows, pl.BlockSpec((T, D_MODEL), lambda b: (b, 0))],
        out_specs=[pl.BlockSpec((N_EXPERTS, cap, D_MODEL), lambda b: (0, b, 0)),
                   pl.BlockSpec((N_EXPERTS, cap, 1), lambda b: (0, b, 0))],
        out_shape=[jax.ShapeDtypeStruct((N_EXPERTS, nreq * cap, D_MODEL), BF16),
                   jax.ShapeDtypeStruct((N_EXPERTS, nreq * cap, 1), F32)],
        scratch_shapes=[pltpu.VMEM((N_EXPERTS * cap, T), BF16)],
        compiler_params=_params("parallel"),
        name="gather",
    )(pos, aff_t, h2)


def _expert_kernel(xp_ref, xs_ref, gp_ref, gsm_ref, wg_ref, wu_ref, wd_ref, yp_ref, ys_ref,
                   accp_ref, accs_ref):
    j = pl.program_id(1)
    wg = wg_ref[...].astype(BF16)
    wu = wu_ref[...].astype(BF16)
    wd = wd_ref[...].astype(BF16)
    for x_ref, g_ref, y_ref, acc_ref in ((xp_ref, gp_ref, yp_ref, accp_ref),
                                         (xs_ref, gsm_ref, ys_ref, accs_ref)):
        for t in range(x_ref.shape[0] // EXPERT_ROW_TILE):
            rows = slice(t * EXPERT_ROW_TILE, (t + 1) * EXPERT_ROW_TILE)
            x = x_ref[rows, :]
            hg = jnp.dot(x, wg, preferred_element_type=F32)
            hu = jnp.dot(x, wu, preferred_element_type=F32)
            hid = (hg * _sigmoid(hg) * hu).astype(BF16)
            y = jnp.dot(hid, wd, preferred_element_type=F32)

            @pl.when(j == 0)
            def _():
                acc_ref[rows, :] = y

            @pl.when(j == EXPERT_SPLIT - 1)
            def _():
                y_ref[rows, :] = ((acc_ref[rows, :] + y) * g_ref[rows, :]).astype(BF16)


def _experts(xs_p, xs_s, gs_p, gs_s, layer, w_gate, w_up, w_down):
    np_, ns_ = xs_p.shape[1], xs_s.shape[1]
    fe = D_EXPERT // EXPERT_SPLIT
    slots = lambda n, w: pl.BlockSpec((None, n, w), lambda e, j: (e, 0, 0))
    return pl.pallas_call(
        _expert_kernel,
        grid=(N_EXPERTS, EXPERT_SPLIT),
        in_specs=[
            slots(np_, D_MODEL), slots(ns_, D_MODEL), slots(np_, 1), slots(ns_, 1),
            pl.BlockSpec((None, None, D_MODEL, fe), lambda e, j: (layer, e, 0, j)),
            pl.BlockSpec((None, None, D_MODEL, fe), lambda e, j: (layer, e, 0, j)),
            pl.BlockSpec((None, None, fe, D_MODEL), lambda e, j: (layer, e, j, 0)),
        ],
        out_specs=[slots(np_, D_MODEL), slots(ns_, D_MODEL)],
        out_shape=[jax.ShapeDtypeStruct(xs_p.shape, BF16), jax.ShapeDtypeStruct(xs_s.shape, BF16)],
        scratch_shapes=[pltpu.VMEM((np_, D_MODEL), F32), pltpu.VMEM((ns_, D_MODEL), F32)],
        compiler_params=_params("parallel", "arbitrary"),
        name="experts",
    )(xs_p, xs_s, gs_p, gs_s, w_gate, w_up, w_down)


def _combine_kernel(*refs, cap, final):
    xn_ref, pc_ref, ye_ref, mod_ref, ex_ref = refs[:5]
    if final:
        fg_ref = refs[5]
    o_ref = refs[-1]
    pos = jnp.dot(pc_ref[...].astype(BF16), ex_ref[...], preferred_element_type=F32)
    lane = lax.broadcasted_iota(jnp.int32, pos.shape, 1)
    slot = (lane % cap).astype(F32)
    onehot = jnp.where(pos == slot, 1.0, 0.0).astype(BF16)
    ye = ye_ref[...].reshape(N_EXPERTS * cap, D_MODEL)
    moe = jnp.dot(onehot, ye, preferred_element_type=F32)
    x = xn_ref[...] + mod_ref[5:6, :] * moe
    if final:
        x = x * lax.rsqrt(jnp.mean(x * x, axis=-1, keepdims=True) + EPS) * fg_ref[...]
    o_ref[...] = x


def _combine(xn, pos_col, ye, mods, layer, mod_row, seq_len, final_g):
    n = xn.shape[0]
    T = seq_len
    tm = _row_tile(seq_len)
    tiles = T // tm
    cap = EC_FACTOR * T // N_EXPERTS
    final = final_g is not None
    group = LANES // N_EXPERTS
    src = jnp.arange(group)[:, None, None] * N_EXPERTS + (jnp.arange(N_EXPERTS * cap) // cap)[None, None, :]
    expander = (jnp.arange(LANES)[None, :, None] == src).astype(BF16)
    in_specs = [
        pl.BlockSpec((tm, D_MODEL), lambda i: (i, 0)),
        pl.BlockSpec((tm, LANES), lambda i: (i % tiles, i // tiles // group)),
        pl.BlockSpec((N_EXPERTS, cap, D_MODEL), lambda i: (0, i // tiles, 0)),
        pl.BlockSpec((None, None, 6, D_MODEL), lambda i: (layer, mod_row(i), 0, 0)),
        pl.BlockSpec((None, LANES, N_EXPERTS * cap), lambda i: (i // tiles % group, 0, 0)),
    ]
    args = [xn, pos_col, ye, mods, expander]
    if final:
        in_specs.append(pl.BlockSpec((1, D_MODEL), lambda i: (0, 0)))
        args.append(final_g)
    return pl.pallas_call(
        functools.partial(_combine_kernel, cap=cap, final=final),
        grid=(n // tm,),
        in_specs=in_specs,
        out_specs=pl.BlockSpec((tm, D_MODEL), lambda i: (i, 0)),
        out_shape=jax.ShapeDtypeStruct((n, D_MODEL), F32),
        compiler_params=_params("parallel"),
        name="combine",
    )(*args)


def _rope_tables(T):
    rows = T // GRID_W
    row = jnp.repeat(jnp.arange(rows, dtype=F32), GRID_W)
    col = jnp.tile(jnp.arange(GRID_W, dtype=F32), rows)
    inv = ROPE_BASE ** (-jnp.arange(0, ROPE_AXIS, 2, dtype=F32) / ROPE_AXIS)
    ar = row[:, None] * inv
    ac = col[:, None] * inv
    cos = jnp.concatenate([jnp.cos(ar), jnp.cos(ar), jnp.cos(ac), jnp.cos(ac)], axis=-1)
    sin = jnp.concatenate([-jnp.sin(ar), jnp.sin(ar), -jnp.sin(ac), jnp.sin(ac)], axis=-1)
    reps = D_QK // HEAD_DIM
    return jnp.tile(cos, (1, reps)), jnp.tile(sin, (1, reps))


def kernel(x_prompt, x_sample, c, cache_k, cache_v, c_ctx, norm1_g, w_mod, b_mod, w_in, conv_w, conv_b,
           conv_ln_g, conv_ln_b, lambda_q1, lambda_k1, lambda_q2, lambda_k2, subln_g, w_out, norm2_g,
           w_router, w_gate, w_up, w_down, final_norm_g):
    batch, seq, _ = x_prompt.shape
    dec_batch, dec_seq, _ = x_sample.shape
    past = cache_k.shape[2]
    assert 1 + dec_batch <= MOD_ROWS

    cvec = jnp.concatenate([c_ctx[None, :], c, jnp.zeros((MOD_ROWS - 1 - dec_batch, D_MODEL), F32)], axis=0)
    mods = _modulation(cvec, w_mod, b_mod)

    w_in_b = w_in.astype(BF16)
    w_out_b = w_out.astype(BF16)
    w_router_b = jnp.pad(w_router, ((0, 0), (0, 0), (0, LANES - N_EXPERTS))).astype(BF16)
    lam_vecs = jnp.stack([lambda_q1, lambda_k1, lambda_q2, lambda_k2], axis=1)
    vec = lambda a: a.reshape(DEPTH, 1, a.shape[-1])
    norm1_v, norm2_v, subln_v = vec(norm1_g), vec(norm2_g), vec(subln_g)
    conv_b_v, ln_g_v, ln_b_v = vec(conv_b), vec(conv_ln_g), vec(conv_ln_b)
    ctx_k = cache_k.reshape(dec_batch, DEPTH, past, D_QK)
    ctx_v = cache_v.reshape(dec_batch, DEPTH, past, D_ATTN)
    rope_tabs = _rope_tables(dec_seq)
    final_g = final_norm_g.reshape(1, D_MODEL)
    tri_p = (jnp.arange(seq)[:, None] <= jnp.arange(seq)[None, :]).astype(BF16)
    tri_s = (jnp.arange(dec_seq)[:, None] <= jnp.arange(dec_seq)[None, :]).astype(BF16)

    prompt_row = lambda i: 0
    tiles_per_sample = dec_seq // _row_tile(dec_seq)
    sample_row = lambda i: 1 + i // tiles_per_sample

    xp = x_prompt.reshape(batch * seq, D_MODEL)
    xs = x_sample.reshape(dec_batch * dec_seq, D_MODEL)
    new_k, new_v = [], []
    for l in range(DEPTH):
        last = l == DEPTH - 1
        up, qp, kp, vp, kf, vf = _inproj(xp, mods, l, norm1_v, w_in_b, prompt_row, None, seq, True)
        us, qs, ks, vs = _inproj(xs, mods, l, norm1_v, w_in_b, sample_row, rope_tabs, dec_seq, False)
        new_k.append(kf.reshape(batch, seq, N_HEADS, 2, HEAD_DIM))
        new_v.append(vf.reshape(batch, seq, N_HEADS, V_DIM))
        cp = _conv(up, l, conv_w, conv_b_v, ln_g_v, ln_b_v, seq)
        cs = _conv(us, l, conv_w, conv_b_v, ln_g_v, ln_b_v, dec_seq)
        op = _attention(qp, kp, vp, lam_vecs, subln_v, l, seq, None)
        os_ = _attention(qs, ks, vs, lam_vecs, subln_v, l, dec_seq, (ctx_k, ctx_v))
        xnp, h2p, affp = _outproj(xp, cp, op, mods, l, w_out_b, norm2_v, w_router_b, prompt_row, seq)
        xns, h2s, affs = _outproj(xs, cs, os_, mods, l, w_out_b, norm2_v, w_router_b, sample_row, dec_seq)
        pos_p, rank_p = _select(affp, tri_p, EC_FACTOR * seq // N_EXPERTS)
        pos_s, rank_s = _select(affs, tri_s, EC_FACTOR * dec_seq // N_EXPERTS)
        slots_p, gates_p = _gather(pos_p, affp, h2p, seq)
        slots_s, gates_s = _gather(pos_s, affs, h2s, dec_seq)
        ye_p, ye_s = _experts(slots_p, slots_s, gates_p, gates_s, l, w_gate, w_up, w_down)
        xp = _combine(xnp, rank_p, ye_p, mods, l, prompt_row, seq, final_g if last else None)
        xs = _combine(xns, rank_s, ye_s, mods, l, sample_row, dec_seq, final_g if last else None)

    y_prompt = xp.reshape(batch, seq, D_MODEL)
    y_sample = xs.reshape(dec_batch, dec_seq, D_MODEL)
    return (y_prompt, y_sample, jnp.stack(new_k, axis=1), jnp.stack(new_v, axis=1))
```

```python
import functools
import math

import jax
import jax.numpy as jnp
from jax import lax
from jax.experimental import pallas as pl
from jax.experimental.pallas import tpu as pltpu

D_MODEL = 1024
DEPTH = 4
GRID_W = 64
D_CONV = 512
CONV_WIDTH = 31
N_HEADS = 4
HEAD_DIM = 64
V_DIM = 2 * HEAD_DIM
D_QK = N_HEADS * 2 * HEAD_DIM
D_ATTN = N_HEADS * V_DIM
D_MIX = D_CONV + D_ATTN
D_IN = 2 * D_CONV + 2 * D_QK + D_ATTN
N_EXPERTS = 16
EC_FACTOR = 2
D_EXPERT = 1024
ROPE_BASE = 10000.0
ROPE_AXIS = HEAD_DIM // 2
EPS = 1e-6
LOG2_E = math.log2(math.e)

LANES = 128
MOD_ROWS = 16
TOKEN_TILE = 256
ROW_TILE = 1024
SUB_ROWS = 256
CONV_CHUNK = 64
NORM_CHUNK = 256
CONV_PAD = 16
EXPERT_ROW_TILE = 512
MANTISSA_BITS = 23
VMEM_LIMIT = 56 * 1024 * 1024
EXPERT_VMEM_LIMIT = 60 * 1024 * 1024

F32 = jnp.float32
BF16 = jnp.bfloat16


def _sigmoid(x):
    return 1.0 / (1.0 + jnp.exp(-x))


def _params(*sem):
    return pltpu.CompilerParams(dimension_semantics=sem, vmem_limit_bytes=VMEM_LIMIT)


def _mod_kernel(c_ref, w_ref, b_ref, o_ref):
    c = c_ref[...]
    s = c * _sigmoid(c)
    o_ref[...] = jnp.dot(s.astype(BF16), w_ref[...].astype(BF16),
                         preferred_element_type=F32) + b_ref[...]


def _modulation(cvec, w_mod, b_mod):
    tn = 1536
    out = pl.pallas_call(
        _mod_kernel,
        grid=(DEPTH, 6 * D_MODEL // tn),
        in_specs=[
            pl.BlockSpec((MOD_ROWS, D_MODEL), lambda l, j: (0, 0)),
            pl.BlockSpec((None, D_MODEL, tn), lambda l, j: (l, 0, j)),
            pl.BlockSpec((None, 1, tn), lambda l, j: (l, 0, j)),
        ],
        out_specs=pl.BlockSpec((None, MOD_ROWS, tn), lambda l, j: (l, 0, j)),
        out_shape=jax.ShapeDtypeStruct((DEPTH, MOD_ROWS, 6 * D_MODEL), F32),
        compiler_params=_params("parallel", "parallel"),
        name="modulation",
    )(cvec, w_mod, b_mod.reshape(DEPTH, 1, 6 * D_MODEL))
    return out.reshape(DEPTH, MOD_ROWS, 6, D_MODEL)


def _inproj_kernel(*refs, rope, cache_out):
    x_ref, mod_ref, g_ref, w_ref = refs[:4]
    refs = refs[4:]
    if rope:
        cos_ref, sin_ref = refs[:2]
        refs = refs[2:]
    u_ref, q_ref, k_ref, v_ref = refs[:4]
    for t in range(x_ref.shape[0] // SUB_ROWS):
        rows = slice(t * SUB_ROWS, (t + 1) * SUB_ROWS)
        x = x_ref[rows, :]
        ms = jnp.mean(x * x, axis=-1, keepdims=True)
        y = x * lax.rsqrt(ms + EPS) * g_ref[...]
        h = y * (1.0 + mod_ref[1:2, :]) + mod_ref[0:1, :]
        z = jnp.dot(h.astype(BF16), w_ref[...], preferred_element_type=F32)
        u_ref[rows, :] = z[:, :D_CONV] * _sigmoid(z[:, D_CONV:2 * D_CONV])
        q = z[:, 2 * D_CONV:2 * D_CONV + D_QK]
        k = z[:, 2 * D_CONV + D_QK:2 * D_CONV + 2 * D_QK]
        v = z[:, 2 * D_CONV + 2 * D_QK:]
        if cache_out:
            kf_ref, vf_ref = refs[4:6]
            kf_ref[rows, :] = k
            vf_ref[rows, :] = v
        if rope:
            lane = lax.broadcasted_iota(jnp.int32, q.shape, 1)
            first = (lane % ROPE_AXIS) < (ROPE_AXIS // 2)
            cos = cos_ref[rows, :]
            sin = sin_ref[rows, :]

            def rot(a):
                partner = jnp.where(first, pltpu.roll(a, D_QK - ROPE_AXIS // 2, 1),
                                    pltpu.roll(a, ROPE_AXIS // 2, 1))
                return a * cos + partner * sin

            q = rot(q)
            k = rot(k)
        q_ref[rows, :] = (q * (HEAD_DIM ** -0.5 * LOG2_E)).astype(BF16)
        k_ref[rows, :] = k.astype(BF16)
        v_ref[rows, :] = v.astype(BF16)


def _inproj(x, mods, layer, norm_g, w_in_b, mod_row, rope_tabs, seq_len, cache_out):
    n = x.shape[0]
    tm = ROW_TILE
    assert rope_tabs is None or seq_len % tm == 0
    tiles_per_seq = max(1, seq_len // tm)
    rope = rope_tabs is not None
    in_specs = [
        pl.BlockSpec((tm, D_MODEL), lambda i: (i, 0)),
        pl.BlockSpec((None, None, 6, D_MODEL), lambda i: (layer, mod_row(i), 0, 0)),
        pl.BlockSpec((None, 1, D_MODEL), lambda i: (layer, 0, 0)),
        pl.BlockSpec((None, D_MODEL, D_IN), lambda i: (layer, 0, 0)),
    ]
    args = [x, mods, norm_g, w_in_b]
    if rope:
        in_specs += [pl.BlockSpec((tm, D_QK), lambda i: (i % tiles_per_seq, 0))] * 2
        args += list(rope_tabs)
    half = pl.BlockSpec((tm, D_CONV), lambda i: (i, 0))
    out_specs = [half, half, half, half]
    out_shape = [jax.ShapeDtypeStruct((n, D_CONV), F32)] + [jax.ShapeDtypeStruct((n, D_QK), BF16)] * 3
    if cache_out:
        out_specs += [half, half]
        out_shape += [jax.ShapeDtypeStruct((n, D_QK), F32)] * 2
    return pl.pallas_call(
        functools.partial(_inproj_kernel, rope=rope, cache_out=cache_out),
        grid=(n // tm,),
        in_specs=in_specs,
        out_specs=out_specs,
        out_shape=out_shape,
        compiler_params=_params("parallel"),
        name="inproj",
    )(*args)


def _conv_kernel(u_ref, w_ref, b_ref, lg_ref, lb_ref, o_ref, pad_ref, sh_ref, wb_ref, acc_ref):
    T = u_ref.shape[0]
    tail = CONV_PAD + T
    for k in range(CONV_WIDTH):
        wb_ref[k] = jnp.broadcast_to(w_ref[k:k + 1, :], (8, D_CONV))
    pad_ref[0:CONV_PAD, :] = jnp.zeros((CONV_PAD, D_CONV), F32)
    pad_ref[tail:tail + CONV_PAD, :] = jnp.zeros((CONV_PAD, D_CONV), F32)
    pad_ref[CONV_PAD:tail, :] = u_ref[...]
    rows = sh_ref.shape[1]
    for r in range(8):
        sh_ref[r] = pad_ref[r:r + rows, :]
    first = CONV_PAD - CONV_WIDTH // 2

    groups = CONV_CHUNK // 8

    for q in range(D_CONV // LANES):
        lanes = slice(q * LANES, (q + 1) * LANES)
        w = [wb_ref[k, :, lanes] for k in range(CONV_WIDTH)]

        def taps_chunk(i, carry, lanes=lanes, w=w):
            base = pl.multiple_of(i * CONV_CHUNK, CONV_CHUNK)
            acc = jnp.zeros((groups, 8, LANES), F32)
            for r in range(8):
                taps = [(k, (first + k) // 8) for k in range(CONV_WIDTH) if (first + k) % 8 == r]
                m0 = min(m for _, m in taps)
                span = max(m for _, m in taps) - m0 + groups
                x = sh_ref[r, pl.ds(base + 8 * m0, 8 * span), lanes].reshape(span, 8, LANES)
                for k, m in taps:
                    acc = acc + w[k] * x[m - m0:m - m0 + groups]
            acc_ref[pl.ds(base, CONV_CHUNK), lanes] = acc.reshape(CONV_CHUNK, LANES)
            return carry

        lax.fori_loop(0, T // CONV_CHUNK, taps_chunk, 0)

    def norm_chunk(i, carry):
        base = pl.multiple_of(i * NORM_CHUNK, NORM_CHUNK)
        acc = acc_ref[pl.ds(base, NORM_CHUNK), :] + b_ref[...]
        mu = jnp.mean(acc, axis=-1, keepdims=True)
        d = acc - mu
        var = jnp.mean(d * d, axis=-1, keepdims=True)
        un = d * lax.rsqrt(var + EPS) * lg_ref[...] + lb_ref[...]
        o_ref[pl.ds(base, NORM_CHUNK), :] = (un * _sigmoid(un)).astype(BF16)
        return carry

    lax.fori_loop(0, T // NORM_CHUNK, norm_chunk, 0)


def _conv(u, layer, conv_w, conv_b, ln_g, ln_b, seq_len):
    n = u.shape[0]
    T = seq_len
    rows = T + 2 * CONV_PAD - 8
    vec = pl.BlockSpec((None, 1, D_CONV), lambda b: (layer, 0, 0))
    return pl.pallas_call(
        _conv_kernel,
        grid=(n // T,),
        in_specs=[
            pl.BlockSpec((T, D_CONV), lambda b: (b, 0)),
            pl.BlockSpec((None, CONV_WIDTH, D_CONV), lambda b: (layer, 0, 0)),
            vec, vec, vec,
        ],
        out_specs=pl.BlockSpec((T, D_CONV), lambda b: (b, 0)),
        out_shape=jax.ShapeDtypeStruct((n, D_CONV), BF16),
        scratch_shapes=[pltpu.VMEM((T + 2 * CONV_PAD, D_CONV), F32),
                        pltpu.VMEM((8, rows, D_CONV), F32),
                        pltpu.VMEM((CONV_WIDTH, 8, D_CONV), F32),
                        pltpu.VMEM((T, D_CONV), F32)],
        compiler_params=_params("parallel"),
        name="conv",
    )(u, conv_w, conv_b, ln_g, ln_b)


def _attn_kernel(*refs, lam_init, has_ctx):
    lam_ref, sg_ref, q_ref, k_ref, v_ref = refs[:5]
    T = k_ref.shape[0]
    if has_ctx:
        ck_ref, cv_ref, o_ref, vals_ref, keys_ref = refs[5:]
    else:
        o_ref, vals_ref = refs[5:]
        keys_ref = k_ref

    @pl.when(pl.program_id(1) == 0)
    def _():
        vals_ref[...] = jnp.ones(vals_ref.shape, BF16)
        for h in range(N_HEADS):
            vals_ref[0:T, 2 * h * V_DIM:(2 * h + 1) * V_DIM] = v_ref[:, h * V_DIM:(h + 1) * V_DIM]
        if has_ctx:
            keys_ref[0:T, :] = k_ref[...]
            keys_ref[T:, :] = ck_ref[...].astype(BF16)
            cv = cv_ref[...].astype(BF16)
            for h in range(N_HEADS):
                vals_ref[T:, 2 * h * V_DIM:(2 * h + 1) * V_DIM] = cv[:, h * V_DIM:(h + 1) * V_DIM]

    l = lam_ref[...]
    lam = (jnp.exp(jnp.sum(l[0:1, :] * l[1:2, :], axis=-1, keepdims=True))
           - jnp.exp(jnp.sum(l[2:3, :] * l[3:4, :], axis=-1, keepdims=True)) + lam_init)
    tq = q_ref.shape[0]
    lane = lax.broadcasted_iota(jnp.int32, (tq, V_DIM), 1)
    nt = (((1,), (1,)), ((), ()))
    for h in range(N_HEADS):
        sl = slice(h * V_DIM, (h + 1) * V_DIM)
        qh = q_ref[:, sl]
        kh = keys_ref[:, sl]
        heads = []
        for c in range(2):
            keep = (lane < HEAD_DIM) if c == 0 else (lane >= HEAD_DIM)
            qc = jnp.where(keep, qh, jnp.zeros_like(qh))
            s = lax.dot_general(qc, kh, nt, preferred_element_type=F32)
            e = jnp.exp2(s - jnp.max(s, axis=-1, keepdims=True)).astype(BF16)
            pv = jnp.dot(e, vals_ref[:, 2 * h * V_DIM:(2 * h + 2) * V_DIM], preferred_element_type=F32)
            heads.append(pv[:, :V_DIM] / pv[:, V_DIM:])
        o = heads[0] - lam * heads[1]
        o = o * lax.rsqrt(jnp.mean(o * o, axis=-1, keepdims=True) + EPS)
        o = o * sg_ref[...] * (1.0 - lam_init)
        o_ref[:, sl] = o.astype(BF16)


def _attention(q, k, v, lam_vecs, subln_g, layer, seq_len, ctx):
    n = q.shape[0]
    T = seq_len
    tq = TOKEN_TILE
    nq = T // tq
    lam_init = 0.8 - 0.6 * math.exp(-0.3 * layer)
    has_ctx = ctx is not None
    kv = pl.BlockSpec((T, D_QK), lambda b, i: (b, 0))
    in_specs = [
        pl.BlockSpec((None, 4, HEAD_DIM), lambda b, i: (layer, 0, 0)),
        pl.BlockSpec((None, 1, V_DIM), lambda b, i: (layer, 0, 0)),
        pl.BlockSpec((tq, D_QK), lambda b, i: (b * nq + i, 0)),
        kv, kv,
    ]
    args = [lam_vecs, subln_g, q, k, v]
    past = ctx[0].shape[2] if has_ctx else 0
    scratch = [pltpu.VMEM((T + past, 2 * D_ATTN), BF16)]
    if has_ctx:
        cspec = pl.BlockSpec((None, None, past, D_QK), lambda b, i: (b, layer, 0, 0))
        in_specs += [cspec, cspec]
        args += list(ctx)
        scratch.append(pltpu.VMEM((T + past, D_QK), BF16))
    return pl.pallas_call(
        functools.partial(_attn_kernel, lam_init=lam_init, has_ctx=has_ctx),
        grid=(n // T, nq),
        in_specs=in_specs,
        out_specs=pl.BlockSpec((tq, D_ATTN), lambda b, i: (b * nq + i, 0)),
        out_shape=jax.ShapeDtypeStruct((n, D_ATTN), BF16),
        scratch_shapes=scratch,
        compiler_params=_params("parallel", "arbitrary"),
        name="attention",
    )(*args)


def _outproj_kernel(x_ref, mc_ref, ma_ref, mod_ref, w_ref, g_ref, wr_ref, xn_ref, h2_ref, aff_ref, *, seq_len):
    for t in range(x_ref.shape[0] // SUB_ROWS):
        rows = slice(t * SUB_ROWS, (t + 1) * SUB_ROWS)
        req, off = divmod(t * SUB_ROWS, seq_len)
        acc = jnp.dot(mc_ref[rows, :], w_ref[0:D_CONV, :], preferred_element_type=F32)
        acc = acc + jnp.dot(ma_ref[rows, :], w_ref[D_CONV:D_MIX, :], preferred_element_type=F32)
        x = x_ref[rows, :] + mod_ref[2:3, :] * acc
        xn_ref[rows, :] = x
        ms = jnp.mean(x * x, axis=-1, keepdims=True)
        y = x * lax.rsqrt(ms + EPS) * g_ref[...]
        h2 = (y * (1.0 + mod_ref[4:5, :]) + mod_ref[3:4, :]).astype(BF16)
        h2_ref[rows, :] = h2
        logits = jnp.dot(h2, wr_ref[...], preferred_element_type=F32)
        lane = lax.broadcasted_iota(jnp.int32, logits.shape, 1)
        logits = jnp.where(lane < N_EXPERTS, logits, -jnp.inf)
        e = jnp.exp(logits - jnp.max(logits, axis=-1, keepdims=True))
        aff = e / jnp.sum(e, axis=-1, keepdims=True)
        aff_ref[req * N_EXPERTS:(req + 1) * N_EXPERTS, off:off + SUB_ROWS] = aff.T[0:N_EXPERTS, :]


def _outproj(x, mix_conv, mix_attn, mods, layer, w_out_b, norm_g, w_router_b, mod_row, seq_len):
    n = x.shape[0]
    tm = ROW_TILE
    assert tm % seq_len == 0 and seq_len % SUB_ROWS == 0
    reqs = tm // seq_len
    row = lambda i: (i, 0)
    return pl.pallas_call(
        functools.partial(_outproj_kernel, seq_len=seq_len),
        grid=(n // tm,),
        in_specs=[
            pl.BlockSpec((tm, D_MODEL), row),
            pl.BlockSpec((tm, D_CONV), row),
            pl.BlockSpec((tm, D_ATTN), row),
            pl.BlockSpec((None, None, 6, D_MODEL), lambda i: (layer, mod_row(i), 0, 0)),
            pl.BlockSpec((None, D_MIX, D_MODEL), lambda i: (layer, 0, 0)),
            pl.BlockSpec((None, 1, D_MODEL), lambda i: (layer, 0, 0)),
            pl.BlockSpec((None, D_MODEL, LANES), lambda i: (layer, 0, 0)),
        ],
        out_specs=[pl.BlockSpec((tm, D_MODEL), row), pl.BlockSpec((tm, D_MODEL), row),
                   pl.BlockSpec((reqs * N_EXPERTS, seq_len), row)],
        out_shape=[jax.ShapeDtypeStruct((n, D_MODEL), F32), jax.ShapeDtypeStruct((n, D_MODEL), BF16),
                   jax.ShapeDtypeStruct((n // seq_len * N_EXPERTS, seq_len), F32)],
        compiler_params=_params("parallel"),
        name="outproj",
    )(x, mix_conv, mix_attn, mods, w_out_b, norm_g, w_router_b)


def _select_kernel(a_ref, tri_ref, pos_ref, posc_ref, *, cap):
    a = a_ref[...]
    rows = a.shape[0]
    capf = float(cap)

    def count_ge(v):
        return jnp.sum(jnp.where(a >= v, 1.0, 0.0), axis=-1, keepdims=True)

    hi = jnp.full((rows, 1), 2.0, F32)
    for j in range(6, -1, -1):
        cand = hi * (2.0 ** -(2 ** j))
        hi = jnp.where(count_ge(cand) < capf, cand, hi)
    lo = hi * 0.5
    step = lo * 0.5
    for _ in range(MANTISSA_BITS):
        cand = lo + step
        lo = jnp.where(count_ge(cand) >= capf, cand, lo)
        step = step * 0.5
    upper = jnp.where(step > 0.0, lo + step * 2.0, hi)
    above = jnp.where(a >= upper, 1.0, 0.0)
    tie = jnp.where(a >= lo, 1.0, 0.0) - above
    need = capf - jnp.sum(above, axis=-1, keepdims=True)
    tri = tri_ref[...]
    tie_rank = jnp.dot(tie.astype(BF16), tri, preferred_element_type=F32)
    sel = above + tie * jnp.where(tie_rank <= need, 1.0, 0.0)
    slot = jnp.dot(sel.astype(BF16), tri, preferred_element_type=F32) - 1.0
    pos = jnp.where(sel > 0.5, slot, -1.0)
    pos_ref[...] = pos
    posc_ref[...] = pos.T


def _select(aff_t, tri, cap):
    rows, T = aff_t.shape
    return pl.pallas_call(
        functools.partial(_select_kernel, cap=cap),
        grid=(1,),
        in_specs=[pl.BlockSpec((rows, T), lambda i: (0, 0)), pl.BlockSpec((T, T), lambda i: (0, 0))],
        out_specs=[pl.BlockSpec((rows, T), lambda i: (0, 0)), pl.BlockSpec((T, rows), lambda i: (0, 0))],
        out_shape=[jax.ShapeDtypeStruct((rows, T), F32), jax.ShapeDtypeStruct((T, rows), F32)],
        compiler_params=_params("arbitrary"),
        name="select",
    )(aff_t, tri)


def _gather_kernel(pos_ref, a_ref, h2_ref, xs_ref, gs_ref, oh_ref, *, cap):
    T = h2_ref.shape[0]
    slot = lax.broadcasted_iota(jnp.int32, (cap, T), 0).astype(F32)
    for e in range(N_EXPERTS):
        onehot = jnp.where(pos_ref[e:e + 1, :] == slot, 1.0, 0.0)
        oh_ref[e * cap:(e + 1) * cap, :] = onehot.astype(BF16)
        gs_ref[e] = jnp.sum(onehot * a_ref[e:e + 1, :], axis=-1, keepdims=True)
    xs = jnp.dot(oh_ref[...], h2_ref[...], preferred_element_type=F32)
    xs_ref[...] = xs.reshape(N_EXPERTS, cap, D_MODEL).astype(BF16)


def _gather(pos, aff_t, h2, seq_len):
    T = seq_len
    nreq = h2.shape[0] // T
    cap = EC_FACTOR * T // N_EXPERTS
    rows = pl.BlockSpec((N_EXPERTS, T), lambda b: (b, 0))
    return pl.pallas_call(
        functools.partial(_gather_kernel, cap=cap),
        grid=(nreq,),
        in_specs=[rows, rows, pl.BlockSpec((T, D_MODEL), lambda b: (b, 0))],
        out_specs=[pl.BlockSpec((N_EXPERTS, cap, D_MODEL), lambda b: (0, b, 0)),
                   pl.BlockSpec((N_EXPERTS, cap, 1), lambda b: (0, b, 0))],
        out_shape=[jax.ShapeDtypeStruct((N_EXPERTS, nreq * cap, D_MODEL), BF16),
                   jax.ShapeDtypeStruct((N_EXPERTS, nreq * cap, 1), F32)],
        scratch_shapes=[pltpu.VMEM((N_EXPERTS * cap, T), BF16)],
        compiler_params=_params("parallel"),
        name="gather",
    )(pos, aff_t, h2)


def _expert_kernel(xp_ref, xs_ref, gp_ref, gsm_ref, wg_ref, wu_ref, wd_ref, yp_ref, ys_ref,
                   wgb_ref, wub_ref, wdb_ref):
    wgb_ref[...] = wg_ref[...].astype(BF16)
    wub_ref[...] = wu_ref[...].astype(BF16)
    wdb_ref[...] = wd_ref[...].astype(BF16)
    for x_ref, g_ref, y_ref in ((xp_ref, gp_ref, yp_ref), (xs_ref, gsm_ref, ys_ref)):
        for t in range(x_ref.shape[0] // EXPERT_ROW_TILE):
            rows = slice(t * EXPERT_ROW_TILE, (t + 1) * EXPERT_ROW_TILE)
            x = x_ref[rows, :]
            hg = jnp.dot(x, wgb_ref[...], preferred_element_type=F32)
            hu = jnp.dot(x, wub_ref[...], preferred_element_type=F32)
            hid = (hg * _sigmoid(hg) * hu).astype(BF16)
            y = jnp.dot(hid, wdb_ref[...], preferred_element_type=F32)
            y_ref[rows, :] = (y * g_ref[rows, :]).astype(BF16)


def _experts(xs_p, xs_s, gs_p, gs_s, layer, w_gate, w_up, w_down):
    np_, ns_ = xs_p.shape[1], xs_s.shape[1]
    slots = lambda n, w: pl.BlockSpec((None, n, w), lambda e: (e, 0, 0))
    weight = lambda rows, cols: pl.BlockSpec((None, None, rows, cols), lambda e: (layer, e, 0, 0))
    return pl.pallas_call(
        _expert_kernel,
        grid=(N_EXPERTS,),
        in_specs=[
            slots(np_, D_MODEL), slots(ns_, D_MODEL), slots(np_, 1), slots(ns_, 1),
            weight(D_MODEL, D_EXPERT), weight(D_MODEL, D_EXPERT), weight(D_EXPERT, D_MODEL),
        ],
        out_specs=[slots(np_, D_MODEL), slots(ns_, D_MODEL)],
        out_shape=[jax.ShapeDtypeStruct(xs_p.shape, BF16), jax.ShapeDtypeStruct(xs_s.shape, BF16)],
        scratch_shapes=[pltpu.VMEM((D_MODEL, D_EXPERT), BF16), pltpu.VMEM((D_MODEL, D_EXPERT), BF16),
                        pltpu.VMEM((D_EXPERT, D_MODEL), BF16)],
        compiler_params=pltpu.CompilerParams(dimension_semantics=("parallel",),
                                             vmem_limit_bytes=EXPERT_VMEM_LIMIT),
        name="experts",
    )(xs_p, xs_s, gs_p, gs_s, w_gate, w_up, w_down)


def _combine_kernel(*refs, cap, final):
    xn_ref, pc_ref, ye_ref, mod_ref, ex_ref = refs[:5]
    if final:
        fg_ref = refs[5]
    o_ref = refs[-1]
    seq_len = pc_ref.shape[0]
    reqs = xn_ref.shape[0] // seq_len
    first_req = pl.program_id(0) * reqs
    for t in range(xn_ref.shape[0] // SUB_ROWS):
        rows = slice(t * SUB_ROWS, (t + 1) * SUB_ROWS)
        req, off = divmod(t * SUB_ROWS, seq_len)
        ye = ye_ref[:, req * cap:(req + 1) * cap, :].reshape(N_EXPERTS * cap, D_MODEL)
        expander = ex_ref[(first_req + req) % ex_ref.shape[0]]
        pos = jnp.dot(pc_ref[off:off + SUB_ROWS, :].astype(BF16), expander, preferred_element_type=F32)
        lane = lax.broadcasted_iota(jnp.int32, pos.shape, 1)
        slot = (lane % cap).astype(F32)
        onehot = jnp.where(pos == slot, 1.0, 0.0).astype(BF16)
        moe = jnp.dot(onehot, ye, preferred_element_type=F32)
        x = xn_ref[rows, :] + mod_ref[5:6, :] * moe
        if final:
            x = x * lax.rsqrt(jnp.mean(x * x, axis=-1, keepdims=True) + EPS) * fg_ref[...]
        o_ref[rows, :] = x


def _combine(xn, pos_col, ye, mods, layer, mod_row, seq_len, final_g):
    n = xn.shape[0]
    T = seq_len
    tm = ROW_TILE
    reqs = tm // T
    cap = EC_FACTOR * T // N_EXPERTS
    final = final_g is not None
    group = LANES // N_EXPERTS
    assert tm % T == 0 and T % SUB_ROWS == 0 and group % reqs == 0
    src = jnp.arange(group)[:, None, None] * N_EXPERTS + (jnp.arange(N_EXPERTS * cap) // cap)[None, None, :]
    expander = (jnp.arange(LANES)[None, :, None] == src).astype(BF16)
    in_specs = [
        pl.BlockSpec((tm, D_MODEL), lambda i: (i, 0)),
        pl.BlockSpec((T, LANES), lambda i: (0, i * reqs // group)),
        pl.BlockSpec((N_EXPERTS, reqs * cap, D_MODEL), lambda i: (0, i, 0)),
        pl.BlockSpec((None, None, 6, D_MODEL), lambda i: (layer, mod_row(i), 0, 0)),
        pl.BlockSpec((group, LANES, N_EXPERTS * cap), lambda i: (0, 0, 0)),
    ]
    args = [xn, pos_col, ye, mods, expander]
    if final:
        in_specs.append(pl.BlockSpec((1, D_MODEL), lambda i: (0, 0)))
        args.append(final_g)
    return pl.pallas_call(
        functools.partial(_combine_kernel, cap=cap, final=final),
        grid=(n // tm,),
        in_specs=in_specs,
        out_specs=pl.BlockSpec((tm, D_MODEL), lambda i: (i, 0)),
        out_shape=jax.ShapeDtypeStruct((n, D_MODEL), F32),
        compiler_params=_params("parallel"),
        name="combine",
    )(*args)


def _rope_tables(T):
    rows = T // GRID_W
    row = jnp.repeat(jnp.arange(rows, dtype=F32), GRID_W)
    col = jnp.tile(jnp.arange(GRID_W, dtype=F32), rows)
    inv = ROPE_BASE ** (-jnp.arange(0, ROPE_AXIS, 2, dtype=F32) / ROPE_AXIS)
    ar = row[:, None] * inv
    ac = col[:, None] * inv
    cos = jnp.concatenate([jnp.cos(ar), jnp.cos(ar), jnp.cos(ac), jnp.cos(ac)], axis=-1)
    sin = jnp.concatenate([-jnp.sin(ar), jnp.sin(ar), -jnp.sin(ac), jnp.sin(ac)], axis=-1)
    reps = D_QK // HEAD_DIM
    return jnp.tile(cos, (1, reps)), jnp.tile(sin, (1, reps))


def kernel(x_prompt, x_sample, c, cache_k, cache_v, c_ctx, norm1_g, w_mod, b_mod, w_in, conv_w, conv_b,
           conv_ln_g, conv_ln_b, lambda_q1, lambda_k1, lambda_q2, lambda_k2, subln_g, w_out, norm2_g,
           w_router, w_gate, w_up, w_down, final_norm_g):
    batch, seq, _ = x_prompt.shape
    dec_batch, dec_seq, _ = x_sample.shape
    past = cache_k.shape[2]
    assert 1 + dec_batch <= MOD_ROWS

    cvec = jnp.concatenate([c_ctx[None, :], c, jnp.zeros((MOD_ROWS - 1 - dec_batch, D_MODEL), F32)], axis=0)
    mods = _modulation(cvec, w_mod, b_mod)

    w_in_b = w_in.astype(BF16)
    w_out_b = w_out.astype(BF16)
    w_router_b = jnp.pad(w_router, ((0, 0), (0, 0), (0, LANES - N_EXPERTS))).astype(BF16)
    lam_vecs = jnp.stack([lambda_q1, lambda_k1, lambda_q2, lambda_k2], axis=1)
    vec = lambda a: a.reshape(DEPTH, 1, a.shape[-1])
    norm1_v, norm2_v, subln_v = vec(norm1_g), vec(norm2_g), vec(subln_g)
    conv_b_v, ln_g_v, ln_b_v = vec(conv_b), vec(conv_ln_g), vec(conv_ln_b)
    ctx_k = cache_k.reshape(dec_batch, DEPTH, past, D_QK)
    ctx_v = cache_v.reshape(dec_batch, DEPTH, past, D_ATTN)
    rope_tabs = _rope_tables(dec_seq)
    final_g = final_norm_g.reshape(1, D_MODEL)
    tri_p = (jnp.arange(seq)[:, None] <= jnp.arange(seq)[None, :]).astype(BF16)
    tri_s = (jnp.arange(dec_seq)[:, None] <= jnp.arange(dec_seq)[None, :]).astype(BF16)

    prompt_row = lambda i: 0
    assert dec_seq % ROW_TILE == 0
    tiles_per_sample = dec_seq // ROW_TILE
    sample_row = lambda i: 1 + i // tiles_per_sample

    xp = x_prompt.reshape(batch * seq, D_MODEL)
    xs = x_sample.reshape(dec_batch * dec_seq, D_MODEL)
    new_k, new_v = [], []
    for l in range(DEPTH):
        last = l == DEPTH - 1
        up, qp, kp, vp, kf, vf = _inproj(xp, mods, l, norm1_v, w_in_b, prompt_row, None, seq, True)
        us, qs, ks, vs = _inproj(xs, mods, l, norm1_v, w_in_b, sample_row, rope_tabs, dec_seq, False)
        new_k.append(kf.reshape(batch, seq, N_HEADS, 2, HEAD_DIM))
        new_v.append(vf.reshape(batch, seq, N_HEADS, V_DIM))
        cp = _conv(up, l, conv_w, conv_b_v, ln_g_v, ln_b_v, seq)
        cs = _conv(us, l, conv_w, conv_b_v, ln_g_v, ln_b_v, dec_seq)
        op = _attention(qp, kp, vp, lam_vecs, subln_v, l, seq, None)
        os_ = _attention(qs, ks, vs, lam_vecs, subln_v, l, dec_seq, (ctx_k, ctx_v))
        xnp, h2p, affp = _outproj(xp, cp, op, mods, l, w_out_b, norm2_v, w_router_b, prompt_row, seq)
        xns, h2s, affs = _outproj(xs, cs, os_, mods, l, w_out_b, norm2_v, w_router_b, sample_row, dec_seq)
        pos_p, rank_p = _select(affp, tri_p, EC_FACTOR * seq // N_EXPERTS)
        pos_s, rank_s = _select(affs, tri_s, EC_FACTOR * dec_seq // N_EXPERTS)
        slots_p, gates_p = _gather(pos_p, affp, h2p, seq)
        slots_s, gates_s = _gather(pos_s, affs, h2s, dec_seq)
        ye_p, ye_s = _experts(slots_p, slots_s, gates_p, gates_s, l, w_gate, w_up, w_down)
        xp = _combine(xnp, rank_p, ye_p, mods, l, prompt_row, seq, final_g if last else None)
        xs = _combine(xns, rank_s, ye_s, mods, l, sample_row, dec_seq, final_g if last else None)

    y_prompt = xp.reshape(batch, seq, D_MODEL)
    y_sample = xs.reshape(dec_batch, dec_seq, D_MODEL)
    return (y_prompt, y_sample, jnp.stack(new_k, axis=1), jnp.stack(new_v, axis=1))
```

```python
import functools
import math

import jax
import jax.numpy as jnp
from jax import lax
from jax.experimental import pallas as pl
from jax.experimental.pallas import tpu as pltpu

D_MODEL = 1024
DEPTH = 4
GRID_W = 64
D_CONV = 512
CONV_WIDTH = 31
N_HEADS = 4
HEAD_DIM = 64
V_DIM = 2 * HEAD_DIM
D_QK = N_HEADS * 2 * HEAD_DIM
D_ATTN = N_HEADS * V_DIM
D_MIX = D_CONV + D_ATTN
D_IN = 2 * D_CONV + 2 * D_QK + D_ATTN
N_EXPERTS = 16
EC_FACTOR = 2
D_EXPERT = 1024
ROPE_BASE = 10000.0
ROPE_AXIS = HEAD_DIM // 2
EPS = 1e-6
LOG2_E = math.log2(math.e)

LANES = 128
MOD_ROWS = 16
TOKEN_TILE = 512
ROW_TILE = 1024
SUB_ROWS = 256
CONV_CHUNK = 64
NORM_CHUNK = 256
CONV_PAD = 16
EXPERT_ROW_TILE = 512
MANTISSA_BITS = 23
VMEM_LIMIT = 56 * 1024 * 1024
EXPERT_VMEM_LIMIT = 60 * 1024 * 1024

F32 = jnp.float32
BF16 = jnp.bfloat16


def _sigmoid(x):
    return 1.0 / (1.0 + jnp.exp(-x))


def _params(*sem):
    return pltpu.CompilerParams(dimension_semantics=sem, vmem_limit_bytes=VMEM_LIMIT)


def _mod_kernel(c_ref, w_ref, b_ref, o_ref):
    c = c_ref[...]
    s = c * _sigmoid(c)
    o_ref[...] = jnp.dot(s.astype(BF16), w_ref[...].astype(BF16),
                         preferred_element_type=F32) + b_ref[...]


def _modulation(cvec, w_mod, b_mod):
    tn = 1536
    out = pl.pallas_call(
        _mod_kernel,
        grid=(DEPTH, 6 * D_MODEL // tn),
        in_specs=[
            pl.BlockSpec((MOD_ROWS, D_MODEL), lambda l, j: (0, 0)),
            pl.BlockSpec((None, D_MODEL, tn), lambda l, j: (l, 0, j)),
            pl.BlockSpec((None, 1, tn), lambda l, j: (l, 0, j)),
        ],
        out_specs=pl.BlockSpec((None, MOD_ROWS, tn), lambda l, j: (l, 0, j)),
        out_shape=jax.ShapeDtypeStruct((DEPTH, MOD_ROWS, 6 * D_MODEL), F32),
        compiler_params=_params("parallel", "parallel"),
        name="modulation",
    )(cvec, w_mod, b_mod.reshape(DEPTH, 1, 6 * D_MODEL))
    return out.reshape(DEPTH, MOD_ROWS, 6, D_MODEL)


def _inproj_kernel(*refs, rope, cache_out):
    x_ref, mod_ref, g_ref, w_ref = refs[:4]
    refs = refs[4:]
    if rope:
        cos_ref, sin_ref = refs[:2]
        refs = refs[2:]
    u_ref, q_ref, k_ref, v_ref = refs[:4]
    wb_ref = refs[-1]

    @pl.when(pl.program_id(0) == 0)
    def _():
        wb_ref[...] = w_ref[...].astype(BF16)

    for t in range(x_ref.shape[0] // SUB_ROWS):
        rows = slice(t * SUB_ROWS, (t + 1) * SUB_ROWS)
        x = x_ref[rows, :]
        ms = jnp.mean(x * x, axis=-1, keepdims=True)
        y = x * lax.rsqrt(ms + EPS) * g_ref[...]
        h = y * (1.0 + mod_ref[1:2, :]) + mod_ref[0:1, :]
        z = jnp.dot(h.astype(BF16), wb_ref[...], preferred_element_type=F32)
        u_ref[rows, :] = z[:, :D_CONV] * _sigmoid(z[:, D_CONV:2 * D_CONV])
        q = z[:, 2 * D_CONV:2 * D_CONV + D_QK]
        k = z[:, 2 * D_CONV + D_QK:2 * D_CONV + 2 * D_QK]
        v = z[:, 2 * D_CONV + 2 * D_QK:]
        if cache_out:
            kf_ref, vf_ref = refs[4:6]
            kf_ref[rows, :] = k
            vf_ref[rows, :] = v
        if rope:
            lane = lax.broadcasted_iota(jnp.int32, q.shape, 1)
            first = (lane % ROPE_AXIS) < (ROPE_AXIS // 2)
            cos = cos_ref[rows, :]
            sin = sin_ref[rows, :]

            def rot(a):
                partner = jnp.where(first, pltpu.roll(a, D_QK - ROPE_AXIS // 2, 1),
                                    pltpu.roll(a, ROPE_AXIS // 2, 1))
                return a * cos + partner * sin

            q = rot(q)
            k = rot(k)
        q_ref[rows, :] = (q * (HEAD_DIM ** -0.5 * LOG2_E)).astype(BF16)
        k_ref[rows, :] = k.astype(BF16)
        v_ref[rows, :] = v.astype(BF16)


def _inproj(x, mods, layer, norm_g, w_in, mod_row, rope_tabs, seq_len, cache_out):
    n = x.shape[0]
    tm = ROW_TILE
    assert rope_tabs is None or seq_len % tm == 0
    tiles_per_seq = max(1, seq_len // tm)
    rope = rope_tabs is not None
    in_specs = [
        pl.BlockSpec((tm, D_MODEL), lambda i: (i, 0)),
        pl.BlockSpec((None, None, 6, D_MODEL), lambda i: (layer, mod_row(i), 0, 0)),
        pl.BlockSpec((None, 1, D_MODEL), lambda i: (layer, 0, 0)),
        pl.BlockSpec((None, D_MODEL, D_IN), lambda i: (layer, 0, 0), pipeline_mode=pl.Buffered(1)),
    ]
    args = [x, mods, norm_g, w_in]
    if rope:
        in_specs += [pl.BlockSpec((tm, D_QK), lambda i: (i % tiles_per_seq, 0))] * 2
        args += list(rope_tabs)
    half = pl.BlockSpec((tm, D_CONV), lambda i: (i, 0))
    out_specs = [half, half, half, half]
    out_shape = [jax.ShapeDtypeStruct((n, D_CONV), F32)] + [jax.ShapeDtypeStruct((n, D_QK), BF16)] * 3
    if cache_out:
        out_specs += [half, half]
        out_shape += [jax.ShapeDtypeStruct((n, D_QK), F32)] * 2
    return pl.pallas_call(
        functools.partial(_inproj_kernel, rope=rope, cache_out=cache_out),
        grid=(n // tm,),
        in_specs=in_specs,
        out_specs=out_specs,
        out_shape=out_shape,
        scratch_shapes=[pltpu.VMEM((D_MODEL, D_IN), BF16)],
        compiler_params=_params("arbitrary"),
        name="inproj",
    )(*args)


def _conv_stage(u_ref, w_ref, pad_ref, sh_ref, wb_ref):
    T = u_ref.shape[0]
    tail = CONV_PAD + T
    for k in range(CONV_WIDTH):
        wb_ref[k] = jnp.broadcast_to(w_ref[k:k + 1, :], (8, D_CONV))
    pad_ref[0:CONV_PAD, :] = jnp.zeros((CONV_PAD, D_CONV), F32)
    pad_ref[tail:tail + CONV_PAD, :] = jnp.zeros((CONV_PAD, D_CONV), F32)
    pad_ref[CONV_PAD:tail, :] = u_ref[...]
    rows = sh_ref.shape[1]
    for r in range(8):
        sh_ref[r] = pad_ref[r:r + rows, :]


def _conv_block(sh_ref, w, base, lanes):
    first = CONV_PAD - CONV_WIDTH // 2
    groups = CONV_CHUNK // 8
    acc = jnp.zeros((groups, 8, LANES), F32)
    for r in range(8):
        taps = [(k, (first + k) // 8) for k in range(CONV_WIDTH) if (first + k) % 8 == r]
        m0 = min(m for _, m in taps)
        span = max(m for _, m in taps) - m0 + groups
        x = sh_ref[r, pl.ds(base + 8 * m0, 8 * span), lanes].reshape(span, 8, LANES)
        for k, m in taps:
            acc = acc + w[k] * x[m - m0:m - m0 + groups]
    return acc.reshape(CONV_CHUNK, LANES)


def _conv_norm(acc, b_ref, lg_ref, lb_ref):
    acc = acc + b_ref[...]
    mu = jnp.mean(acc, axis=-1, keepdims=True)
    d = acc - mu
    var = jnp.mean(d * d, axis=-1, keepdims=True)
    un = d * lax.rsqrt(var + EPS) * lg_ref[...] + lb_ref[...]
    return (un * _sigmoid(un)).astype(BF16)


def _conv_kernel(u_ref, w_ref, b_ref, lg_ref, lb_ref, o_ref, pad_ref, sh_ref, wb_ref, acc_ref):
    T = u_ref.shape[0]
    _conv_stage(u_ref, w_ref, pad_ref, sh_ref, wb_ref)

    for q in range(D_CONV // LANES):
        lanes = slice(q * LANES, (q + 1) * LANES)
        w = [wb_ref[k, :, lanes] for k in range(CONV_WIDTH)]

        def taps_chunk(i, carry, lanes=lanes, w=w):
            base = pl.multiple_of(i * CONV_CHUNK, CONV_CHUNK)
            acc_ref[pl.ds(base, CONV_CHUNK), lanes] = _conv_block(sh_ref, w, base, lanes)
            return carry

        lax.fori_loop(0, T // CONV_CHUNK, taps_chunk, 0)

    def norm_chunk(i, carry):
        base = pl.multiple_of(i * NORM_CHUNK, NORM_CHUNK)
        o_ref[pl.ds(base, NORM_CHUNK), :] = _conv_norm(acc_ref[pl.ds(base, NORM_CHUNK), :],
                                                       b_ref, lg_ref, lb_ref)
        return carry

    lax.fori_loop(0, T // NORM_CHUNK, norm_chunk, 0)


def _conv(u, layer, conv_w, conv_b, ln_g, ln_b, seq_len):
    n = u.shape[0]
    T = seq_len
    rows = T + 2 * CONV_PAD - 8
    vec = pl.BlockSpec((None, 1, D_CONV), lambda b: (layer, 0, 0))
    return pl.pallas_call(
        _conv_kernel,
        grid=(n // T,),
        in_specs=[
            pl.BlockSpec((T, D_CONV), lambda b: (b, 0)),
            pl.BlockSpec((None, CONV_WIDTH, D_CONV), lambda b: (layer, 0, 0)),
            vec, vec, vec,
        ],
        out_specs=pl.BlockSpec((T, D_CONV), lambda b: (b, 0)),
        out_shape=jax.ShapeDtypeStruct((n, D_CONV), BF16),
        scratch_shapes=[pltpu.VMEM((T + 2 * CONV_PAD, D_CONV), F32),
                        pltpu.VMEM((8, rows, D_CONV), F32),
                        pltpu.VMEM((CONV_WIDTH, 8, D_CONV), F32),
                        pltpu.VMEM((T, D_CONV), F32)],
        compiler_params=_params("parallel"),
        name="conv",
    )(u, conv_w, conv_b, ln_g, ln_b)


def _attn_kernel(*refs, lam_init, has_ctx):
    lam_ref, sg_ref, q_ref, k_ref, v_ref = refs[:5]
    T = k_ref.shape[0]
    if has_ctx:
        ck_ref, cv_ref, o_ref, vals_ref, keys_ref = refs[5:]
    else:
        o_ref, vals_ref = refs[5:]
        keys_ref = k_ref

    @pl.when(pl.program_id(1) == 0)
    def _():
        vals_ref[...] = jnp.ones(vals_ref.shape, BF16)
        for h in range(N_HEADS):
            vals_ref[0:T, 2 * h * V_DIM:(2 * h + 1) * V_DIM] = v_ref[:, h * V_DIM:(h + 1) * V_DIM]
        if has_ctx:
            keys_ref[0:T, :] = k_ref[...]
            keys_ref[T:, :] = ck_ref[...].astype(BF16)
            cv = cv_ref[...].astype(BF16)
            for h in range(N_HEADS):
                vals_ref[T:, 2 * h * V_DIM:(2 * h + 1) * V_DIM] = cv[:, h * V_DIM:(h + 1) * V_DIM]

    l = lam_ref[...]
    lam = (jnp.exp(jnp.sum(l[0:1, :] * l[1:2, :], axis=-1, keepdims=True))
           - jnp.exp(jnp.sum(l[2:3, :] * l[3:4, :], axis=-1, keepdims=True)) + lam_init)
    tq = q_ref.shape[0]
    lane = lax.broadcasted_iota(jnp.int32, (tq, V_DIM), 1)
    nt = (((1,), (1,)), ((), ()))
    for h in range(N_HEADS):
        sl = slice(h * V_DIM, (h + 1) * V_DIM)
        qh = q_ref[:, sl]
        kh = keys_ref[:, sl]
        heads = []
        for c in range(2):
            keep = (lane < HEAD_DIM) if c == 0 else (lane >= HEAD_DIM)
            qc = jnp.where(keep, qh, jnp.zeros_like(qh))
            s = lax.dot_general(qc, kh, nt, preferred_element_type=F32)
            e = jnp.exp2(s - jnp.max(s, axis=-1, keepdims=True)).astype(BF16)
            pv = jnp.dot(e, vals_ref[:, 2 * h * V_DIM:(2 * h + 2) * V_DIM], preferred_element_type=F32)
            heads.append(pv[:, :V_DIM] / pv[:, V_DIM:])
        o = heads[0] - lam * heads[1]
        o = o * lax.rsqrt(jnp.mean(o * o, axis=-1, keepdims=True) + EPS)
        o = o * sg_ref[...] * (1.0 - lam_init)
        o_ref[:, sl] = o.astype(BF16)


def _attention(q, k, v, lam_vecs, subln_g, layer, seq_len, ctx):
    n = q.shape[0]
    T = seq_len
    tq = min(TOKEN_TILE, T)
    nq = T // tq
    lam_init = 0.8 - 0.6 * math.exp(-0.3 * layer)
    has_ctx = ctx is not None
    kv = pl.BlockSpec((T, D_QK), lambda b, i: (b, 0))
    in_specs = [
        pl.BlockSpec((None, 4, HEAD_DIM), lambda b, i: (layer, 0, 0)),
        pl.BlockSpec((None, 1, V_DIM), lambda b, i: (layer, 0, 0)),
        pl.BlockSpec((tq, D_QK), lambda b, i: (b * nq + i, 0)),
        kv, kv,
    ]
    args = [lam_vecs, subln_g, q, k, v]
    past = ctx[0].shape[2] if has_ctx else 0
    scratch = [pltpu.VMEM((T + past, 2 * D_ATTN), BF16)]
    if has_ctx:
        cspec = pl.BlockSpec((None, None, past, D_QK), lambda b, i: (b, layer, 0, 0))
        in_specs += [cspec, cspec]
        args += list(ctx)
        scratch.append(pltpu.VMEM((T + past, D_QK), BF16))
    return pl.pallas_call(
        functools.partial(_attn_kernel, lam_init=lam_init, has_ctx=has_ctx),
        grid=(n // T, nq),
        in_specs=in_specs,
        out_specs=pl.BlockSpec((tq, D_ATTN), lambda b, i: (b * nq + i, 0)),
        out_shape=jax.ShapeDtypeStruct((n, D_ATTN), BF16),
        scratch_shapes=scratch,
        compiler_params=_params("parallel", "arbitrary"),
        name="attention",
    )(*args)


def _outproj_kernel(x_ref, mc_ref, ma_ref, mod_ref, w_ref, g_ref, wr_ref, xn_ref, h2_ref, aff_ref, wb_ref,
                    *, seq_len):
    @pl.when(pl.program_id(0) == 0)
    def _():
        wb_ref[...] = w_ref[...].astype(BF16)

    for t in range(x_ref.shape[0] // SUB_ROWS):
        rows = slice(t * SUB_ROWS, (t + 1) * SUB_ROWS)
        req, off = divmod(t * SUB_ROWS, seq_len)
        acc = jnp.dot(mc_ref[rows, :], wb_ref[0:D_CONV, :], preferred_element_type=F32)
        acc = acc + jnp.dot(ma_ref[rows, :], wb_ref[D_CONV:D_MIX, :], preferred_element_type=F32)
        x = x_ref[rows, :] + mod_ref[2:3, :] * acc
        xn_ref[rows, :] = x
        ms = jnp.mean(x * x, axis=-1, keepdims=True)
        y = x * lax.rsqrt(ms + EPS) * g_ref[...]
        h2 = (y * (1.0 + mod_ref[4:5, :]) + mod_ref[3:4, :]).astype(BF16)
        h2_ref[rows, :] = h2
        logits = jnp.dot(h2, wr_ref[...], preferred_element_type=F32)
        lane = lax.broadcasted_iota(jnp.int32, logits.shape, 1)
        logits = jnp.where(lane < N_EXPERTS, logits, -jnp.inf)
        e = jnp.exp(logits - jnp.max(logits, axis=-1, keepdims=True))
        aff = e / jnp.sum(e, axis=-1, keepdims=True)
        aff_ref[req * N_EXPERTS:(req + 1) * N_EXPERTS, off:off + SUB_ROWS] = aff.T[0:N_EXPERTS, :]


def _outproj(x, mix_conv, mix_attn, mods, layer, w_out, norm_g, w_router_b, mod_row, seq_len):
    n = x.shape[0]
    tm = ROW_TILE
    assert tm % seq_len == 0 and seq_len % SUB_ROWS == 0
    reqs = tm // seq_len
    row = lambda i: (i, 0)
    return pl.pallas_call(
        functools.partial(_outproj_kernel, seq_len=seq_len),
        grid=(n // tm,),
        in_specs=[
            pl.BlockSpec((tm, D_MODEL), row),
            pl.BlockSpec((tm, D_CONV), row),
            pl.BlockSpec((tm, D_ATTN), row),
            pl.BlockSpec((None, None, 6, D_MODEL), lambda i: (layer, mod_row(i), 0, 0)),
            pl.BlockSpec((None, D_MIX, D_MODEL), lambda i: (layer, 0, 0), pipeline_mode=pl.Buffered(1)),
            pl.BlockSpec((None, 1, D_MODEL), lambda i: (layer, 0, 0)),
            pl.BlockSpec((None, D_MODEL, LANES), lambda i: (layer, 0, 0)),
        ],
        out_specs=[pl.BlockSpec((tm, D_MODEL), row), pl.BlockSpec((tm, D_MODEL), row),
                   pl.BlockSpec((reqs * N_EXPERTS, seq_len), row)],
        out_shape=[jax.ShapeDtypeStruct((n, D_MODEL), F32), jax.ShapeDtypeStruct((n, D_MODEL), BF16),
                   jax.ShapeDtypeStruct((n // seq_len * N_EXPERTS, seq_len), F32)],
        scratch_shapes=[pltpu.VMEM((D_MIX, D_MODEL), BF16)],
        compiler_params=_params("arbitrary"),
        name="outproj",
    )(x, mix_conv, mix_attn, mods, w_out, norm_g, w_router_b)


def _select_kernel(a_ref, tri_ref, pos_ref, posc_ref, *, cap):
    a = a_ref[...]
    rows = a.shape[0]
    capf = float(cap)

    def count_ge(v):
        return jnp.sum(jnp.where(a >= v, 1.0, 0.0), axis=-1, keepdims=True)

    hi = jnp.full((rows, 1), 2.0, F32)
    for j in range(6, -1, -1):
        cand = hi * (2.0 ** -(2 ** j))
        hi = jnp.where(count_ge(cand) < capf, cand, hi)
    lo = hi * 0.5
    step = lo * 0.5
    for _ in range(MANTISSA_BITS):
        cand = lo + step
        lo = jnp.where(count_ge(cand) >= capf, cand, lo)
        step = step * 0.5
    upper = jnp.where(step > 0.0, lo + step * 2.0, hi)
    above = jnp.where(a >= upper, 1.0, 0.0)
    tie = jnp.where(a >= lo, 1.0, 0.0) - above
    need = capf - jnp.sum(above, axis=-1, keepdims=True)
    tri = tri_ref[...]
    tie_rank = jnp.dot(tie.astype(BF16), tri, preferred_element_type=F32)
    sel = above + tie * jnp.where(tie_rank <= need, 1.0, 0.0)
    slot = jnp.dot(sel.astype(BF16), tri, preferred_element_type=F32) - 1.0
    pos = jnp.where(sel > 0.5, slot, -1.0)
    pos_ref[...] = pos
    posc_ref[...] = pos.T


def _select(aff_t, tri, cap):
    rows, T = aff_t.shape
    return pl.pallas_call(
        functools.partial(_select_kernel, cap=cap),
        grid=(1,),
        in_specs=[pl.BlockSpec((rows, T), lambda i: (0, 0)), pl.BlockSpec((T, T), lambda i: (0, 0))],
        out_specs=[pl.BlockSpec((rows, T), lambda i: (0, 0)), pl.BlockSpec((T, rows), lambda i: (0, 0))],
        out_shape=[jax.ShapeDtypeStruct((rows, T), F32), jax.ShapeDtypeStruct((T, rows), F32)],
        compiler_params=_params("arbitrary"),
        name="select",
    )(aff_t, tri)


def _gather_kernel(pos_ref, a_ref, h2_ref, xs_ref, gs_ref, oh_ref, *, cap):
    T = h2_ref.shape[0]
    slot = lax.broadcasted_iota(jnp.int32, (cap, T), 0).astype(F32)
    for e in range(N_EXPERTS):
        onehot = jnp.where(pos_ref[e:e + 1, :] == slot, 1.0, 0.0)
        oh_ref[e * cap:(e + 1) * cap, :] = onehot.astype(BF16)
        gs_ref[e] = jnp.sum(onehot * a_ref[e:e + 1, :], axis=-1, keepdims=True)
    xs = jnp.dot(oh_ref[...], h2_ref[...], preferred_element_type=F32)
    xs_ref[...] = xs.reshape(N_EXPERTS, cap, D_MODEL).astype(BF16)


def _gather(pos, aff_t, h2, seq_len):
    T = seq_len
    nreq = h2.shape[0] // T
    cap = EC_FACTOR * T // N_EXPERTS
    rows = pl.BlockSpec((N_EXPERTS, T), lambda b: (b, 0))
    return pl.pallas_call(
        functools.partial(_gather_kernel, cap=cap),
        grid=(nreq,),
        in_specs=[rows, rows, pl.BlockSpec((T, D_MODEL), lambda b: (b, 0))],
        out_specs=[pl.BlockSpec((N_EXPERTS, cap, D_MODEL), lambda b: (0, b, 0)),
                   pl.BlockSpec((N_EXPERTS, cap, 1), lambda b: (0, b, 0))],
        out_shape=[jax.ShapeDtypeStruct((N_EXPERTS, nreq * cap, D_MODEL), BF16),
                   jax.ShapeDtypeStruct((N_EXPERTS, nreq * cap, 1), F32)],
        scratch_shapes=[pltpu.VMEM((N_EXPERTS * cap, T), BF16)],
        compiler_params=_params("parallel"),
        name="gather",
    )(pos, aff_t, h2)


def _expert_kernel(xp_ref, xs_ref, gp_ref, gsm_ref, wg_ref, wu_ref, wd_ref, yp_ref, ys_ref,
                   wgb_ref, wub_ref, wdb_ref):
    wgb_ref[...] = wg_ref[...].astype(BF16)
    wub_ref[...] = wu_ref[...].astype(BF16)
    wdb_ref[...] = wd_ref[...].astype(BF16)
    for x_ref, g_ref, y_ref in ((xp_ref, gp_ref, yp_ref), (xs_ref, gsm_ref, ys_ref)):
        for t in range(x_ref.shape[0] // EXPERT_ROW_TILE):
            rows = slice(t * EXPERT_ROW_TILE, (t + 1) * EXPERT_ROW_TILE)
            x = x_ref[rows, :]
            hg = jnp.dot(x, wgb_ref[...], preferred_element_type=F32)
            hu = jnp.dot(x, wub_ref[...], preferred_element_type=F32)
            hid = (hg * _sigmoid(hg) * hu).astype(BF16)
            y = jnp.dot(hid, wdb_ref[...], preferred_element_type=F32)
            y_ref[rows, :] = (y * g_ref[rows, :]).astype(BF16)


def _experts(xs_p, xs_s, gs_p, gs_s, layer, w_gate, w_up, w_down):
    np_, ns_ = xs_p.shape[1], xs_s.shape[1]
    slots = lambda n, w: pl.BlockSpec((None, n, w), lambda e: (e, 0, 0))
    weight = lambda rows, cols: pl.BlockSpec((None, None, rows, cols), lambda e: (layer, e, 0, 0))
    return pl.pallas_call(
        _expert_kernel,
        grid=(N_EXPERTS,),
        in_specs=[
            slots(np_, D_MODEL), slots(ns_, D_MODEL), slots(np_, 1), slots(ns_, 1),
            weight(D_MODEL, D_EXPERT), weight(D_MODEL, D_EXPERT), weight(D_EXPERT, D_MODEL),
        ],
        out_specs=[slots(np_, D_MODEL), slots(ns_, D_MODEL)],
        out_shape=[jax.ShapeDtypeStruct(xs_p.shape, BF16), jax.ShapeDtypeStruct(xs_s.shape, BF16)],
        scratch_shapes=[pltpu.VMEM((D_MODEL, D_EXPERT), BF16), pltpu.VMEM((D_MODEL, D_EXPERT), BF16),
                        pltpu.VMEM((D_EXPERT, D_MODEL), BF16)],
        compiler_params=pltpu.CompilerParams(dimension_semantics=("parallel",),
                                             vmem_limit_bytes=EXPERT_VMEM_LIMIT),
        name="experts",
    )(xs_p, xs_s, gs_p, gs_s, w_gate, w_up, w_down)


def _combine_kernel(*refs, cap, final):
    xn_ref, pc_ref, ye_ref, mod_ref, ex_ref = refs[:5]
    if final:
        fg_ref = refs[5]
    o_ref = refs[-1]
    seq_len = pc_ref.shape[0]
    reqs = xn_ref.shape[0] // seq_len
    first_req = pl.program_id(0) * reqs
    for t in range(xn_ref.shape[0] // SUB_ROWS):
        rows = slice(t * SUB_ROWS, (t + 1) * SUB_ROWS)
        req, off = divmod(t * SUB_ROWS, seq_len)
        ye = ye_ref[:, req * cap:(req + 1) * cap, :].reshape(N_EXPERTS * cap, D_MODEL)
        expander = ex_ref[(first_req + req) % ex_ref.shape[0]]
        pos = jnp.dot(pc_ref[off:off + SUB_ROWS, :].astype(BF16), expander, preferred_element_type=F32)
        lane = lax.broadcasted_iota(jnp.int32, pos.shape, 1)
        slot = (lane % cap).astype(F32)
        onehot = jnp.where(pos == slot, 1.0, 0.0).astype(BF16)
        moe = jnp.dot(onehot, ye, preferred_element_type=F32)
        x = xn_ref[rows, :] + mod_ref[5:6, :] * moe
        if final:
            x = x * lax.rsqrt(jnp.mean(x * x, axis=-1, keepdims=True) + EPS) * fg_ref[...]
        o_ref[rows, :] = x


def _combine(xn, pos_col, ye, mods, layer, mod_row, seq_len, final_g):
    n = xn.shape[0]
    T = seq_len
    tm = ROW_TILE
    reqs = tm // T
    cap = EC_FACTOR * T // N_EXPERTS
    final = final_g is not None
    group = LANES // N_EXPERTS
    assert tm % T == 0 and T % SUB_ROWS == 0 and group % reqs == 0
    src = jnp.arange(group)[:, None, None] * N_EXPERTS + (jnp.arange(N_EXPERTS * cap) // cap)[None, None, :]
    expander = (jnp.arange(LANES)[None, :, None] == src).astype(BF16)
    in_specs = [
        pl.BlockSpec((tm, D_MODEL), lambda i: (i, 0)),
        pl.BlockSpec((T, LANES), lambda i: (0, i * reqs // group)),
        pl.BlockSpec((N_EXPERTS, reqs * cap, D_MODEL), lambda i: (0, i, 0)),
        pl.BlockSpec((None, None, 6, D_MODEL), lambda i: (layer, mod_row(i), 0, 0)),
        pl.BlockSpec((group, LANES, N_EXPERTS * cap), lambda i: (0, 0, 0)),
    ]
    args = [xn, pos_col, ye, mods, expander]
    if final:
        in_specs.append(pl.BlockSpec((1, D_MODEL), lambda i: (0, 0)))
        args.append(final_g)
    return pl.pallas_call(
        functools.partial(_combine_kernel, cap=cap, final=final),
        grid=(n // tm,),
        in_specs=in_specs,
        out_specs=pl.BlockSpec((tm, D_MODEL), lambda i: (i, 0)),
        out_shape=jax.ShapeDtypeStruct((n, D_MODEL), F32),
        compiler_params=_params("parallel"),
        name="combine",
    )(*args)


def _rope_tables(T):
    rows = T // GRID_W
    row = jnp.repeat(jnp.arange(rows, dtype=F32), GRID_W)
    col = jnp.tile(jnp.arange(GRID_W, dtype=F32), rows)
    inv = ROPE_BASE ** (-jnp.arange(0, ROPE_AXIS, 2, dtype=F32) / ROPE_AXIS)
    ar = row[:, None] * inv
    ac = col[:, None] * inv
    cos = jnp.concatenate([jnp.cos(ar), jnp.cos(ar), jnp.cos(ac), jnp.cos(ac)], axis=-1)
    sin = jnp.concatenate([-jnp.sin(ar), jnp.sin(ar), -jnp.sin(ac), jnp.sin(ac)], axis=-1)
    reps = D_QK // HEAD_DIM
    return jnp.tile(cos, (1, reps)), jnp.tile(sin, (1, reps))


def kernel(x_prompt, x_sample, c, cache_k, cache_v, c_ctx, norm1_g, w_mod, b_mod, w_in, conv_w, conv_b,
           conv_ln_g, conv_ln_b, lambda_q1, lambda_k1, lambda_q2, lambda_k2, subln_g, w_out, norm2_g,
           w_router, w_gate, w_up, w_down, final_norm_g):
    batch, seq, _ = x_prompt.shape
    dec_batch, dec_seq, _ = x_sample.shape
    past = cache_k.shape[2]
    assert 1 + dec_batch <= MOD_ROWS

    cvec = jnp.concatenate([c_ctx[None, :], c, jnp.zeros((MOD_ROWS - 1 - dec_batch, D_MODEL), F32)], axis=0)
    mods = _modulation(cvec, w_mod, b_mod)

    w_router_b = jnp.pad(w_router, ((0, 0), (0, 0), (0, LANES - N_EXPERTS))).astype(BF16)
    lam_vecs = jnp.stack([lambda_q1, lambda_k1, lambda_q2, lambda_k2], axis=1)
    vec = lambda a: a.reshape(DEPTH, 1, a.shape[-1])
    norm1_v, norm2_v, subln_v = vec(norm1_g), vec(norm2_g), vec(subln_g)
    conv_b_v, ln_g_v, ln_b_v = vec(conv_b), vec(conv_ln_g), vec(conv_ln_b)
    ctx_k = cache_k.reshape(dec_batch, DEPTH, past, D_QK)
    ctx_v = cache_v.reshape(dec_batch, DEPTH, past, D_ATTN)
    rope_tabs = _rope_tables(dec_seq)
    final_g = final_norm_g.reshape(1, D_MODEL)
    tri_p = (jnp.arange(seq)[:, None] <= jnp.arange(seq)[None, :]).astype(BF16)
    tri_s = (jnp.arange(dec_seq)[:, None] <= jnp.arange(dec_seq)[None, :]).astype(BF16)

    prompt_row = lambda i: 0
    assert dec_seq % ROW_TILE == 0
    tiles_per_sample = dec_seq // ROW_TILE
    sample_row = lambda i: 1 + i // tiles_per_sample

    xp = x_prompt.reshape(batch * seq, D_MODEL)
    xs = x_sample.reshape(dec_batch * dec_seq, D_MODEL)
    new_k, new_v = [], []
    for l in range(DEPTH):
        last = l == DEPTH - 1
        up, qp, kp, vp, kf, vf = _inproj(xp, mods, l, norm1_v, w_in, prompt_row, None, seq, True)
        us, qs, ks, vs = _inproj(xs, mods, l, norm1_v, w_in, sample_row, rope_tabs, dec_seq, False)
        new_k.append(kf.reshape(batch, seq, N_HEADS, 2, HEAD_DIM))
        new_v.append(vf.reshape(batch, seq, N_HEADS, V_DIM))
        cp = _conv(up, l, conv_w, conv_b_v, ln_g_v, ln_b_v, seq)
        cs = _conv(us, l, conv_w, conv_b_v, ln_g_v, ln_b_v, dec_seq)
        op = _attention(qp, kp, vp, lam_vecs, subln_v, l, seq, None)
        os_ = _attention(qs, ks, vs, lam_vecs, subln_v, l, dec_seq, (ctx_k, ctx_v))
        xnp, h2p, affp = _outproj(xp, cp, op, mods, l, w_out, norm2_v, w_router_b, prompt_row, seq)
        xns, h2s, affs = _outproj(xs, cs, os_, mods, l, w_out, norm2_v, w_router_b, sample_row, dec_seq)
        pos_p, rank_p = _select(affp, tri_p, EC_FACTOR * seq // N_EXPERTS)
        pos_s, rank_s = _select(affs, tri_s, EC_FACTOR * dec_seq // N_EXPERTS)
        slots_p, gates_p = _gather(pos_p, affp, h2p, seq)
        slots_s, gates_s = _gather(pos_s, affs, h2s, dec_seq)
        ye_p, ye_s = _experts(slots_p, slots_s, gates_p, gates_s, l, w_gate, w_up, w_down)
        xp = _combine(xnp, rank_p, ye_p, mods, l, prompt_row, seq, final_g if last else None)
        xs = _combine(xns, rank_s, ye_s, mods, l, sample_row, dec_seq, final_g if last else None)

    y_prompt = xp.reshape(batch, seq, D_MODEL)
    y_sample = xs.reshape(dec_batch, dec_seq, D_MODEL)
    return (y_prompt, y_sample, jnp.stack(new_k, axis=1), jnp.stack(new_v, axis=1))
```

```python
import functools
import math

import jax
import jax.numpy as jnp
from jax import lax
from jax.experimental import pallas as pl
from jax.experimental.pallas import tpu as pltpu

D_MODEL = 1024
DEPTH = 4
GRID_W = 64
D_CONV = 512
CONV_WIDTH = 31
N_HEADS = 4
HEAD_DIM = 64
V_DIM = 2 * HEAD_DIM
D_QK = N_HEADS * 2 * HEAD_DIM
D_ATTN = N_HEADS * V_DIM
D_MIX = D_CONV + D_ATTN
D_IN = 2 * D_CONV + 2 * D_QK + D_ATTN
N_EXPERTS = 16
EC_FACTOR = 2
D_EXPERT = 1024
ROPE_BASE = 10000.0
ROPE_AXIS = HEAD_DIM // 2
EPS = 1e-6
LOG2_E = math.log2(math.e)

LANES = 128
MOD_ROWS = 16
TOKEN_TILE = 512
ROW_TILE = 1024
SUB_ROWS = 256
CONV_CHUNK = 64
NORM_CHUNK = 256
CONV_PAD = 16
EXPERT_ROW_TILE = 512
MANTISSA_BITS = 23
VMEM_LIMIT = 56 * 1024 * 1024
EXPERT_VMEM_LIMIT = 60 * 1024 * 1024

F32 = jnp.float32
BF16 = jnp.bfloat16


def _sigmoid(x):
    return 1.0 / (1.0 + jnp.exp(-x))


def _params(*sem):
    return pltpu.CompilerParams(dimension_semantics=sem, vmem_limit_bytes=VMEM_LIMIT)


def _mod_kernel(c_ref, w_ref, b_ref, o_ref):
    c = c_ref[...]
    s = c * _sigmoid(c)
    o_ref[...] = jnp.dot(s.astype(BF16), w_ref[...].astype(BF16),
                         preferred_element_type=F32) + b_ref[...]


def _modulation(cvec, w_mod, b_mod):
    tn = 1536
    out = pl.pallas_call(
        _mod_kernel,
        grid=(DEPTH, 6 * D_MODEL // tn),
        in_specs=[
            pl.BlockSpec((MOD_ROWS, D_MODEL), lambda l, j: (0, 0)),
            pl.BlockSpec((None, D_MODEL, tn), lambda l, j: (l, 0, j)),
            pl.BlockSpec((None, 1, tn), lambda l, j: (l, 0, j)),
        ],
        out_specs=pl.BlockSpec((None, MOD_ROWS, tn), lambda l, j: (l, 0, j)),
        out_shape=jax.ShapeDtypeStruct((DEPTH, MOD_ROWS, 6 * D_MODEL), F32),
        compiler_params=_params("parallel", "parallel"),
        name="modulation",
    )(cvec, w_mod, b_mod.reshape(DEPTH, 1, 6 * D_MODEL))
    return out.reshape(DEPTH, MOD_ROWS, 6, D_MODEL)


def _inproj_kernel(*refs, rope, cache_out, aliased, seq_len):
    x_ref, mod_ref, g_ref, w_ref = refs[:4]
    refs = refs[4:]
    if rope:
        cos_ref, sin_ref = refs[:2]
        refs = refs[2:]
    if aliased:
        refs = refs[1:]
    u_ref, q_ref, k_ref, v_ref = refs[:4]
    wb_ref = refs[-1]

    @pl.when(pl.program_id(0) == 0)
    def _():
        wb_ref[...] = w_ref[...].astype(BF16)

    for t in range(x_ref.shape[0] // SUB_ROWS):
        rows = slice(t * SUB_ROWS, (t + 1) * SUB_ROWS)
        x = x_ref[rows, :]
        ms = jnp.mean(x * x, axis=-1, keepdims=True)
        y = x * lax.rsqrt(ms + EPS) * g_ref[...]
        h = y * (1.0 + mod_ref[1:2, :]) + mod_ref[0:1, :]
        z = jnp.dot(h.astype(BF16), wb_ref[...], preferred_element_type=F32)
        u_ref[rows, :] = z[:, :D_CONV] * _sigmoid(z[:, D_CONV:2 * D_CONV])
        q = z[:, 2 * D_CONV:2 * D_CONV + D_QK]
        k = z[:, 2 * D_CONV + D_QK:2 * D_CONV + 2 * D_QK]
        v = z[:, 2 * D_CONV + 2 * D_QK:]
        if cache_out:
            kf_ref, vf_ref = refs[4:6]
            kf_ref[rows, :] = k
            req, off = divmod(t * SUB_ROWS, seq_len)
            for hd in range(N_HEADS):
                vf_ref[req, off:off + SUB_ROWS, hd, :] = v[:, hd * V_DIM:(hd + 1) * V_DIM]
        if rope:
            lane = lax.broadcasted_iota(jnp.int32, q.shape, 1)
            first = (lane % ROPE_AXIS) < (ROPE_AXIS // 2)
            cos = cos_ref[rows, :]
            sin = sin_ref[rows, :]

            def rot(a):
                partner = jnp.where(first, pltpu.roll(a, D_QK - ROPE_AXIS // 2, 1),
                                    pltpu.roll(a, ROPE_AXIS // 2, 1))
                return a * cos + partner * sin

            q = rot(q)
            k = rot(k)
        q_ref[rows, :] = (q * (HEAD_DIM ** -0.5 * LOG2_E)).astype(BF16)
        k_ref[rows, :] = k.astype(BF16)
        v_ref[rows, :] = v.astype(BF16)


def _inproj(x, mods, layer, norm_g, w_in, mod_row, rope_tabs, seq_len, cache_out, v_cache=None):
    n = x.shape[0]
    tm = ROW_TILE
    assert rope_tabs is None or seq_len % tm == 0
    tiles_per_seq = max(1, seq_len // tm)
    rope = rope_tabs is not None
    aliased = v_cache is not None
    in_specs = [
        pl.BlockSpec((tm, D_MODEL), lambda i: (i, 0)),
        pl.BlockSpec((None, None, 6, D_MODEL), lambda i: (layer, mod_row(i), 0, 0)),
        pl.BlockSpec((None, 1, D_MODEL), lambda i: (layer, 0, 0)),
        pl.BlockSpec((None, D_MODEL, D_IN), lambda i: (layer, 0, 0), pipeline_mode=pl.Buffered(1)),
    ]
    args = [x, mods, norm_g, w_in]
    if rope:
        in_specs += [pl.BlockSpec((tm, D_QK), lambda i: (i % tiles_per_seq, 0))] * 2
        args += list(rope_tabs)
    half = pl.BlockSpec((tm, D_CONV), lambda i: (i, 0))
    out_specs = [half, half, half, half]
    out_shape = [jax.ShapeDtypeStruct((n, D_CONV), F32)] + [jax.ShapeDtypeStruct((n, D_QK), BF16)] * 3
    aliases = {}
    if cache_out:
        assert tm % seq_len == 0
        reqs = tm // seq_len
        out_specs += [half, pl.BlockSpec((reqs, None, seq_len, N_HEADS, V_DIM), lambda i: (i, layer, 0, 0, 0))]
        out_shape += [jax.ShapeDtypeStruct((n, D_QK), F32),
                      jax.ShapeDtypeStruct((n // seq_len, DEPTH, seq_len, N_HEADS, V_DIM), F32)]
        if aliased:
            aliases = {len(args): len(out_shape) - 1}
            in_specs.append(pl.BlockSpec(memory_space=pl.ANY))
            args.append(v_cache)
    return pl.pallas_call(
        functools.partial(_inproj_kernel, rope=rope, cache_out=cache_out, aliased=aliased, seq_len=seq_len),
        grid=(n // tm,),
        in_specs=in_specs,
        out_specs=out_specs,
        out_shape=out_shape,
        input_output_aliases=aliases,
        scratch_shapes=[pltpu.VMEM((D_MODEL, D_IN), BF16)],
        compiler_params=_params("arbitrary"),
        name="inproj",
    )(*args)


def _conv_stage(u_ref, w_ref, pad_ref, sh_ref, wb_ref):
    T = u_ref.shape[0]
    tail = CONV_PAD + T
    for k in range(CONV_WIDTH):
        wb_ref[k] = jnp.broadcast_to(w_ref[k:k + 1, :], (8, D_CONV))
    pad_ref[0:CONV_PAD, :] = jnp.zeros((CONV_PAD, D_CONV), F32)
    pad_ref[tail:tail + CONV_PAD, :] = jnp.zeros((CONV_PAD, D_CONV), F32)
    pad_ref[CONV_PAD:tail, :] = u_ref[...]
    rows = sh_ref.shape[1]
    for r in range(8):
        sh_ref[r] = pad_ref[r:r + rows, :]


def _conv_block(sh_ref, w, base, lanes):
    first = CONV_PAD - CONV_WIDTH // 2
    groups = CONV_CHUNK // 8
    acc = jnp.zeros((groups, 8, LANES), F32)
    for r in range(8):
        taps = [(k, (first + k) // 8) for k in range(CONV_WIDTH) if (first + k) % 8 == r]
        m0 = min(m for _, m in taps)
        span = max(m for _, m in taps) - m0 + groups
        x = sh_ref[r, pl.ds(base + 8 * m0, 8 * span), lanes].reshape(span, 8, LANES)
        for k, m in taps:
            acc = acc + w[k] * x[m - m0:m - m0 + groups]
    return acc.reshape(CONV_CHUNK, LANES)


def _conv_norm(acc, b_ref, lg_ref, lb_ref):
    acc = acc + b_ref[...]
    mu = jnp.mean(acc, axis=-1, keepdims=True)
    d = acc - mu
    var = jnp.mean(d * d, axis=-1, keepdims=True)
    un = d * lax.rsqrt(var + EPS) * lg_ref[...] + lb_ref[...]
    return (un * _sigmoid(un)).astype(BF16)


def _conv_kernel(u_ref, w_ref, b_ref, lg_ref, lb_ref, o_ref, pad_ref, sh_ref, wb_ref, acc_ref):
    T = u_ref.shape[0]
    _conv_stage(u_ref, w_ref, pad_ref, sh_ref, wb_ref)

    for q in range(D_CONV // LANES):
        lanes = slice(q * LANES, (q + 1) * LANES)
        w = [wb_ref[k, :, lanes] for k in range(CONV_WIDTH)]

        def taps_chunk(i, carry, lanes=lanes, w=w):
            base = pl.multiple_of(i * CONV_CHUNK, CONV_CHUNK)
            acc_ref[pl.ds(base, CONV_CHUNK), lanes] = _conv_block(sh_ref, w, base, lanes)
            return carry

        lax.fori_loop(0, T // CONV_CHUNK, taps_chunk, 0)

    def norm_chunk(i, carry):
        base = pl.multiple_of(i * NORM_CHUNK, NORM_CHUNK)
        o_ref[pl.ds(base, NORM_CHUNK), :] = _conv_norm(acc_ref[pl.ds(base, NORM_CHUNK), :],
                                                       b_ref, lg_ref, lb_ref)
        return carry

    lax.fori_loop(0, T // NORM_CHUNK, norm_chunk, 0)


def _conv(u, layer, conv_w, conv_b, ln_g, ln_b, seq_len):
    n = u.shape[0]
    T = seq_len
    rows = T + 2 * CONV_PAD - 8
    vec = pl.BlockSpec((None, 1, D_CONV), lambda b: (layer, 0, 0))
    return pl.pallas_call(
        _conv_kernel,
        grid=(n // T,),
        in_specs=[
            pl.BlockSpec((T, D_CONV), lambda b: (b, 0)),
            pl.BlockSpec((None, CONV_WIDTH, D_CONV), lambda b: (layer, 0, 0)),
            vec, vec, vec,
        ],
        out_specs=pl.BlockSpec((T, D_CONV), lambda b: (b, 0)),
        out_shape=jax.ShapeDtypeStruct((n, D_CONV), BF16),
        scratch_shapes=[pltpu.VMEM((T + 2 * CONV_PAD, D_CONV), F32),
                        pltpu.VMEM((8, rows, D_CONV), F32),
                        pltpu.VMEM((CONV_WIDTH, 8, D_CONV), F32),
                        pltpu.VMEM((T, D_CONV), F32)],
        compiler_params=_params("parallel"),
        name="conv",
    )(u, conv_w, conv_b, ln_g, ln_b)


def _attn_kernel(*refs, lam_init, has_ctx):
    lam_ref, sg_ref, q_ref, k_ref, v_ref = refs[:5]
    T = k_ref.shape[0]
    if has_ctx:
        ck_ref, cv_ref, o_ref, vals_ref, keys_ref = refs[5:]
    else:
        o_ref, vals_ref = refs[5:]
        keys_ref = k_ref

    @pl.when(pl.program_id(1) == 0)
    def _():
        vals_ref[...] = jnp.ones(vals_ref.shape, BF16)
        for h in range(N_HEADS):
            vals_ref[0:T, 2 * h * V_DIM:(2 * h + 1) * V_DIM] = v_ref[:, h * V_DIM:(h + 1) * V_DIM]
        if has_ctx:
            keys_ref[0:T, :] = k_ref[...]
            keys_ref[T:, :] = ck_ref[...].astype(BF16)
            for h in range(N_HEADS):
                vals_ref[T:, 2 * h * V_DIM:(2 * h + 1) * V_DIM] = cv_ref[:, h, :].astype(BF16)

    l = lam_ref[...]
    lam = (jnp.exp(jnp.sum(l[0:1, :] * l[1:2, :], axis=-1, keepdims=True))
           - jnp.exp(jnp.sum(l[2:3, :] * l[3:4, :], axis=-1, keepdims=True)) + lam_init)
    tq = q_ref.shape[0]
    lane = lax.broadcasted_iota(jnp.int32, (tq, V_DIM), 1)
    nt = (((1,), (1,)), ((), ()))
    for h in range(N_HEADS):
        sl = slice(h * V_DIM, (h + 1) * V_DIM)
        qh = q_ref[:, sl]
        kh = keys_ref[:, sl]
        heads = []
        for c in range(2):
            keep = (lane < HEAD_DIM) if c == 0 else (lane >= HEAD_DIM)
            qc = jnp.where(keep, qh, jnp.zeros_like(qh))
            s = lax.dot_general(qc, kh, nt, preferred_element_type=F32)
            e = jnp.exp2(s - jnp.max(s, axis=-1, keepdims=True)).astype(BF16)
            pv = jnp.dot(e, vals_ref[:, 2 * h * V_DIM:(2 * h + 2) * V_DIM], preferred_element_type=F32)
            heads.append(pv[:, :V_DIM] / pv[:, V_DIM:])
        o = heads[0] - lam * heads[1]
        o = o * lax.rsqrt(jnp.mean(o * o, axis=-1, keepdims=True) + EPS)
        o = o * sg_ref[...] * (1.0 - lam_init)
        o_ref[:, sl] = o.astype(BF16)


def _attention(q, k, v, lam_vecs, subln_g, layer, seq_len, ctx):
    n = q.shape[0]
    T = seq_len
    tq = min(TOKEN_TILE, T)
    nq = T // tq
    lam_init = 0.8 - 0.6 * math.exp(-0.3 * layer)
    has_ctx = ctx is not None
    kv = pl.BlockSpec((T, D_QK), lambda b, i: (b, 0))
    in_specs = [
        pl.BlockSpec((None, 4, HEAD_DIM), lambda b, i: (layer, 0, 0)),
        pl.BlockSpec((None, 1, V_DIM), lambda b, i: (layer, 0, 0)),
        pl.BlockSpec((tq, D_QK), lambda b, i: (b * nq + i, 0)),
        kv, kv,
    ]
    args = [lam_vecs, subln_g, q, k, v]
    past = ctx[0].shape[2] if has_ctx else 0
    scratch = [pltpu.VMEM((T + past, 2 * D_ATTN), BF16)]
    if has_ctx:
        in_specs += [pl.BlockSpec((None, None, past, D_QK), lambda b, i: (b, layer, 0, 0)),
                     pl.BlockSpec((None, None, past, N_HEADS, V_DIM), lambda b, i: (b, layer, 0, 0, 0))]
        args += list(ctx)
        scratch.append(pltpu.VMEM((T + past, D_QK), BF16))
    return pl.pallas_call(
        functools.partial(_attn_kernel, lam_init=lam_init, has_ctx=has_ctx),
        grid=(n // T, nq),
        in_specs=in_specs,
        out_specs=pl.BlockSpec((tq, D_ATTN), lambda b, i: (b * nq + i, 0)),
        out_shape=jax.ShapeDtypeStruct((n, D_ATTN), BF16),
        scratch_shapes=scratch,
        compiler_params=_params("parallel", "arbitrary"),
        name="attention",
    )(*args)


def _outproj_kernel(x_ref, mc_ref, ma_ref, mod_ref, w_ref, g_ref, wr_ref, xn_ref, h2_ref, aff_ref, wb_ref,
                    *, seq_len):
    @pl.when(pl.program_id(0) == 0)
    def _():
        wb_ref[...] = w_ref[...].astype(BF16)

    for t in range(x_ref.shape[0] // SUB_ROWS):
        rows = slice(t * SUB_ROWS, (t + 1) * SUB_ROWS)
        req, off = divmod(t * SUB_ROWS, seq_len)
        acc = jnp.dot(mc_ref[rows, :], wb_ref[0:D_CONV, :], preferred_element_type=F32)
        acc = acc + jnp.dot(ma_ref[rows, :], wb_ref[D_CONV:D_MIX, :], preferred_element_type=F32)
        x = x_ref[rows, :] + mod_ref[2:3, :] * acc
        xn_ref[rows, :] = x
        ms = jnp.mean(x * x, axis=-1, keepdims=True)
        y = x * lax.rsqrt(ms + EPS) * g_ref[...]
        h2 = (y * (1.0 + mod_ref[4:5, :]) + mod_ref[3:4, :]).astype(BF16)
        h2_ref[rows, :] = h2
        logits = jnp.dot(h2, wr_ref[...], preferred_element_type=F32)
        lane = lax.broadcasted_iota(jnp.int32, logits.shape, 1)
        logits = jnp.where(lane < N_EXPERTS, logits, -jnp.inf)
        e = jnp.exp(logits - jnp.max(logits, axis=-1, keepdims=True))
        aff = e / jnp.sum(e, axis=-1, keepdims=True)
        aff_ref[req * N_EXPERTS:(req + 1) * N_EXPERTS, off:off + SUB_ROWS] = aff.T[0:N_EXPERTS, :]


def _outproj(x, mix_conv, mix_attn, mods, layer, w_out, norm_g, w_router_b, mod_row, seq_len):
    n = x.shape[0]
    tm = ROW_TILE
    assert tm % seq_len == 0 and seq_len % SUB_ROWS == 0
    reqs = tm // seq_len
    row = lambda i: (i, 0)
    return pl.pallas_call(
        functools.partial(_outproj_kernel, seq_len=seq_len),
        grid=(n // tm,),
        in_specs=[
            pl.BlockSpec((tm, D_MODEL), row),
            pl.BlockSpec((tm, D_CONV), row),
            pl.BlockSpec((tm, D_ATTN), row),
            pl.BlockSpec((None, None, 6, D_MODEL), lambda i: (layer, mod_row(i), 0, 0)),
            pl.BlockSpec((None, D_MIX, D_MODEL), lambda i: (layer, 0, 0), pipeline_mode=pl.Buffered(1)),
            pl.BlockSpec((None, 1, D_MODEL), lambda i: (layer, 0, 0)),
            pl.BlockSpec((None, D_MODEL, LANES), lambda i: (layer, 0, 0)),
        ],
        out_specs=[pl.BlockSpec((tm, D_MODEL), row), pl.BlockSpec((tm, D_MODEL), row),
                   pl.BlockSpec((reqs * N_EXPERTS, seq_len), row)],
        out_shape=[jax.ShapeDtypeStruct((n, D_MODEL), F32), jax.ShapeDtypeStruct((n, D_MODEL), BF16),
                   jax.ShapeDtypeStruct((n // seq_len * N_EXPERTS, seq_len), F32)],
        scratch_shapes=[pltpu.VMEM((D_MIX, D_MODEL), BF16)],
        compiler_params=_params("arbitrary"),
        name="outproj",
    )(x, mix_conv, mix_attn, mods, w_out, norm_g, w_router_b)


def _top_slots(a, tri, cap):
    rows = a.shape[0]
    capf = float(cap)

    def count_ge(v):
        return jnp.sum(jnp.where(a >= v, 1.0, 0.0), axis=-1, keepdims=True)

    hi = jnp.full((rows, 1), 2.0, F32)
    for j in range(6, -1, -1):
        cand = hi * (2.0 ** -(2 ** j))
        hi = jnp.where(count_ge(cand) < capf, cand, hi)
    lo = hi * 0.5
    step = lo * 0.5
    for _ in range(MANTISSA_BITS):
        cand = lo + step
        lo = jnp.where(count_ge(cand) >= capf, cand, lo)
        step = step * 0.5
    upper = jnp.where(step > 0.0, lo + step * 2.0, hi)
    above = jnp.where(a >= upper, 1.0, 0.0)
    tie = jnp.where(a >= lo, 1.0, 0.0) - above
    need = capf - jnp.sum(above, axis=-1, keepdims=True)
    tie_rank = jnp.dot(tie.astype(BF16), tri, preferred_element_type=F32)
    sel = above + tie * jnp.where(tie_rank <= need, 1.0, 0.0)
    slot = jnp.dot(sel.astype(BF16), tri, preferred_element_type=F32) - 1.0
    return jnp.where(sel > 0.5, slot, -1.0)


def _select_kernel(ap_ref, tp_ref, as_ref, ts_ref, pp_ref, pcp_ref, ps_ref, pcs_ref, *, caps):
    pos_p = _top_slots(ap_ref[...], tp_ref[...], caps[0])
    pos_s = _top_slots(as_ref[...], ts_ref[...], caps[1])
    pp_ref[...] = pos_p
    pcp_ref[...] = pos_p.T
    ps_ref[...] = pos_s
    pcs_ref[...] = pos_s.T


def _select(aff_p, tri_p, cap_p, aff_s, tri_s, cap_s):
    whole = lambda a: pl.BlockSpec(a.shape, lambda i: (0, 0))
    outs = [jax.ShapeDtypeStruct(aff_p.shape, F32), jax.ShapeDtypeStruct(aff_p.shape[::-1], F32),
            jax.ShapeDtypeStruct(aff_s.shape, F32), jax.ShapeDtypeStruct(aff_s.shape[::-1], F32)]
    return pl.pallas_call(
        functools.partial(_select_kernel, caps=(cap_p, cap_s)),
        grid=(1,),
        in_specs=[whole(aff_p), whole(tri_p), whole(aff_s), whole(tri_s)],
        out_specs=[whole(o) for o in outs],
        out_shape=outs,
        compiler_params=_params("arbitrary"),
        name="select",
    )(aff_p, tri_p, aff_s, tri_s)


def _gather_kernel(pos_ref, a_ref, h2_ref, xs_ref, gs_ref, oh_ref, *, cap):
    T = h2_ref.shape[0]
    slot = lax.broadcasted_iota(jnp.int32, (cap, T), 0).astype(F32)
    for e in range(N_EXPERTS):
        onehot = jnp.where(pos_ref[e:e + 1, :] == slot, 1.0, 0.0)
        oh_ref[e * cap:(e + 1) * cap, :] = onehot.astype(BF16)
        gs_ref[e] = jnp.sum(onehot * a_ref[e:e + 1, :], axis=-1, keepdims=True)
    xs = jnp.dot(oh_ref[...], h2_ref[...], preferred_element_type=F32)
    xs_ref[...] = xs.reshape(N_EXPERTS, cap, D_MODEL).astype(BF16)


def _gather(pos, aff_t, h2, seq_len):
    T = seq_len
    nreq = h2.shape[0] // T
    cap = EC_FACTOR * T // N_EXPERTS
    rows = pl.BlockSpec((N_EXPERTS, T), lambda b: (b, 0))
    return pl.pallas_call(
        functools.partial(_gather_kernel, cap=cap),
        grid=(nreq,),
        in_specs=[rows, rows, pl.BlockSpec((T, D_MODEL), lambda b: (b, 0))],
        out_specs=[pl.BlockSpec((N_EXPERTS, cap, D_MODEL), lambda b: (0, b, 0)),
                   pl.BlockSpec((N_EXPERTS, cap, 1), lambda b: (0, b, 0))],
        out_shape=[jax.ShapeDtypeStruct((N_EXPERTS, nreq * cap, D_MODEL), BF16),
                   jax.ShapeDtypeStruct((N_EXPERTS, nreq * cap, 1), F32)],
        scratch_shapes=[pltpu.VMEM((N_EXPERTS * cap, T), BF16)],
        compiler_params=_params("parallel"),
        name="gather",
    )(pos, aff_t, h2)


def _expert_kernel(xp_ref, xs_ref, gp_ref, gsm_ref, wg_ref, wu_ref, wd_ref, yp_ref, ys_ref,
                   wgb_ref, wub_ref, wdb_ref):
    wgb_ref[...] = wg_ref[...].astype(BF16)
    wub_ref[...] = wu_ref[...].astype(BF16)
    wdb_ref[...] = wd_ref[...].astype(BF16)
    for x_ref, g_ref, y_ref in ((xp_ref, gp_ref, yp_ref), (xs_ref, gsm_ref, ys_ref)):
        for t in range(x_ref.shape[0] // EXPERT_ROW_TILE):
            rows = slice(t * EXPERT_ROW_TILE, (t + 1) * EXPERT_ROW_TILE)
            x = x_ref[rows, :]
            hg = jnp.dot(x, wgb_ref[...], preferred_element_type=F32)
            hu = jnp.dot(x, wub_ref[...], preferred_element_type=F32)
            hid = (hg * _sigmoid(hg) * hu).astype(BF16)
            y = jnp.dot(hid, wdb_ref[...], preferred_element_type=F32)
            y_ref[rows, :] = (y * g_ref[rows, :]).astype(BF16)


def _experts(xs_p, xs_s, gs_p, gs_s, layer, w_gate, w_up, w_down):
    np_, ns_ = xs_p.shape[1], xs_s.shape[1]
    slots = lambda n, w: pl.BlockSpec((None, n, w), lambda e: (e, 0, 0))
    weight = lambda rows, cols: pl.BlockSpec((None, None, rows, cols), lambda e: (layer, e, 0, 0))
    return pl.pallas_call(
        _expert_kernel,
        grid=(N_EXPERTS,),
        in_specs=[
            slots(np_, D_MODEL), slots(ns_, D_MODEL), slots(np_, 1), slots(ns_, 1),
            weight(D_MODEL, D_EXPERT), weight(D_MODEL, D_EXPERT), weight(D_EXPERT, D_MODEL),
        ],
        out_specs=[slots(np_, D_MODEL), slots(ns_, D_MODEL)],
        out_shape=[jax.ShapeDtypeStruct(xs_p.shape, BF16), jax.ShapeDtypeStruct(xs_s.shape, BF16)],
        scratch_shapes=[pltpu.VMEM((D_MODEL, D_EXPERT), BF16), pltpu.VMEM((D_MODEL, D_EXPERT), BF16),
                        pltpu.VMEM((D_EXPERT, D_MODEL), BF16)],
        compiler_params=pltpu.CompilerParams(dimension_semantics=("parallel",),
                                             vmem_limit_bytes=EXPERT_VMEM_LIMIT),
        name="experts",
    )(xs_p, xs_s, gs_p, gs_s, w_gate, w_up, w_down)


def _combine_kernel(*refs, cap, final):
    xn_ref, pc_ref, ye_ref, mod_ref, ex_ref = refs[:5]
    if final:
        fg_ref = refs[5]
    o_ref = refs[-1]
    seq_len = pc_ref.shape[0]
    reqs = xn_ref.shape[0] // seq_len
    first_req = pl.program_id(0) * reqs
    for t in range(xn_ref.shape[0] // SUB_ROWS):
        rows = slice(t * SUB_ROWS, (t + 1) * SUB_ROWS)
        req, off = divmod(t * SUB_ROWS, seq_len)
        ye = ye_ref[:, req * cap:(req + 1) * cap, :].reshape(N_EXPERTS * cap, D_MODEL)
        expander = ex_ref[(first_req + req) % ex_ref.shape[0]]
        pos = jnp.dot(pc_ref[off:off + SUB_ROWS, :].astype(BF16), expander, preferred_element_type=F32)
        lane = lax.broadcasted_iota(jnp.int32, pos.shape, 1)
        slot = (lane % cap).astype(F32)
        onehot = jnp.where(pos == slot, 1.0, 0.0).astype(BF16)
        moe = jnp.dot(onehot, ye, preferred_element_type=F32)
        x = xn_ref[rows, :] + mod_ref[5:6, :] * moe
        if final:
            x = x * lax.rsqrt(jnp.mean(x * x, axis=-1, keepdims=True) + EPS) * fg_ref[...]
        o_ref[rows, :] = x


def _combine(xn, pos_col, ye, mods, layer, mod_row, seq_len, final_g):
    n = xn.shape[0]
    T = seq_len
    tm = ROW_TILE
    reqs = tm // T
    cap = EC_FACTOR * T // N_EXPERTS
    final = final_g is not None
    group = LANES // N_EXPERTS
    assert tm % T == 0 and T % SUB_ROWS == 0 and group % reqs == 0
    src = jnp.arange(group)[:, None, None] * N_EXPERTS + (jnp.arange(N_EXPERTS * cap) // cap)[None, None, :]
    expander = (jnp.arange(LANES)[None, :, None] == src).astype(BF16)
    in_specs = [
        pl.BlockSpec((tm, D_MODEL), lambda i: (i, 0)),
        pl.BlockSpec((T, LANES), lambda i: (0, i * reqs // group)),
        pl.BlockSpec((N_EXPERTS, reqs * cap, D_MODEL), lambda i: (0, i, 0)),
        pl.BlockSpec((None, None, 6, D_MODEL), lambda i: (layer, mod_row(i), 0, 0)),
        pl.BlockSpec((group, LANES, N_EXPERTS * cap), lambda i: (0, 0, 0)),
    ]
    args = [xn, pos_col, ye, mods, expander]
    if final:
        in_specs.append(pl.BlockSpec((1, D_MODEL), lambda i: (0, 0)))
        args.append(final_g)
    return pl.pallas_call(
        functools.partial(_combine_kernel, cap=cap, final=final),
        grid=(n // tm,),
        in_specs=in_specs,
        out_specs=pl.BlockSpec((tm, D_MODEL), lambda i: (i, 0)),
        out_shape=jax.ShapeDtypeStruct((n, D_MODEL), F32),
        compiler_params=_params("parallel"),
        name="combine",
    )(*args)


def _rope_tables(T):
    rows = T // GRID_W
    row = jnp.repeat(jnp.arange(rows, dtype=F32), GRID_W)
    col = jnp.tile(jnp.arange(GRID_W, dtype=F32), rows)
    inv = ROPE_BASE ** (-jnp.arange(0, ROPE_AXIS, 2, dtype=F32) / ROPE_AXIS)
    ar = row[:, None] * inv
    ac = col[:, None] * inv
    cos = jnp.concatenate([jnp.cos(ar), jnp.cos(ar), jnp.cos(ac), jnp.cos(ac)], axis=-1)
    sin = jnp.concatenate([-jnp.sin(ar), jnp.sin(ar), -jnp.sin(ac), jnp.sin(ac)], axis=-1)
    reps = D_QK // HEAD_DIM
    return jnp.tile(cos, (1, reps)), jnp.tile(sin, (1, reps))


def kernel(x_prompt, x_sample, c, cache_k, cache_v, c_ctx, norm1_g, w_mod, b_mod, w_in, conv_w, conv_b,
           conv_ln_g, conv_ln_b, lambda_q1, lambda_k1, lambda_q2, lambda_k2, subln_g, w_out, norm2_g,
           w_router, w_gate, w_up, w_down, final_norm_g):
    batch, seq, _ = x_prompt.shape
    dec_batch, dec_seq, _ = x_sample.shape
    past = cache_k.shape[2]
    ctx_k = cache_k.reshape(dec_batch, DEPTH, past, D_QK)
    assert 1 + dec_batch <= MOD_ROWS

    cvec = jnp.concatenate([c_ctx[None, :], c, jnp.zeros((MOD_ROWS - 1 - dec_batch, D_MODEL), F32)], axis=0)
    mods = _modulation(cvec, w_mod, b_mod)

    w_router_b = jnp.pad(w_router, ((0, 0), (0, 0), (0, LANES - N_EXPERTS))).astype(BF16)
    lam_vecs = jnp.stack([lambda_q1, lambda_k1, lambda_q2, lambda_k2], axis=1)
    vec = lambda a: a.reshape(DEPTH, 1, a.shape[-1])
    norm1_v, norm2_v, subln_v = vec(norm1_g), vec(norm2_g), vec(subln_g)
    conv_b_v, ln_g_v, ln_b_v = vec(conv_b), vec(conv_ln_g), vec(conv_ln_b)
    rope_tabs = _rope_tables(dec_seq)
    final_g = final_norm_g.reshape(1, D_MODEL)
    tri_p = (jnp.arange(seq)[:, None] <= jnp.arange(seq)[None, :]).astype(BF16)
    tri_s = (jnp.arange(dec_seq)[:, None] <= jnp.arange(dec_seq)[None, :]).astype(BF16)

    prompt_row = lambda i: 0
    assert dec_seq % ROW_TILE == 0
    tiles_per_sample = dec_seq // ROW_TILE
    sample_row = lambda i: 1 + i // tiles_per_sample

    xp = x_prompt.reshape(batch * seq, D_MODEL)
    xs = x_sample.reshape(dec_batch * dec_seq, D_MODEL)
    new_k, new_v = [], None
    for l in range(DEPTH):
        last = l == DEPTH - 1
        up, qp, kp, vp, kf, new_v = _inproj(xp, mods, l, norm1_v, w_in, prompt_row, None, seq, True, new_v)
        us, qs, ks, vs = _inproj(xs, mods, l, norm1_v, w_in, sample_row, rope_tabs, dec_seq, False)
        new_k.append(kf.reshape(batch, seq, N_HEADS, 2, HEAD_DIM))
        cp = _conv(up, l, conv_w, conv_b_v, ln_g_v, ln_b_v, seq)
        cs = _conv(us, l, conv_w, conv_b_v, ln_g_v, ln_b_v, dec_seq)
        op = _attention(qp, kp, vp, lam_vecs, subln_v, l, seq, None)
        os_ = _attention(qs, ks, vs, lam_vecs, subln_v, l, dec_seq, (ctx_k, cache_v))
        xnp, h2p, affp = _outproj(xp, cp, op, mods, l, w_out, norm2_v, w_router_b, prompt_row, seq)
        xns, h2s, affs = _outproj(xs, cs, os_, mods, l, w_out, norm2_v, w_router_b, sample_row, dec_seq)
        pos_p, rank_p, pos_s, rank_s = _select(affp, tri_p, EC_FACTOR * seq // N_EXPERTS,
                                               affs, tri_s, EC_FACTOR * dec_seq // N_EXPERTS)
        slots_p, gates_p = _gather(pos_p, affp, h2p, seq)
        slots_s, gates_s = _gather(pos_s, affs, h2s, dec_seq)
        ye_p, ye_s = _experts(slots_p, slots_s, gates_p, gates_s, l, w_gate, w_up, w_down)
        xp = _combine(xnp, rank_p, ye_p, mods, l, prompt_row, seq, final_g if last else None)
        xs = _combine(xns, rank_s, ye_s, mods, l, sample_row, dec_seq, final_g if last else None)

    y_prompt = xp.reshape(batch, seq, D_MODEL)
    y_sample = xs.reshape(dec_batch, dec_seq, D_MODEL)
    return (y_prompt, y_sample, jnp.stack(new_k, axis=1), new_v)
```

```python
import functools
import math

import jax
import jax.numpy as jnp
from jax import lax
from jax.experimental import pallas as pl
from jax.experimental.pallas import tpu as pltpu

D_MODEL = 1024
DEPTH = 4
GRID_W = 64
D_CONV = 512
CONV_WIDTH = 31
N_HEADS = 4
HEAD_DIM = 64
V_DIM = 2 * HEAD_DIM
D_QK = N_HEADS * 2 * HEAD_DIM
D_ATTN = N_HEADS * V_DIM
D_MIX = D_CONV + D_ATTN
D_IN = 2 * D_CONV + 2 * D_QK + D_ATTN
N_EXPERTS = 16
EC_FACTOR = 2
D_EXPERT = 1024
ROPE_BASE = 10000.0
ROPE_AXIS = HEAD_DIM // 2
EPS = 1e-6
LOG2_E = math.log2(math.e)

LANES = 128
MOD_ROWS = 16
TOKEN_TILE = 512
ROW_TILE = 1024
SUB_ROWS = 256
CONV_CHUNK = 64
NORM_CHUNK = 256
CONV_PAD = 16
EXPERT_ROW_TILE = 512
MANTISSA_BITS = 23
VMEM_LIMIT = 56 * 1024 * 1024
EXPERT_VMEM_LIMIT = 60 * 1024 * 1024

F32 = jnp.float32
BF16 = jnp.bfloat16


def _sigmoid(x):
    return 1.0 / (1.0 + jnp.exp(-x))


def _params(*sem):
    return pltpu.CompilerParams(dimension_semantics=sem, vmem_limit_bytes=VMEM_LIMIT)


def _mod_kernel(c_ref, w_ref, b_ref, o_ref):
    c = c_ref[...]
    s = c * _sigmoid(c)
    o_ref[...] = jnp.dot(s.astype(BF16), w_ref[...].astype(BF16),
                         preferred_element_type=F32) + b_ref[...]


def _modulation(cvec, w_mod, b_mod):
    tn = 1536
    out = pl.pallas_call(
        _mod_kernel,
        grid=(DEPTH, 6 * D_MODEL // tn),
        in_specs=[
            pl.BlockSpec((MOD_ROWS, D_MODEL), lambda l, j: (0, 0)),
            pl.BlockSpec((None, D_MODEL, tn), lambda l, j: (l, 0, j)),
            pl.BlockSpec((None, 1, tn), lambda l, j: (l, 0, j)),
        ],
        out_specs=pl.BlockSpec((None, MOD_ROWS, tn), lambda l, j: (l, 0, j)),
        out_shape=jax.ShapeDtypeStruct((DEPTH, MOD_ROWS, 6 * D_MODEL), F32),
        compiler_params=_params("parallel", "parallel"),
        name="modulation",
    )(cvec, w_mod, b_mod.reshape(DEPTH, 1, 6 * D_MODEL))
    return out.reshape(DEPTH, MOD_ROWS, 6, D_MODEL)


def _inproj_kernel(*refs, rope, cache_out, aliased, seq_len):
    x_ref, mod_ref, g_ref, w_ref = refs[:4]
    refs = refs[4:]
    if rope:
        cos_ref, sin_ref = refs[:2]
        refs = refs[2:]
    if aliased:
        refs = refs[2:]
    u_ref, q_ref, k_ref, v_ref = refs[:4]
    wb_ref = refs[-1]

    @pl.when(pl.program_id(0) == 0)
    def _():
        wb_ref[...] = w_ref[...].astype(BF16)

    for t in range(x_ref.shape[0] // SUB_ROWS):
        rows = slice(t * SUB_ROWS, (t + 1) * SUB_ROWS)
        x = x_ref[rows, :]
        ms = jnp.mean(x * x, axis=-1, keepdims=True)
        y = x * lax.rsqrt(ms + EPS) * g_ref[...]
        h = y * (1.0 + mod_ref[1:2, :]) + mod_ref[0:1, :]
        z = jnp.dot(h.astype(BF16), wb_ref[...], preferred_element_type=F32)
        u_ref[rows, :] = z[:, :D_CONV] * _sigmoid(z[:, D_CONV:2 * D_CONV])
        q = z[:, 2 * D_CONV:2 * D_CONV + D_QK]
        k = z[:, 2 * D_CONV + D_QK:2 * D_CONV + 2 * D_QK]
        v = z[:, 2 * D_CONV + 2 * D_QK:]
        if cache_out:
            kf_ref, vf_ref = refs[4:6]
            req, off = divmod(t * SUB_ROWS, seq_len)
            kf_ref[req, :, off:off + SUB_ROWS] = k.T
            for hd in range(N_HEADS):
                vf_ref[req, off:off + SUB_ROWS, hd, :] = v[:, hd * V_DIM:(hd + 1) * V_DIM]
        if rope:
            lane = lax.broadcasted_iota(jnp.int32, q.shape, 1)
            first = (lane % ROPE_AXIS) < (ROPE_AXIS // 2)
            cos = cos_ref[rows, :]
            sin = sin_ref[rows, :]

            def rot(a):
                partner = jnp.where(first, pltpu.roll(a, D_QK - ROPE_AXIS // 2, 1),
                                    pltpu.roll(a, ROPE_AXIS // 2, 1))
                return a * cos + partner * sin

            q = rot(q)
            k = rot(k)
        q_ref[rows, :] = (q * (HEAD_DIM ** -0.5 * LOG2_E)).astype(BF16)
        k_ref[rows, :] = k.astype(BF16)
        v_ref[rows, :] = v.astype(BF16)


def _inproj(x, mods, layer, norm_g, w_in, mod_row, rope_tabs, seq_len, cache_out, caches=None):
    n = x.shape[0]
    tm = ROW_TILE
    assert rope_tabs is None or seq_len % tm == 0
    tiles_per_seq = max(1, seq_len // tm)
    rope = rope_tabs is not None
    aliased = caches is not None
    in_specs = [
        pl.BlockSpec((tm, D_MODEL), lambda i: (i, 0)),
        pl.BlockSpec((None, None, 6, D_MODEL), lambda i: (layer, mod_row(i), 0, 0)),
        pl.BlockSpec((None, 1, D_MODEL), lambda i: (layer, 0, 0)),
        pl.BlockSpec((None, D_MODEL, D_IN), lambda i: (layer, 0, 0), pipeline_mode=pl.Buffered(1)),
    ]
    args = [x, mods, norm_g, w_in]
    if rope:
        in_specs += [pl.BlockSpec((tm, D_QK), lambda i: (i % tiles_per_seq, 0))] * 2
        args += list(rope_tabs)
    half = pl.BlockSpec((tm, D_CONV), lambda i: (i, 0))
    out_specs = [half, half, half, half]
    out_shape = [jax.ShapeDtypeStruct((n, D_CONV), F32)] + [jax.ShapeDtypeStruct((n, D_QK), BF16)] * 3
    aliases = {}
    if cache_out:
        assert tm % seq_len == 0
        reqs = tm // seq_len
        out_specs += [pl.BlockSpec((reqs, None, D_QK, seq_len), lambda i: (i, layer, 0, 0)),
                      pl.BlockSpec((reqs, None, seq_len, N_HEADS, V_DIM), lambda i: (i, layer, 0, 0, 0))]
        out_shape += [jax.ShapeDtypeStruct((n // seq_len, DEPTH, D_QK, seq_len), F32),
                      jax.ShapeDtypeStruct((n // seq_len, DEPTH, seq_len, N_HEADS, V_DIM), F32)]
        if aliased:
            aliases = {len(args): len(out_shape) - 2, len(args) + 1: len(out_shape) - 1}
            in_specs += [pl.BlockSpec(memory_space=pl.ANY)] * 2
            args += list(caches)
    return pl.pallas_call(
        functools.partial(_inproj_kernel, rope=rope, cache_out=cache_out, aliased=aliased, seq_len=seq_len),
        grid=(n // tm,),
        in_specs=in_specs,
        out_specs=out_specs,
        out_shape=out_shape,
        input_output_aliases=aliases,
        scratch_shapes=[pltpu.VMEM((D_MODEL, D_IN), BF16)],
        compiler_params=_params("arbitrary"),
        name="inproj",
    )(*args)


def _conv_stage(u_ref, w_ref, pad_ref, sh_ref, wb_ref):
    T = u_ref.shape[0]
    tail = CONV_PAD + T
    for k in range(CONV_WIDTH):
        wb_ref[k] = jnp.broadcast_to(w_ref[k:k + 1, :], (8, D_CONV))
    pad_ref[0:CONV_PAD, :] = jnp.zeros((CONV_PAD, D_CONV), F32)
    pad_ref[tail:tail + CONV_PAD, :] = jnp.zeros((CONV_PAD, D_CONV), F32)
    pad_ref[CONV_PAD:tail, :] = u_ref[...]
    rows = sh_ref.shape[1]
    for r in range(8):
        sh_ref[r] = pad_ref[r:r + rows, :]


def _conv_block(sh_ref, w, base, lanes):
    first = CONV_PAD - CONV_WIDTH // 2
    groups = CONV_CHUNK // 8
    acc = jnp.zeros((groups, 8, LANES), F32)
    for r in range(8):
        taps = [(k, (first + k) // 8) for k in range(CONV_WIDTH) if (first + k) % 8 == r]
        m0 = min(m for _, m in taps)
        span = max(m for _, m in taps) - m0 + groups
        x = sh_ref[r, pl.ds(base + 8 * m0, 8 * span), lanes].reshape(span, 8, LANES)
        for k, m in taps:
            acc = acc + w[k] * x[m - m0:m - m0 + groups]
    return acc.reshape(CONV_CHUNK, LANES)


def _conv_norm(acc, b_ref, lg_ref, lb_ref):
    acc = acc + b_ref[...]
    mu = jnp.mean(acc, axis=-1, keepdims=True)
    d = acc - mu
    var = jnp.mean(d * d, axis=-1, keepdims=True)
    un = d * lax.rsqrt(var + EPS) * lg_ref[...] + lb_ref[...]
    return (un * _sigmoid(un)).astype(BF16)


def _conv_kernel(u_ref, w_ref, b_ref, lg_ref, lb_ref, o_ref, pad_ref, sh_ref, wb_ref, acc_ref):
    T = u_ref.shape[0]
    _conv_stage(u_ref, w_ref, pad_ref, sh_ref, wb_ref)

    for q in range(D_CONV // LANES):
        lanes = slice(q * LANES, (q + 1) * LANES)
        w = [wb_ref[k, :, lanes] for k in range(CONV_WIDTH)]

        def taps_chunk(i, carry, lanes=lanes, w=w):
            base = pl.multiple_of(i * CONV_CHUNK, CONV_CHUNK)
            acc_ref[pl.ds(base, CONV_CHUNK), lanes] = _conv_block(sh_ref, w, base, lanes)
            return carry

        lax.fori_loop(0, T // CONV_CHUNK, taps_chunk, 0)

    def norm_chunk(i, carry):
        base = pl.multiple_of(i * NORM_CHUNK, NORM_CHUNK)
        o_ref[pl.ds(base, NORM_CHUNK), :] = _conv_norm(acc_ref[pl.ds(base, NORM_CHUNK), :],
                                                       b_ref, lg_ref, lb_ref)
        return carry

    lax.fori_loop(0, T // NORM_CHUNK, norm_chunk, 0)


def _conv(u, layer, conv_w, conv_b, ln_g, ln_b, seq_len):
    n = u.shape[0]
    T = seq_len
    rows = T + 2 * CONV_PAD - 8
    vec = pl.BlockSpec((None, 1, D_CONV), lambda b: (layer, 0, 0))
    return pl.pallas_call(
        _conv_kernel,
        grid=(n // T,),
        in_specs=[
            pl.BlockSpec((T, D_CONV), lambda b: (b, 0)),
            pl.BlockSpec((None, CONV_WIDTH, D_CONV), lambda b: (layer, 0, 0)),
            vec, vec, vec,
        ],
        out_specs=pl.BlockSpec((T, D_CONV), lambda b: (b, 0)),
        out_shape=jax.ShapeDtypeStruct((n, D_CONV), BF16),
        scratch_shapes=[pltpu.VMEM((T + 2 * CONV_PAD, D_CONV), F32),
                        pltpu.VMEM((8, rows, D_CONV), F32),
                        pltpu.VMEM((CONV_WIDTH, 8, D_CONV), F32),
                        pltpu.VMEM((T, D_CONV), F32)],
        compiler_params=_params("parallel"),
        name="conv",
    )(u, conv_w, conv_b, ln_g, ln_b)


def _attn_kernel(*refs, lam_init, has_ctx):
    lam_ref, sg_ref, q_ref, k_ref, v_ref = refs[:5]
    T = k_ref.shape[0]
    if has_ctx:
        ck_ref, cv_ref, o_ref, vals_ref, keys_ref = refs[5:]
    else:
        o_ref, vals_ref = refs[5:]
        keys_ref = k_ref

    @pl.when(pl.program_id(1) == 0)
    def _():
        vals_ref[...] = jnp.ones(vals_ref.shape, BF16)
        for h in range(N_HEADS):
            vals_ref[0:T, 2 * h * V_DIM:(2 * h + 1) * V_DIM] = v_ref[:, h * V_DIM:(h + 1) * V_DIM]
        if has_ctx:
            keys_ref[0:T, :] = k_ref[...]
            keys_ref[T:, :] = ck_ref[...].astype(BF16)
            for h in range(N_HEADS):
                vals_ref[T:, 2 * h * V_DIM:(2 * h + 1) * V_DIM] = cv_ref[:, h, :].astype(BF16)

    l = lam_ref[...]
    lam = (jnp.exp(jnp.sum(l[0:1, :] * l[1:2, :], axis=-1, keepdims=True))
           - jnp.exp(jnp.sum(l[2:3, :] * l[3:4, :], axis=-1, keepdims=True)) + lam_init)
    tq = q_ref.shape[0]
    lane = lax.broadcasted_iota(jnp.int32, (tq, V_DIM), 1)
    nt = (((1,), (1,)), ((), ()))
    for h in range(N_HEADS):
        sl = slice(h * V_DIM, (h + 1) * V_DIM)
        qh = q_ref[:, sl]
        kh = keys_ref[:, sl]
        heads = []
        for c in range(2):
            keep = (lane < HEAD_DIM) if c == 0 else (lane >= HEAD_DIM)
            qc = jnp.where(keep, qh, jnp.zeros_like(qh))
            s = lax.dot_general(qc, kh, nt, preferred_element_type=F32)
            e = jnp.exp2(s - jnp.max(s, axis=-1, keepdims=True)).astype(BF16)
            pv = jnp.dot(e, vals_ref[:, 2 * h * V_DIM:(2 * h + 2) * V_DIM], preferred_element_type=F32)
            heads.append(pv[:, :V_DIM] / pv[:, V_DIM:])
        o = heads[0] - lam * heads[1]
        o = o * lax.rsqrt(jnp.mean(o * o, axis=-1, keepdims=True) + EPS)
        o = o * sg_ref[...] * (1.0 - lam_init)
        o_ref[:, sl] = o.astype(BF16)


def _attention(q, k, v, lam_vecs, subln_g, layer, seq_len, ctx):
    n = q.shape[0]
    T = seq_len
    tq = min(TOKEN_TILE, T)
    nq = T // tq
    lam_init = 0.8 - 0.6 * math.exp(-0.3 * layer)
    has_ctx = ctx is not None
    kv = pl.BlockSpec((T, D_QK), lambda b, i: (b, 0))
    in_specs = [
        pl.BlockSpec((None, 4, HEAD_DIM), lambda b, i: (layer, 0, 0)),
        pl.BlockSpec((None, 1, V_DIM), lambda b, i: (layer, 0, 0)),
        pl.BlockSpec((tq, D_QK), lambda b, i: (b * nq + i, 0)),
        kv, kv,
    ]
    args = [lam_vecs, subln_g, q, k, v]
    past = ctx[0].shape[2] if has_ctx else 0
    scratch = [pltpu.VMEM((T + past, 2 * D_ATTN), BF16)]
    if has_ctx:
        in_specs += [pl.BlockSpec((None, None, past, D_QK), lambda b, i: (b, layer, 0, 0)),
                     pl.BlockSpec((None, None, past, N_HEADS, V_DIM), lambda b, i: (b, layer, 0, 0, 0))]
        args += list(ctx)
        scratch.append(pltpu.VMEM((T + past, D_QK), BF16))
    return pl.pallas_call(
        functools.partial(_attn_kernel, lam_init=lam_init, has_ctx=has_ctx),
        grid=(n // T, nq),
        in_specs=in_specs,
        out_specs=pl.BlockSpec((tq, D_ATTN), lambda b, i: (b * nq + i, 0)),
        out_shape=jax.ShapeDtypeStruct((n, D_ATTN), BF16),
        scratch_shapes=scratch,
        compiler_params=_params("parallel", "arbitrary"),
        name="attention",
    )(*args)


def _outproj_kernel(x_ref, mc_ref, ma_ref, mod_ref, w_ref, g_ref, wr_ref, xn_ref, h2_ref, aff_ref, wb_ref,
                    *, seq_len):
    @pl.when(pl.program_id(0) == 0)
    def _():
        wb_ref[...] = w_ref[...].astype(BF16)

    for t in range(x_ref.shape[0] // SUB_ROWS):
        rows = slice(t * SUB_ROWS, (t + 1) * SUB_ROWS)
        req, off = divmod(t * SUB_ROWS, seq_len)
        acc = jnp.dot(mc_ref[rows, :], wb_ref[0:D_CONV, :], preferred_element_type=F32)
        acc = acc + jnp.dot(ma_ref[rows, :], wb_ref[D_CONV:D_MIX, :], preferred_element_type=F32)
        x = x_ref[rows, :] + mod_ref[2:3, :] * acc
        xn_ref[rows, :] = x
        ms = jnp.mean(x * x, axis=-1, keepdims=True)
        y = x * lax.rsqrt(ms + EPS) * g_ref[...]
        h2 = (y * (1.0 + mod_ref[4:5, :]) + mod_ref[3:4, :]).astype(BF16)
        h2_ref[rows, :] = h2
        logits = jnp.dot(h2, wr_ref[...], preferred_element_type=F32)
        lane = lax.broadcasted_iota(jnp.int32, logits.shape, 1)
        logits = jnp.where(lane < N_EXPERTS, logits, -jnp.inf)
        e = jnp.exp(logits - jnp.max(logits, axis=-1, keepdims=True))
        aff = e / jnp.sum(e, axis=-1, keepdims=True)
        aff_ref[req * N_EXPERTS:(req + 1) * N_EXPERTS, off:off + SUB_ROWS] = aff.T[0:N_EXPERTS, :]


def _outproj(x, mix_conv, mix_attn, mods, layer, w_out, norm_g, w_router_b, mod_row, seq_len):
    n = x.shape[0]
    tm = ROW_TILE
    assert tm % seq_len == 0 and seq_len % SUB_ROWS == 0
    reqs = tm // seq_len
    row = lambda i: (i, 0)
    return pl.pallas_call(
        functools.partial(_outproj_kernel, seq_len=seq_len),
        grid=(n // tm,),
        in_specs=[
            pl.BlockSpec((tm, D_MODEL), row),
            pl.BlockSpec((tm, D_CONV), row),
            pl.BlockSpec((tm, D_ATTN), row),
            pl.BlockSpec((None, None, 6, D_MODEL), lambda i: (layer, mod_row(i), 0, 0)),
            pl.BlockSpec((None, D_MIX, D_MODEL), lambda i: (layer, 0, 0), pipeline_mode=pl.Buffered(1)),
            pl.BlockSpec((None, 1, D_MODEL), lambda i: (layer, 0, 0)),
            pl.BlockSpec((None, D_MODEL, LANES), lambda i: (layer, 0, 0)),
        ],
        out_specs=[pl.BlockSpec((tm, D_MODEL), row), pl.BlockSpec((tm, D_MODEL), row),
                   pl.BlockSpec((reqs * N_EXPERTS, seq_len), row)],
        out_shape=[jax.ShapeDtypeStruct((n, D_MODEL), F32), jax.ShapeDtypeStruct((n, D_MODEL), BF16),
                   jax.ShapeDtypeStruct((n // seq_len * N_EXPERTS, seq_len), F32)],
        scratch_shapes=[pltpu.VMEM((D_MIX, D_MODEL), BF16)],
        compiler_params=_params("arbitrary"),
        name="outproj",
    )(x, mix_conv, mix_attn, mods, w_out, norm_g, w_router_b)


def _top_slots(a, tri, cap):
    rows = a.shape[0]
    capf = float(cap)

    def count_ge(v):
        return jnp.sum(jnp.where(a >= v, 1.0, 0.0), axis=-1, keepdims=True)

    hi = jnp.full((rows, 1), 2.0, F32)
    for j in range(6, -1, -1):
        cand = hi * (2.0 ** -(2 ** j))
        hi = jnp.where(count_ge(cand) < capf, cand, hi)
    lo = hi * 0.5
    step = lo * 0.5
    for _ in range(MANTISSA_BITS):
        cand = lo + step
        lo = jnp.where(count_ge(cand) >= capf, cand, lo)
        step = step * 0.5
    upper = jnp.where(step > 0.0, lo + step * 2.0, hi)
    above = jnp.where(a >= upper, 1.0, 0.0)
    tie = jnp.where(a >= lo, 1.0, 0.0) - above
    need = capf - jnp.sum(above, axis=-1, keepdims=True)
    tie_rank = jnp.dot(tie.astype(BF16), tri, preferred_element_type=F32)
    sel = above + tie * jnp.where(tie_rank <= need, 1.0, 0.0)
    slot = jnp.dot(sel.astype(BF16), tri, preferred_element_type=F32) - 1.0
    return jnp.where(sel > 0.5, slot, -1.0)


def _select_kernel(ap_ref, tp_ref, as_ref, ts_ref, pp_ref, pcp_ref, ps_ref, pcs_ref, *, caps):
    pos_p = _top_slots(ap_ref[...], tp_ref[...], caps[0])
    pos_s = _top_slots(as_ref[...], ts_ref[...], caps[1])
    pp_ref[...] = pos_p
    pcp_ref[...] = pos_p.T
    ps_ref[...] = pos_s
    pcs_ref[...] = pos_s.T


def _select(aff_p, tri_p, cap_p, aff_s, tri_s, cap_s):
    whole = lambda a: pl.BlockSpec(a.shape, lambda i: (0, 0))
    outs = [jax.ShapeDtypeStruct(aff_p.shape, F32), jax.ShapeDtypeStruct(aff_p.shape[::-1], F32),
            jax.ShapeDtypeStruct(aff_s.shape, F32), jax.ShapeDtypeStruct(aff_s.shape[::-1], F32)]
    return pl.pallas_call(
        functools.partial(_select_kernel, caps=(cap_p, cap_s)),
        grid=(1,),
        in_specs=[whole(aff_p), whole(tri_p), whole(aff_s), whole(tri_s)],
        out_specs=[whole(o) for o in outs],
        out_shape=outs,
        compiler_params=_params("arbitrary"),
        name="select",
    )(aff_p, tri_p, aff_s, tri_s)


def _gather_kernel(pos_ref, a_ref, h2_ref, xs_ref, gs_ref, oh_ref, *, cap):
    reqs, _, T = oh_ref.shape
    slot = lax.broadcasted_iota(jnp.int32, (cap, T), 0).astype(F32)
    for r in range(reqs):
        for e in range(N_EXPERTS):
            row = r * N_EXPERTS + e
            onehot = jnp.where(pos_ref[row:row + 1, :] == slot, 1.0, 0.0)
            oh_ref[r, e * cap:(e + 1) * cap, :] = onehot.astype(BF16)
            gs_ref[e, r * cap:(r + 1) * cap, :] = jnp.sum(onehot * a_ref[row:row + 1, :], axis=-1,
                                                          keepdims=True)
        xs = jnp.dot(oh_ref[r], h2_ref[r * T:(r + 1) * T, :], preferred_element_type=F32)
        xs_ref[:, r * cap:(r + 1) * cap, :] = xs.reshape(N_EXPERTS, cap, D_MODEL).astype(BF16)


def _gather(pos, aff_t, h2, seq_len):
    T = seq_len
    assert ROW_TILE % T == 0
    reqs = ROW_TILE // T
    nreq = h2.shape[0] // T
    cap = EC_FACTOR * T // N_EXPERTS
    rows = pl.BlockSpec((reqs * N_EXPERTS, T), lambda b: (b, 0))
    return pl.pallas_call(
        functools.partial(_gather_kernel, cap=cap),
        grid=(nreq // reqs,),
        in_specs=[rows, rows, pl.BlockSpec((reqs * T, D_MODEL), lambda b: (b, 0))],
        out_specs=[pl.BlockSpec((N_EXPERTS, reqs * cap, D_MODEL), lambda b: (0, b, 0)),
                   pl.BlockSpec((N_EXPERTS, reqs * cap, 1), lambda b: (0, b, 0))],
        out_shape=[jax.ShapeDtypeStruct((N_EXPERTS, nreq * cap, D_MODEL), BF16),
                   jax.ShapeDtypeStruct((N_EXPERTS, nreq * cap, 1), F32)],
        scratch_shapes=[pltpu.VMEM((reqs, N_EXPERTS * cap, T), BF16)],
        compiler_params=_params("parallel"),
        name="gather",
    )(pos, aff_t, h2)


def _expert_kernel(xp_ref, xs_ref, gp_ref, gsm_ref, wg_ref, wu_ref, wd_ref, yp_ref, ys_ref,
                   wgb_ref, wub_ref, wdb_ref):
    wgb_ref[...] = wg_ref[...].astype(BF16)
    wub_ref[...] = wu_ref[...].astype(BF16)
    wdb_ref[...] = wd_ref[...].astype(BF16)
    for x_ref, g_ref, y_ref in ((xp_ref, gp_ref, yp_ref), (xs_ref, gsm_ref, ys_ref)):
        for t in range(x_ref.shape[0] // EXPERT_ROW_TILE):
            rows = slice(t * EXPERT_ROW_TILE, (t + 1) * EXPERT_ROW_TILE)
            x = x_ref[rows, :]
            hg = jnp.dot(x, wgb_ref[...], preferred_element_type=F32)
            hu = jnp.dot(x, wub_ref[...], preferred_element_type=F32)
            hid = (hg * _sigmoid(hg) * hu).astype(BF16)
            y = jnp.dot(hid, wdb_ref[...], preferred_element_type=F32)
            y_ref[rows, :] = (y * g_ref[rows, :]).astype(BF16)


def _experts(xs_p, xs_s, gs_p, gs_s, layer, w_gate, w_up, w_down):
    np_, ns_ = xs_p.shape[1], xs_s.shape[1]
    slots = lambda n, w: pl.BlockSpec((None, n, w), lambda e: (e, 0, 0))
    weight = lambda rows, cols: pl.BlockSpec((None, None, rows, cols), lambda e: (layer, e, 0, 0))
    return pl.pallas_call(
        _expert_kernel,
        grid=(N_EXPERTS,),
        in_specs=[
            slots(np_, D_MODEL), slots(ns_, D_MODEL), slots(np_, 1), slots(ns_, 1),
            weight(D_MODEL, D_EXPERT), weight(D_MODEL, D_EXPERT), weight(D_EXPERT, D_MODEL),
        ],
        out_specs=[slots(np_, D_MODEL), slots(ns_, D_MODEL)],
        out_shape=[jax.ShapeDtypeStruct(xs_p.shape, BF16), jax.ShapeDtypeStruct(xs_s.shape, BF16)],
        scratch_shapes=[pltpu.VMEM((D_MODEL, D_EXPERT), BF16), pltpu.VMEM((D_MODEL, D_EXPERT), BF16),
                        pltpu.VMEM((D_EXPERT, D_MODEL), BF16)],
        compiler_params=pltpu.CompilerParams(dimension_semantics=("parallel",),
                                             vmem_limit_bytes=EXPERT_VMEM_LIMIT),
        name="experts",
    )(xs_p, xs_s, gs_p, gs_s, w_gate, w_up, w_down)


def _combine_kernel(*refs, cap, final):
    xn_ref, pc_ref, ye_ref, mod_ref, ex_ref = refs[:5]
    if final:
        fg_ref = refs[5]
    o_ref = refs[-1]
    seq_len = pc_ref.shape[0]
    reqs = xn_ref.shape[0] // seq_len
    first_req = pl.program_id(0) * reqs
    for t in range(xn_ref.shape[0] // SUB_ROWS):
        rows = slice(t * SUB_ROWS, (t + 1) * SUB_ROWS)
        req, off = divmod(t * SUB_ROWS, seq_len)
        ye = ye_ref[:, req * cap:(req + 1) * cap, :].reshape(N_EXPERTS * cap, D_MODEL)
        expander = ex_ref[(first_req + req) % ex_ref.shape[0]]
        pos = jnp.dot(pc_ref[off:off + SUB_ROWS, :].astype(BF16), expander, preferred_element_type=F32)
        lane = lax.broadcasted_iota(jnp.int32, pos.shape, 1)
        slot = (lane % cap).astype(F32)
        onehot = jnp.where(pos == slot, 1.0, 0.0).astype(BF16)
        moe = jnp.dot(onehot, ye, preferred_element_type=F32)
        x = xn_ref[rows, :] + mod_ref[5:6, :] * moe
        if final:
            x = x * lax.rsqrt(jnp.mean(x * x, axis=-1, keepdims=True) + EPS) * fg_ref[...]
        o_ref[rows, :] = x


def _combine(xn, pos_col, ye, mods, layer, mod_row, seq_len, final_g):
    n = xn.shape[0]
    T = seq_len
    tm = ROW_TILE
    reqs = tm // T
    cap = EC_FACTOR * T // N_EXPERTS
    final = final_g is not None
    group = LANES // N_EXPERTS
    assert tm % T == 0 and T % SUB_ROWS == 0 and group % reqs == 0
    src = jnp.arange(group)[:, None, None] * N_EXPERTS + (jnp.arange(N_EXPERTS * cap) // cap)[None, None, :]
    expander = (jnp.arange(LANES)[None, :, None] == src).astype(BF16)
    in_specs = [
        pl.BlockSpec((tm, D_MODEL), lambda i: (i, 0)),
        pl.BlockSpec((T, LANES), lambda i: (0, i * reqs // group)),
        pl.BlockSpec((N_EXPERTS, reqs * cap, D_MODEL), lambda i: (0, i, 0)),
        pl.BlockSpec((None, None, 6, D_MODEL), lambda i: (layer, mod_row(i), 0, 0)),
        pl.BlockSpec((group, LANES, N_EXPERTS * cap), lambda i: (0, 0, 0)),
    ]
    args = [xn, pos_col, ye, mods, expander]
    if final:
        in_specs.append(pl.BlockSpec((1, D_MODEL), lambda i: (0, 0)))
        args.append(final_g)
    return pl.pallas_call(
        functools.partial(_combine_kernel, cap=cap, final=final),
        grid=(n // tm,),
        in_specs=in_specs,
        out_specs=pl.BlockSpec((tm, D_MODEL), lambda i: (i, 0)),
        out_shape=jax.ShapeDtypeStruct((n, D_MODEL), F32),
        compiler_params=_params("parallel"),
        name="combine",
    )(*args)


def _rope_tables(T):
    rows = T // GRID_W
    row = jnp.repeat(jnp.arange(rows, dtype=F32), GRID_W)
    col = jnp.tile(jnp.arange(GRID_W, dtype=F32), rows)
    inv = ROPE_BASE ** (-jnp.arange(0, ROPE_AXIS, 2, dtype=F32) / ROPE_AXIS)
    ar = row[:, None] * inv
    ac = col[:, None] * inv
    cos = jnp.concatenate([jnp.cos(ar), jnp.cos(ar), jnp.cos(ac), jnp.cos(ac)], axis=-1)
    sin = jnp.concatenate([-jnp.sin(ar), jnp.sin(ar), -jnp.sin(ac), jnp.sin(ac)], axis=-1)
    reps = D_QK // HEAD_DIM
    return jnp.tile(cos, (1, reps)), jnp.tile(sin, (1, reps))


def kernel(x_prompt, x_sample, c, cache_k, cache_v, c_ctx, norm1_g, w_mod, b_mod, w_in, conv_w, conv_b,
           conv_ln_g, conv_ln_b, lambda_q1, lambda_k1, lambda_q2, lambda_k2, subln_g, w_out, norm2_g,
           w_router, w_gate, w_up, w_down, final_norm_g):
    batch, seq, _ = x_prompt.shape
    dec_batch, dec_seq, _ = x_sample.shape
    past = cache_k.shape[2]
    ctx_k = cache_k.reshape(dec_batch, DEPTH, past, D_QK)
    assert 1 + dec_batch <= MOD_ROWS

    cvec = jnp.concatenate([c_ctx[None, :], c, jnp.zeros((MOD_ROWS - 1 - dec_batch, D_MODEL), F32)], axis=0)
    mods = _modulation(cvec, w_mod, b_mod)

    w_router_b = jnp.pad(w_router, ((0, 0), (0, 0), (0, LANES - N_EXPERTS))).astype(BF16)
    lam_vecs = jnp.stack([lambda_q1, lambda_k1, lambda_q2, lambda_k2], axis=1)
    vec = lambda a: a.reshape(DEPTH, 1, a.shape[-1])
    norm1_v, norm2_v, subln_v = vec(norm1_g), vec(norm2_g), vec(subln_g)
    conv_b_v, ln_g_v, ln_b_v = vec(conv_b), vec(conv_ln_g), vec(conv_ln_b)
    rope_tabs = _rope_tables(dec_seq)
    final_g = final_norm_g.reshape(1, D_MODEL)
    tri_p = (jnp.arange(seq)[:, None] <= jnp.arange(seq)[None, :]).astype(BF16)
    tri_s = (jnp.arange(dec_seq)[:, None] <= jnp.arange(dec_seq)[None, :]).astype(BF16)

    prompt_row = lambda i: 0
    assert dec_seq % ROW_TILE == 0
    tiles_per_sample = dec_seq // ROW_TILE
    sample_row = lambda i: 1 + i // tiles_per_sample

    xp = x_prompt.reshape(batch * seq, D_MODEL)
    xs = x_sample.reshape(dec_batch * dec_seq, D_MODEL)
    caches = None
    for l in range(DEPTH):
        last = l == DEPTH - 1
        up, qp, kp, vp, *caches = _inproj(xp, mods, l, norm1_v, w_in, prompt_row, None, seq, True, caches)
        us, qs, ks, vs = _inproj(xs, mods, l, norm1_v, w_in, sample_row, rope_tabs, dec_seq, False)
        cp = _conv(up, l, conv_w, conv_b_v, ln_g_v, ln_b_v, seq)
        cs = _conv(us, l, conv_w, conv_b_v, ln_g_v, ln_b_v, dec_seq)
        op = _attention(qp, kp, vp, lam_vecs, subln_v, l, seq, None)
        os_ = _attention(qs, ks, vs, lam_vecs, subln_v, l, dec_seq, (ctx_k, cache_v))
        xnp, h2p, affp = _outproj(xp, cp, op, mods, l, w_out, norm2_v, w_router_b, prompt_row, seq)
        xns, h2s, affs = _outproj(xs, cs, os_, mods, l, w_out, norm2_v, w_router_b, sample_row, dec_seq)
        pos_p, rank_p, pos_s, rank_s = _select(affp, tri_p, EC_FACTOR * seq // N_EXPERTS,
                                               affs, tri_s, EC_FACTOR * dec_seq // N_EXPERTS)
        slots_p, gates_p = _gather(pos_p, affp, h2p, seq)
        slots_s, gates_s = _gather(pos_s, affs, h2s, dec_seq)
        ye_p, ye_s = _experts(slots_p, slots_s, gates_p, gates_s, l, w_gate, w_up, w_down)
        xp = _combine(xnp, rank_p, ye_p, mods, l, prompt_row, seq, final_g if last else None)
        xs = _combine(xns, rank_s, ye_s, mods, l, sample_row, dec_seq, final_g if last else None)

    y_prompt = xp.reshape(batch, seq, D_MODEL)
    y_sample = xs.reshape(dec_batch, dec_seq, D_MODEL)
    keys_t, new_v = caches
    new_k = keys_t.reshape(batch, DEPTH, N_HEADS, 2, HEAD_DIM, seq).transpose(0, 1, 5, 2, 3, 4)
    return (y_prompt, y_sample, new_k, new_v)
```

```python
import functools
import math

import jax
import jax.numpy as jnp
from jax import lax
from jax.experimental import pallas as pl
from jax.experimental.pallas import tpu as pltpu

D_MODEL = 1024
DEPTH = 4
GRID_W = 64
D_CONV = 512
CONV_WIDTH = 31
N_HEADS = 4
HEAD_DIM = 64
V_DIM = 2 * HEAD_DIM
D_QK = N_HEADS * 2 * HEAD_DIM
D_ATTN = N_HEADS * V_DIM
D_MIX = D_CONV + D_ATTN
D_IN = 2 * D_CONV + 2 * D_QK + D_ATTN
N_EXPERTS = 16
EC_FACTOR = 2
D_EXPERT = 1024
ROPE_BASE = 10000.0
ROPE_AXIS = HEAD_DIM // 2
EPS = 1e-6
LOG2_E = math.log2(math.e)

LANES = 128
MOD_ROWS = 16
TOKEN_TILE = 1024
ROW_TILE = 1024
SUB_ROWS = 256
CONV_CHUNK = 128
NORM_CHUNK = 256
CONV_PAD = 16
EXPERT_ROW_TILE = 512
MANTISSA_BITS = 23
VMEM_LIMIT = 56 * 1024 * 1024
EXPERT_VMEM_LIMIT = 60 * 1024 * 1024

F32 = jnp.float32
BF16 = jnp.bfloat16


def _sigmoid(x):
    return 1.0 / (1.0 + jnp.exp(-x))


def _params(*sem):
    return pltpu.CompilerParams(dimension_semantics=sem, vmem_limit_bytes=VMEM_LIMIT)


def _mod_kernel(c_ref, w_ref, b_ref, o_ref):
    c = c_ref[...]
    s = c * _sigmoid(c)
    o_ref[...] = jnp.dot(s.astype(BF16), w_ref[...].astype(BF16),
                         preferred_element_type=F32) + b_ref[...]


def _modulation(cvec, w_mod, b_mod):
    tn = 1536
    out = pl.pallas_call(
        _mod_kernel,
        grid=(DEPTH, 6 * D_MODEL // tn),
        in_specs=[
            pl.BlockSpec((MOD_ROWS, D_MODEL), lambda l, j: (0, 0)),
            pl.BlockSpec((None, D_MODEL, tn), lambda l, j: (l, 0, j)),
            pl.BlockSpec((None, 1, tn), lambda l, j: (l, 0, j)),
        ],
        out_specs=pl.BlockSpec((None, MOD_ROWS, tn), lambda l, j: (l, 0, j)),
        out_shape=jax.ShapeDtypeStruct((DEPTH, MOD_ROWS, 6 * D_MODEL), F32),
        compiler_params=_params("parallel", "parallel"),
        name="modulation",
    )(cvec, w_mod, b_mod.reshape(DEPTH, 1, 6 * D_MODEL))
    return out.reshape(DEPTH, MOD_ROWS, 6, D_MODEL)


def _inproj_kernel(*refs, rope, cache_out, aliased, seq_len):
    x_ref, mod_ref, g_ref, w_ref = refs[:4]
    refs = refs[4:]
    if rope:
        cos_ref, sin_ref = refs[:2]
        refs = refs[2:]
    if aliased:
        refs = refs[2:]
    u_ref, q_ref, k_ref, v_ref = refs[:4]
    wb_ref = refs[-1]

    @pl.when(pl.program_id(0) == 0)
    def _():
        wb_ref[...] = w_ref[...].astype(BF16)

    for t in range(x_ref.shape[0] // SUB_ROWS):
        rows = slice(t * SUB_ROWS, (t + 1) * SUB_ROWS)
        x = x_ref[rows, :]
        ms = jnp.mean(x * x, axis=-1, keepdims=True)
        y = x * lax.rsqrt(ms + EPS) * g_ref[...]
        h = y * (1.0 + mod_ref[1:2, :]) + mod_ref[0:1, :]
        z = jnp.dot(h.astype(BF16), wb_ref[...], preferred_element_type=F32)
        u_ref[rows, :] = z[:, :D_CONV] * _sigmoid(z[:, D_CONV:2 * D_CONV])
        q = z[:, 2 * D_CONV:2 * D_CONV + D_QK]
        k = z[:, 2 * D_CONV + D_QK:2 * D_CONV + 2 * D_QK]
        v = z[:, 2 * D_CONV + 2 * D_QK:]
        if cache_out:
            kf_ref, vf_ref = refs[4:6]
            req, off = divmod(t * SUB_ROWS, seq_len)
            kf_ref[req, :, off:off + SUB_ROWS] = k.T
            for hd in range(N_HEADS):
                vf_ref[req, off:off + SUB_ROWS, hd, :] = v[:, hd * V_DIM:(hd + 1) * V_DIM]
        if rope:
            lane = lax.broadcasted_iota(jnp.int32, q.shape, 1)
            first = (lane % ROPE_AXIS) < (ROPE_AXIS // 2)
            cos = cos_ref[rows, :]
            sin = sin_ref[rows, :]

            def rot(a):
                partner = jnp.where(first, pltpu.roll(a, D_QK - ROPE_AXIS // 2, 1),
                                    pltpu.roll(a, ROPE_AXIS // 2, 1))
                return a * cos + partner * sin

            q = rot(q)
            k = rot(k)
        q_ref[rows, :] = (q * (HEAD_DIM ** -0.5 * LOG2_E)).astype(BF16)
        k_ref[rows, :] = k.astype(BF16)
        v_ref[rows, :] = v.astype(BF16)


def _inproj(x, mods, layer, norm_g, w_in, mod_row, rope_tabs, seq_len, cache_out, caches=None):
    n = x.shape[0]
    tm = ROW_TILE
    assert rope_tabs is None or seq_len % tm == 0
    tiles_per_seq = max(1, seq_len // tm)
    rope = rope_tabs is not None
    aliased = caches is not None
    in_specs = [
        pl.BlockSpec((tm, D_MODEL), lambda i: (i, 0)),
        pl.BlockSpec((None, None, 6, D_MODEL), lambda i: (layer, mod_row(i), 0, 0)),
        pl.BlockSpec((None, 1, D_MODEL), lambda i: (layer, 0, 0)),
        pl.BlockSpec((None, D_MODEL, D_IN), lambda i: (layer, 0, 0), pipeline_mode=pl.Buffered(1)),
    ]
    args = [x, mods, norm_g, w_in]
    if rope:
        in_specs += [pl.BlockSpec((tm, D_QK), lambda i: (i % tiles_per_seq, 0))] * 2
        args += list(rope_tabs)
    half = pl.BlockSpec((tm, D_CONV), lambda i: (i, 0))
    out_specs = [half, half, half, half]
    out_shape = [jax.ShapeDtypeStruct((n, D_CONV), F32)] + [jax.ShapeDtypeStruct((n, D_QK), BF16)] * 3
    aliases = {}
    if cache_out:
        assert tm % seq_len == 0
        reqs = tm // seq_len
        out_specs += [pl.BlockSpec((reqs, None, D_QK, seq_len), lambda i: (i, layer, 0, 0)),
                      pl.BlockSpec((reqs, None, seq_len, N_HEADS, V_DIM), lambda i: (i, layer, 0, 0, 0))]
        out_shape += [jax.ShapeDtypeStruct((n // seq_len, DEPTH, D_QK, seq_len), F32),
                      jax.ShapeDtypeStruct((n // seq_len, DEPTH, seq_len, N_HEADS, V_DIM), F32)]
        if aliased:
            aliases = {len(args): len(out_shape) - 2, len(args) + 1: len(out_shape) - 1}
            in_specs += [pl.BlockSpec(memory_space=pl.ANY)] * 2
            args += list(caches)
    return pl.pallas_call(
        functools.partial(_inproj_kernel, rope=rope, cache_out=cache_out, aliased=aliased, seq_len=seq_len),
        grid=(n // tm,),
        in_specs=in_specs,
        out_specs=out_specs,
        out_shape=out_shape,
        input_output_aliases=aliases,
        scratch_shapes=[pltpu.VMEM((D_MODEL, D_IN), BF16)],
        compiler_params=_params("arbitrary"),
        name="inproj",
    )(*args)


def _conv_stage(u_ref, w_ref, pad_ref, sh_ref, wb_ref):
    T = u_ref.shape[0]
    tail = CONV_PAD + T
    for k in range(CONV_WIDTH):
        wb_ref[k] = jnp.broadcast_to(w_ref[k:k + 1, :], (8, D_CONV))
    pad_ref[0:CONV_PAD, :] = jnp.zeros((CONV_PAD, D_CONV), F32)
    pad_ref[tail:tail + CONV_PAD, :] = jnp.zeros((CONV_PAD, D_CONV), F32)
    pad_ref[CONV_PAD:tail, :] = u_ref[...]
    rows = sh_ref.shape[1]
    for r in range(8):
        sh_ref[r] = pad_ref[r:r + rows, :]


def _conv_block(sh_ref, w, base, lanes):
    first = CONV_PAD - CONV_WIDTH // 2
    groups = CONV_CHUNK // 8
    acc = jnp.zeros((groups, 8, LANES), F32)
    for r in range(8):
        taps = [(k, (first + k) // 8) for k in range(CONV_WIDTH) if (first + k) % 8 == r]
        m0 = min(m for _, m in taps)
        span = max(m for _, m in taps) - m0 + groups
        x = sh_ref[r, pl.ds(base + 8 * m0, 8 * span), lanes].reshape(span, 8, LANES)
        for k, m in taps:
            acc = acc + w[k] * x[m - m0:m - m0 + groups]
    return acc.reshape(CONV_CHUNK, LANES)


def _conv_norm(acc, b_ref, lg_ref, lb_ref):
    acc = acc + b_ref[...]
    mu = jnp.mean(acc, axis=-1, keepdims=True)
    d = acc - mu
    var = jnp.mean(d * d, axis=-1, keepdims=True)
    un = d * lax.rsqrt(var + EPS) * lg_ref[...] + lb_ref[...]
    return (un * _sigmoid(un)).astype(BF16)


def _conv_kernel(u_ref, w_ref, b_ref, lg_ref, lb_ref, o_ref, pad_ref, sh_ref, wb_ref, acc_ref):
    T = u_ref.shape[0]
    _conv_stage(u_ref, w_ref, pad_ref, sh_ref, wb_ref)

    for q in range(D_CONV // LANES):
        lanes = slice(q * LANES, (q + 1) * LANES)
        w = [wb_ref[k, :, lanes] for k in range(CONV_WIDTH)]

        def taps_chunk(i, carry, lanes=lanes, w=w):
            base = pl.multiple_of(i * CONV_CHUNK, CONV_CHUNK)
            acc_ref[pl.ds(base, CONV_CHUNK), lanes] = _conv_block(sh_ref, w, base, lanes)
            return carry

        lax.fori_loop(0, T // CONV_CHUNK, taps_chunk, 0)

    def norm_chunk(i, carry):
        base = pl.multiple_of(i * NORM_CHUNK, NORM_CHUNK)
        o_ref[pl.ds(base, NORM_CHUNK), :] = _conv_norm(acc_ref[pl.ds(base, NORM_CHUNK), :],
                                                       b_ref, lg_ref, lb_ref)
        return carry

    lax.fori_loop(0, T // NORM_CHUNK, norm_chunk, 0)


def _conv(u, layer, conv_w, conv_b, ln_g, ln_b, seq_len):
    n = u.shape[0]
    T = seq_len
    rows = T + 2 * CONV_PAD - 8
    vec = pl.BlockSpec((None, 1, D_CONV), lambda b: (layer, 0, 0))
    return pl.pallas_call(
        _conv_kernel,
        grid=(n // T,),
        in_specs=[
            pl.BlockSpec((T, D_CONV), lambda b: (b, 0)),
            pl.BlockSpec((None, CONV_WIDTH, D_CONV), lambda b: (layer, 0, 0)),
            vec, vec, vec,
        ],
        out_specs=pl.BlockSpec((T, D_CONV), lambda b: (b, 0)),
        out_shape=jax.ShapeDtypeStruct((n, D_CONV), BF16),
        scratch_shapes=[pltpu.VMEM((T + 2 * CONV_PAD, D_CONV), F32),
                        pltpu.VMEM((8, rows, D_CONV), F32),
                        pltpu.VMEM((CONV_WIDTH, 8, D_CONV), F32),
                        pltpu.VMEM((T, D_CONV), F32)],
        compiler_params=_params("parallel"),
        name="conv",
    )(u, conv_w, conv_b, ln_g, ln_b)


def _attn_kernel(*refs, lam_init, has_ctx, seq_len):
    lam_ref, sg_ref, q_ref, k_ref, v_ref = refs[:5]
    T = seq_len
    reqs = k_ref.shape[0] // T
    if has_ctx:
        ck_ref, cv_ref, o_ref, vals_ref, keys_ref = refs[5:]
    else:
        o_ref, vals_ref = refs[5:]

    @pl.when(pl.program_id(1) == 0)
    def _():
        vals_ref[...] = jnp.ones(vals_ref.shape, BF16)
        for r in range(reqs):
            for h in range(N_HEADS):
                vals_ref[r, 0:T, 2 * h * V_DIM:(2 * h + 1) * V_DIM] = v_ref[r * T:(r + 1) * T,
                                                                            h * V_DIM:(h + 1) * V_DIM]
        if has_ctx:
            keys_ref[0:T, :] = k_ref[...]
            keys_ref[T:, :] = ck_ref[...].astype(BF16)
            for h in range(N_HEADS):
                vals_ref[0, T:, 2 * h * V_DIM:(2 * h + 1) * V_DIM] = cv_ref[:, h, :].astype(BF16)

    l = lam_ref[...]
    lam = (jnp.exp(jnp.sum(l[0:1, :] * l[1:2, :], axis=-1, keepdims=True))
           - jnp.exp(jnp.sum(l[2:3, :] * l[3:4, :], axis=-1, keepdims=True)) + lam_init)
    tq = q_ref.shape[0] // reqs
    lane = lax.broadcasted_iota(jnp.int32, (tq, V_DIM), 1)
    nt = (((1,), (1,)), ((), ()))
    for r in range(reqs):
        q_rows = slice(r * tq, (r + 1) * tq)
        for h in range(N_HEADS):
            sl = slice(h * V_DIM, (h + 1) * V_DIM)
            qh = q_ref[q_rows, sl]
            kh = keys_ref[:, sl] if has_ctx else k_ref[r * T:(r + 1) * T, sl]
            heads = []
            for c in range(2):
                keep = (lane < HEAD_DIM) if c == 0 else (lane >= HEAD_DIM)
                qc = jnp.where(keep, qh, jnp.zeros_like(qh))
                s = lax.dot_general(qc, kh, nt, preferred_element_type=F32)
                e = jnp.exp2(s - jnp.max(s, axis=-1, keepdims=True)).astype(BF16)
                pv = jnp.dot(e, vals_ref[r, :, 2 * h * V_DIM:(2 * h + 2) * V_DIM],
                             preferred_element_type=F32)
                heads.append(pv[:, :V_DIM] / pv[:, V_DIM:])
            o = heads[0] - lam * heads[1]
            o = o * lax.rsqrt(jnp.mean(o * o, axis=-1, keepdims=True) + EPS)
            o = o * sg_ref[...] * (1.0 - lam_init)
            o_ref[q_rows, sl] = o.astype(BF16)


def _attention(q, k, v, lam_vecs, subln_g, layer, seq_len, ctx):
    n = q.shape[0]
    T = seq_len
    has_ctx = ctx is not None
    tq = min(TOKEN_TILE, T)
    nq = T // tq
    reqs = 1 if has_ctx else max(1, TOKEN_TILE // T)
    lam_init = 0.8 - 0.6 * math.exp(-0.3 * layer)
    kv = pl.BlockSpec((reqs * T, D_QK), lambda b, i: (b, 0))
    tile = pl.BlockSpec((reqs * tq, D_QK), lambda b, i: (b * nq + i, 0))
    in_specs = [
        pl.BlockSpec((None, 4, HEAD_DIM), lambda b, i: (layer, 0, 0)),
        pl.BlockSpec((None, 1, V_DIM), lambda b, i: (layer, 0, 0)),
        tile, kv, kv,
    ]
    args = [lam_vecs, subln_g, q, k, v]
    past = ctx[0].shape[2] if has_ctx else 0
    scratch = [pltpu.VMEM((reqs, T + past, 2 * D_ATTN), BF16)]
    if has_ctx:
        in_specs += [pl.BlockSpec((None, None, past, D_QK), lambda b, i: (b, layer, 0, 0)),
                     pl.BlockSpec((None, None, past, N_HEADS, V_DIM), lambda b, i: (b, layer, 0, 0, 0))]
        args += list(ctx)
        scratch.append(pltpu.VMEM((T + past, D_QK), BF16))
    return pl.pallas_call(
        functools.partial(_attn_kernel, lam_init=lam_init, has_ctx=has_ctx, seq_len=T),
        grid=(n // (reqs * T), nq),
        in_specs=in_specs,
        out_specs=tile,
        out_shape=jax.ShapeDtypeStruct((n, D_ATTN), BF16),
        scratch_shapes=scratch,
        compiler_params=_params("parallel", "arbitrary"),
        name="attention",
    )(*args)


def _outproj_kernel(x_ref, mc_ref, ma_ref, mod_ref, w_ref, g_ref, wr_ref, xn_ref, h2_ref, aff_ref, wb_ref,
                    *, seq_len):
    @pl.when(pl.program_id(0) == 0)
    def _():
        wb_ref[...] = w_ref[...].astype(BF16)

    for t in range(x_ref.shape[0] // SUB_ROWS):
        rows = slice(t * SUB_ROWS, (t + 1) * SUB_ROWS)
        req, off = divmod(t * SUB_ROWS, seq_len)
        mix = jnp.concatenate([mc_ref[rows, :], ma_ref[rows, :]], axis=-1)
        acc = jnp.dot(mix, wb_ref[...], preferred_element_type=F32)
        x = x_ref[rows, :] + mod_ref[2:3, :] * acc
        xn_ref[rows, :] = x
        ms = jnp.mean(x * x, axis=-1, keepdims=True)
        y = x * lax.rsqrt(ms + EPS) * g_ref[...]
        h2 = (y * (1.0 + mod_ref[4:5, :]) + mod_ref[3:4, :]).astype(BF16)
        h2_ref[rows, :] = h2
        logits = jnp.dot(h2, wr_ref[...], preferred_element_type=F32)
        lane = lax.broadcasted_iota(jnp.int32, logits.shape, 1)
        logits = jnp.where(lane < N_EXPERTS, logits, -jnp.inf)
        e = jnp.exp(logits - jnp.max(logits, axis=-1, keepdims=True))
        aff = e / jnp.sum(e, axis=-1, keepdims=True)
        aff_ref[req * N_EXPERTS:(req + 1) * N_EXPERTS, off:off + SUB_ROWS] = aff.T[0:N_EXPERTS, :]


def _outproj(x, mix_conv, mix_attn, mods, layer, w_out, norm_g, w_router_b, mod_row, seq_len):
    n = x.shape[0]
    tm = ROW_TILE
    assert tm % seq_len == 0 and seq_len % SUB_ROWS == 0
    reqs = tm // seq_len
    row = lambda i: (i, 0)
    return pl.pallas_call(
        functools.partial(_outproj_kernel, seq_len=seq_len),
        grid=(n // tm,),
        in_specs=[
            pl.BlockSpec((tm, D_MODEL), row),
            pl.BlockSpec((tm, D_CONV), row),
            pl.BlockSpec((tm, D_ATTN), row),
            pl.BlockSpec((None, None, 6, D_MODEL), lambda i: (layer, mod_row(i), 0, 0)),
            pl.BlockSpec((None, D_MIX, D_MODEL), lambda i: (layer, 0, 0), pipeline_mode=pl.Buffered(1)),
            pl.BlockSpec((None, 1, D_MODEL), lambda i: (layer, 0, 0)),
            pl.BlockSpec((None, D_MODEL, LANES), lambda i: (layer, 0, 0)),
        ],
        out_specs=[pl.BlockSpec((tm, D_MODEL), row), pl.BlockSpec((tm, D_MODEL), row),
                   pl.BlockSpec((reqs * N_EXPERTS, seq_len), row)],
        out_shape=[jax.ShapeDtypeStruct((n, D_MODEL), F32), jax.ShapeDtypeStruct((n, D_MODEL), BF16),
                   jax.ShapeDtypeStruct((n // seq_len * N_EXPERTS, seq_len), F32)],
        scratch_shapes=[pltpu.VMEM((D_MIX, D_MODEL), BF16)],
        compiler_params=_params("arbitrary"),
        name="outproj",
    )(x, mix_conv, mix_attn, mods, w_out, norm_g, w_router_b)


def _top_slots(a, tri, cap):
    rows = a.shape[0]
    capf = float(cap)

    def count_ge(v):
        return jnp.sum(jnp.where(a >= v, 1.0, 0.0), axis=-1, keepdims=True)

    hi = jnp.full((rows, 1), 2.0, F32)
    for j in range(6, -1, -1):
        cand = hi * (2.0 ** -(2 ** j))
        hi = jnp.where(count_ge(cand) < capf, cand, hi)
    lo = hi * 0.5
    step = lo * 0.5
    for _ in range(MANTISSA_BITS):
        cand = lo + step
        lo = jnp.where(count_ge(cand) >= capf, cand, lo)
        step = step * 0.5
    upper = jnp.where(step > 0.0, lo + step * 2.0, hi)
    above = jnp.where(a >= upper, 1.0, 0.0)
    tie = jnp.where(a >= lo, 1.0, 0.0) - above
    need = capf - jnp.sum(above, axis=-1, keepdims=True)
    tie_rank = jnp.dot(tie.astype(BF16), tri, preferred_element_type=F32)
    sel = above + tie * jnp.where(tie_rank <= need, 1.0, 0.0)
    slot = jnp.dot(sel.astype(BF16), tri, preferred_element_type=F32) - 1.0
    return jnp.where(sel > 0.5, slot, -1.0)


def _select_kernel(ap_ref, tp_ref, as_ref, ts_ref, pp_ref, pcp_ref, ps_ref, pcs_ref, *, caps):
    pos_p = _top_slots(ap_ref[...], tp_ref[...], caps[0])
    pos_s = _top_slots(as_ref[...], ts_ref[...], caps[1])
    pp_ref[...] = pos_p
    pcp_ref[...] = pos_p.T
    ps_ref[...] = pos_s
    pcs_ref[...] = pos_s.T


def _select(aff_p, tri_p, cap_p, aff_s, tri_s, cap_s):
    whole = lambda a: pl.BlockSpec(a.shape, lambda i: (0, 0))
    outs = [jax.ShapeDtypeStruct(aff_p.shape, F32), jax.ShapeDtypeStruct(aff_p.shape[::-1], F32),
            jax.ShapeDtypeStruct(aff_s.shape, F32), jax.ShapeDtypeStruct(aff_s.shape[::-1], F32)]
    return pl.pallas_call(
        functools.partial(_select_kernel, caps=(cap_p, cap_s)),
        grid=(1,),
        in_specs=[whole(aff_p), whole(tri_p), whole(aff_s), whole(tri_s)],
        out_specs=[whole(o) for o in outs],
        out_shape=outs,
        compiler_params=_params("arbitrary"),
        name="select",
    )(aff_p, tri_p, aff_s, tri_s)


def _gather_kernel(pos_ref, a_ref, h2_ref, xs_ref, gs_ref, oh_ref, *, cap):
    reqs, _, T = oh_ref.shape
    slot = lax.broadcasted_iota(jnp.int32, (cap, T), 0).astype(F32)
    for r in range(reqs):
        for e in range(N_EXPERTS):
            row = r * N_EXPERTS + e
            onehot = jnp.where(pos_ref[row:row + 1, :] == slot, 1.0, 0.0)
            oh_ref[r, e * cap:(e + 1) * cap, :] = onehot.astype(BF16)
            gs_ref[e, r * cap:(r + 1) * cap, :] = jnp.sum(onehot * a_ref[row:row + 1, :], axis=-1,
                                                          keepdims=True)
        xs = jnp.dot(oh_ref[r], h2_ref[r * T:(r + 1) * T, :], preferred_element_type=F32)
        xs_ref[:, r * cap:(r + 1) * cap, :] = xs.reshape(N_EXPERTS, cap, D_MODEL).astype(BF16)


def _gather(pos, aff_t, h2, seq_len):
    T = seq_len
    assert ROW_TILE % T == 0
    reqs = ROW_TILE // T
    nreq = h2.shape[0] // T
    cap = EC_FACTOR * T // N_EXPERTS
    rows = pl.BlockSpec((reqs * N_EXPERTS, T), lambda b: (b, 0))
    return pl.pallas_call(
        functools.partial(_gather_kernel, cap=cap),
        grid=(nreq // reqs,),
        in_specs=[rows, rows, pl.BlockSpec((reqs * T, D_MODEL), lambda b: (b, 0))],
        out_specs=[pl.BlockSpec((N_EXPERTS, reqs * cap, D_MODEL), lambda b: (0, b, 0)),
                   pl.BlockSpec((N_EXPERTS, reqs * cap, 1), lambda b: (0, b, 0))],
        out_shape=[jax.ShapeDtypeStruct((N_EXPERTS, nreq * cap, D_MODEL), BF16),
                   jax.ShapeDtypeStruct((N_EXPERTS, nreq * cap, 1), F32)],
        scratch_shapes=[pltpu.VMEM((reqs, N_EXPERTS * cap, T), BF16)],
        compiler_params=_params("parallel"),
        name="gather",
    )(pos, aff_t, h2)


def _expert_kernel(xp_ref, xs_ref, gp_ref, gsm_ref, wg_ref, wu_ref, wd_ref, yp_ref, ys_ref,
                   wgb_ref, wub_ref, wdb_ref):
    wgb_ref[...] = wg_ref[...].astype(BF16)
    wub_ref[...] = wu_ref[...].astype(BF16)
    wdb_ref[...] = wd_ref[...].astype(BF16)
    for x_ref, g_ref, y_ref in ((xp_ref, gp_ref, yp_ref), (xs_ref, gsm_ref, ys_ref)):
        for t in range(x_ref.shape[0] // EXPERT_ROW_TILE):
            rows = slice(t * EXPERT_ROW_TILE, (t + 1) * EXPERT_ROW_TILE)
            x = x_ref[rows, :]
            hg = jnp.dot(x, wgb_ref[...], preferred_element_type=F32)
            hu = jnp.dot(x, wub_ref[...], preferred_element_type=F32)
            hid = (hg * _sigmoid(hg) * hu).astype(BF16)
            y = jnp.dot(hid, wdb_ref[...], preferred_element_type=F32)
            y_ref[rows, :] = (y * g_ref[rows, :]).astype(BF16)


def _experts(xs_p, xs_s, gs_p, gs_s, layer, w_gate, w_up, w_down):
    np_, ns_ = xs_p.shape[1], xs_s.shape[1]
    slots = lambda n, w: pl.BlockSpec((None, n, w), lambda e: (e, 0, 0))
    weight = lambda rows, cols: pl.BlockSpec((None, None, rows, cols), lambda e: (layer, e, 0, 0))
    return pl.pallas_call(
        _expert_kernel,
        grid=(N_EXPERTS,),
        in_specs=[
            slots(np_, D_MODEL), slots(ns_, D_MODEL), slots(np_, 1), slots(ns_, 1),
            weight(D_MODEL, D_EXPERT), weight(D_MODEL, D_EXPERT), weight(D_EXPERT, D_MODEL),
        ],
        out_specs=[slots(np_, D_MODEL), slots(ns_, D_MODEL)],
        out_shape=[jax.ShapeDtypeStruct(xs_p.shape, BF16), jax.ShapeDtypeStruct(xs_s.shape, BF16)],
        scratch_shapes=[pltpu.VMEM((D_MODEL, D_EXPERT), BF16), pltpu.VMEM((D_MODEL, D_EXPERT), BF16),
                        pltpu.VMEM((D_EXPERT, D_MODEL), BF16)],
        compiler_params=pltpu.CompilerParams(dimension_semantics=("parallel",),
                                             vmem_limit_bytes=EXPERT_VMEM_LIMIT),
        name="experts",
    )(xs_p, xs_s, gs_p, gs_s, w_gate, w_up, w_down)


def _combine_kernel(*refs, cap, final):
    xn_ref, pc_ref, ye_ref, mod_ref, ex_ref = refs[:5]
    if final:
        fg_ref = refs[5]
    o_ref = refs[-1]
    seq_len = pc_ref.shape[0]
    reqs = xn_ref.shape[0] // seq_len
    first_req = pl.program_id(0) * reqs
    for t in range(xn_ref.shape[0] // SUB_ROWS):
        rows = slice(t * SUB_ROWS, (t + 1) * SUB_ROWS)
        req, off = divmod(t * SUB_ROWS, seq_len)
        ye = ye_ref[:, req * cap:(req + 1) * cap, :].reshape(N_EXPERTS * cap, D_MODEL)
        expander = ex_ref[(first_req + req) % ex_ref.shape[0]]
        pos = jnp.dot(pc_ref[off:off + SUB_ROWS, :].astype(BF16), expander, preferred_element_type=F32)
        lane = lax.broadcasted_iota(jnp.int32, pos.shape, 1)
        slot = (lane % cap).astype(F32)
        onehot = jnp.where(pos == slot, 1.0, 0.0).astype(BF16)
        moe = jnp.dot(onehot, ye, preferred_element_type=F32)
        x = xn_ref[rows, :] + mod_ref[5:6, :] * moe
        if final:
            x = x * lax.rsqrt(jnp.mean(x * x, axis=-1, keepdims=True) + EPS) * fg_ref[...]
        o_ref[rows, :] = x


def _combine(xn, pos_col, ye, mods, layer, mod_row, seq_len, final_g):
    n = xn.shape[0]
    T = seq_len
    tm = ROW_TILE
    reqs = tm // T
    cap = EC_FACTOR * T // N_EXPERTS
    final = final_g is not None
    group = LANES // N_EXPERTS
    assert tm % T == 0 and T % SUB_ROWS == 0 and group % reqs == 0
    src = jnp.arange(group)[:, None, None] * N_EXPERTS + (jnp.arange(N_EXPERTS * cap) // cap)[None, None, :]
    expander = (jnp.arange(LANES)[None, :, None] == src).astype(BF16)
    in_specs = [
        pl.BlockSpec((tm, D_MODEL), lambda i: (i, 0)),
        pl.BlockSpec((T, LANES), lambda i: (0, i * reqs // group)),
        pl.BlockSpec((N_EXPERTS, reqs * cap, D_MODEL), lambda i: (0, i, 0)),
        pl.BlockSpec((None, None, 6, D_MODEL), lambda i: (layer, mod_row(i), 0, 0)),
        pl.BlockSpec((group, LANES, N_EXPERTS * cap), lambda i: (0, 0, 0)),
    ]
    args = [xn, pos_col, ye, mods, expander]
    if final:
        in_specs.append(pl.BlockSpec((1, D_MODEL), lambda i: (0, 0)))
        args.append(final_g)
    return pl.pallas_call(
        functools.partial(_combine_kernel, cap=cap, final=final),
        grid=(n // tm,),
        in_specs=in_specs,
        out_specs=pl.BlockSpec((tm, D_MODEL), lambda i: (i, 0)),
        out_shape=jax.ShapeDtypeStruct((n, D_MODEL), F32),
        compiler_params=_params("parallel"),
        name="combine",
    )(*args)


def _rope_tables(T):
    rows = T // GRID_W
    row = jnp.repeat(jnp.arange(rows, dtype=F32), GRID_W)
    col = jnp.tile(jnp.arange(GRID_W, dtype=F32), rows)
    inv = ROPE_BASE ** (-jnp.arange(0, ROPE_AXIS, 2, dtype=F32) / ROPE_AXIS)
    ar = row[:, None] * inv
    ac = col[:, None] * inv
    cos = jnp.concatenate([jnp.cos(ar), jnp.cos(ar), jnp.cos(ac), jnp.cos(ac)], axis=-1)
    sin = jnp.concatenate([-jnp.sin(ar), jnp.sin(ar), -jnp.sin(ac), jnp.sin(ac)], axis=-1)
    reps = D_QK // HEAD_DIM
    return jnp.tile(cos, (1, reps)), jnp.tile(sin, (1, reps))


def kernel(x_prompt, x_sample, c, cache_k, cache_v, c_ctx, norm1_g, w_mod, b_mod, w_in, conv_w, conv_b,
           conv_ln_g, conv_ln_b, lambda_q1, lambda_k1, lambda_q2, lambda_k2, subln_g, w_out, norm2_g,
           w_router, w_gate, w_up, w_down, final_norm_g):
    batch, seq, _ = x_prompt.shape
    dec_batch, dec_seq, _ = x_sample.shape
    past = cache_k.shape[2]
    ctx_k = cache_k.reshape(dec_batch, DEPTH, past, D_QK)
    assert 1 + dec_batch <= MOD_ROWS

    cvec = jnp.concatenate([c_ctx[None, :], c, jnp.zeros((MOD_ROWS - 1 - dec_batch, D_MODEL), F32)], axis=0)
    mods = _modulation(cvec, w_mod, b_mod)

    w_router_b = jnp.pad(w_router, ((0, 0), (0, 0), (0, LANES - N_EXPERTS))).astype(BF16)
    lam_vecs = jnp.stack([lambda_q1, lambda_k1, lambda_q2, lambda_k2], axis=1)
    vec = lambda a: a.reshape(DEPTH, 1, a.shape[-1])
    norm1_v, norm2_v, subln_v = vec(norm1_g), vec(norm2_g), vec(subln_g)
    conv_b_v, ln_g_v, ln_b_v = vec(conv_b), vec(conv_ln_g), vec(conv_ln_b)
    rope_tabs = _rope_tables(dec_seq)
    final_g = final_norm_g.reshape(1, D_MODEL)
    tri_p = (jnp.arange(seq)[:, None] <= jnp.arange(seq)[None, :]).astype(BF16)
    tri_s = (jnp.arange(dec_seq)[:, None] <= jnp.arange(dec_seq)[None, :]).astype(BF16)

    prompt_row = lambda i: 0
    assert dec_seq % ROW_TILE == 0
    tiles_per_sample = dec_seq // ROW_TILE
    sample_row = lambda i: 1 + i // tiles_per_sample

    xp = x_prompt.reshape(batch * seq, D_MODEL)
    xs = x_sample.reshape(dec_batch * dec_seq, D_MODEL)
    caches = None
    for l in range(DEPTH):
        last = l == DEPTH - 1
        up, qp, kp, vp, *caches = _inproj(xp, mods, l, norm1_v, w_in, prompt_row, None, seq, True, caches)
        us, qs, ks, vs = _inproj(xs, mods, l, norm1_v, w_in, sample_row, rope_tabs, dec_seq, False)
        cp = _conv(up, l, conv_w, conv_b_v, ln_g_v, ln_b_v, seq)
        cs = _conv(us, l, conv_w, conv_b_v, ln_g_v, ln_b_v, dec_seq)
        op = _attention(qp, kp, vp, lam_vecs, subln_v, l, seq, None)
        os_ = _attention(qs, ks, vs, lam_vecs, subln_v, l, dec_seq, (ctx_k, cache_v))
        xnp, h2p, affp = _outproj(xp, cp, op, mods, l, w_out, norm2_v, w_router_b, prompt_row, seq)
        xns, h2s, affs = _outproj(xs, cs, os_, mods, l, w_out, norm2_v, w_router_b, sample_row, dec_seq)
        pos_p, rank_p, pos_s, rank_s = _select(affp, tri_p, EC_FACTOR * seq // N_EXPERTS,
                                               affs, tri_s, EC_FACTOR * dec_seq // N_EXPERTS)
        slots_p, gates_p = _gather(pos_p, affp, h2p, seq)
        slots_s, gates_s = _gather(pos_s, affs, h2s, dec_seq)
        ye_p, ye_s = _experts(slots_p, slots_s, gates_p, gates_s, l, w_gate, w_up, w_down)
        xp = _combine(xnp, rank_p, ye_p, mods, l, prompt_row, seq, final_g if last else None)
        xs = _combine(xns, rank_s, ye_s, mods, l, sample_row, dec_seq, final_g if last else None)

    y_prompt = xp.reshape(batch, seq, D_MODEL)
    y_sample = xs.reshape(dec_batch, dec_seq, D_MODEL)
    keys_t, new_v = caches
    new_k = keys_t.reshape(batch, DEPTH, N_HEADS, 2, HEAD_DIM, seq).transpose(0, 1, 5, 2, 3, 4)
    return (y_prompt, y_sample, new_k, new_v)
```

```python
import functools
import math

import jax
import jax.numpy as jnp
from jax import lax
from jax.experimental import pallas as pl
from jax.experimental.pallas import tpu as pltpu

D_MODEL = 1024
DEPTH = 4
GRID_W = 64
D_CONV = 512
CONV_WIDTH = 31
N_HEADS = 4
HEAD_DIM = 64
V_DIM = 2 * HEAD_DIM
D_QK = N_HEADS * 2 * HEAD_DIM
D_ATTN = N_HEADS * V_DIM
D_MIX = D_CONV + D_ATTN
D_IN = 2 * D_CONV + 2 * D_QK + D_ATTN
N_EXPERTS = 16
EC_FACTOR = 2
D_EXPERT = 1024
ROPE_BASE = 10000.0
ROPE_AXIS = HEAD_DIM // 2
EPS = 1e-6
LOG2_E = math.log2(math.e)

LANES = 128
MOD_ROWS = 16
TOKEN_TILE = 1024
ROW_TILE = 1024
SUB_ROWS = 256
CONV_CHUNK = 128
NORM_CHUNK = 256
CONV_PAD = 16
EXPERT_ROW_TILE = 512
MANTISSA_BITS = 23
VMEM_LIMIT = 56 * 1024 * 1024
EXPERT_VMEM_LIMIT = 60 * 1024 * 1024

F32 = jnp.float32
BF16 = jnp.bfloat16


def _sigmoid(x):
    return 1.0 / (1.0 + jnp.exp(-x))


def _params(*sem):
    return pltpu.CompilerParams(dimension_semantics=sem, vmem_limit_bytes=VMEM_LIMIT)


def _mod_kernel(c_ref, w_ref, b_ref, o_ref):
    c = c_ref[...]
    s = c * _sigmoid(c)
    o_ref[...] = jnp.dot(s.astype(BF16), w_ref[...].astype(BF16),
                         preferred_element_type=F32) + b_ref[...]


def _modulation(cvec, w_mod, b_mod):
    tn = 1536
    out = pl.pallas_call(
        _mod_kernel,
        grid=(DEPTH, 6 * D_MODEL // tn),
        in_specs=[
            pl.BlockSpec((MOD_ROWS, D_MODEL), lambda l, j: (0, 0)),
            pl.BlockSpec((None, D_MODEL, tn), lambda l, j: (l, 0, j)),
            pl.BlockSpec((None, 1, tn), lambda l, j: (l, 0, j)),
        ],
        out_specs=pl.BlockSpec((None, MOD_ROWS, tn), lambda l, j: (l, 0, j)),
        out_shape=jax.ShapeDtypeStruct((DEPTH, MOD_ROWS, 6 * D_MODEL), F32),
        compiler_params=_params("parallel", "parallel"),
        name="modulation",
    )(cvec, w_mod, b_mod.reshape(DEPTH, 1, 6 * D_MODEL))
    return out.reshape(DEPTH, MOD_ROWS, 6, D_MODEL)


def _inproj_kernel(*refs, rope, cache_out, seq_len):
    x_ref, mod_ref, g_ref, w_ref = refs[:4]
    refs = refs[4:]
    if rope:
        cos_ref, sin_ref = refs[:2]
        refs = refs[2:]
    if cache_out:
        refs = refs[2:]
    u_ref, q_ref, k_ref, v_ref = refs[:4]
    wb_ref = refs[-1]

    @pl.when(pl.program_id(0) == 0)
    def _():
        wb_ref[...] = w_ref[...].astype(BF16)

    for t in range(x_ref.shape[0] // SUB_ROWS):
        rows = slice(t * SUB_ROWS, (t + 1) * SUB_ROWS)
        x = x_ref[rows, :]
        ms = jnp.mean(x * x, axis=-1, keepdims=True)
        y = x * lax.rsqrt(ms + EPS) * g_ref[...]
        h = y * (1.0 + mod_ref[1:2, :]) + mod_ref[0:1, :]
        z = jnp.dot(h.astype(BF16), wb_ref[...], preferred_element_type=F32)
        u_ref[rows, :] = z[:, :D_CONV] * _sigmoid(z[:, D_CONV:2 * D_CONV])
        q = z[:, 2 * D_CONV:2 * D_CONV + D_QK]
        k = z[:, 2 * D_CONV + D_QK:2 * D_CONV + 2 * D_QK]
        v = z[:, 2 * D_CONV + 2 * D_QK:]
        if cache_out:
            kf_ref, vf_ref = refs[4:6]
            req, off = divmod(t * SUB_ROWS, seq_len)
            kf_ref[req, :, off:off + SUB_ROWS] = k.T
            for hd in range(N_HEADS):
                vf_ref[req, off:off + SUB_ROWS, hd, :] = v[:, hd * V_DIM:(hd + 1) * V_DIM]
        if rope:
            lane = lax.broadcasted_iota(jnp.int32, q.shape, 1)
            first = (lane % ROPE_AXIS) < (ROPE_AXIS // 2)
            cos = cos_ref[rows, :]
            sin = sin_ref[rows, :]

            def rot(a):
                partner = jnp.where(first, pltpu.roll(a, D_QK - ROPE_AXIS // 2, 1),
                                    pltpu.roll(a, ROPE_AXIS // 2, 1))
                return a * cos + partner * sin

            q = rot(q)
            k = rot(k)
        q_ref[rows, :] = (q * (HEAD_DIM ** -0.5 * LOG2_E)).astype(BF16)
        k_ref[rows, :] = k.astype(BF16)
        v_ref[rows, :] = v.astype(BF16)


def _inproj(x, mods, layer, norm_g, w_in, mod_row, rope_tabs, seq_len, caches=None):
    n = x.shape[0]
    tm = ROW_TILE
    assert rope_tabs is None or seq_len % tm == 0
    tiles_per_seq = max(1, seq_len // tm)
    rope = rope_tabs is not None
    cache_out = caches is not None
    in_specs = [
        pl.BlockSpec((tm, D_MODEL), lambda i: (i, 0)),
        pl.BlockSpec((None, None, 6, D_MODEL), lambda i: (layer, mod_row(i), 0, 0)),
        pl.BlockSpec((None, 1, D_MODEL), lambda i: (layer, 0, 0)),
        pl.BlockSpec((None, D_MODEL, D_IN), lambda i: (layer, 0, 0), pipeline_mode=pl.Buffered(1)),
    ]
    args = [x, mods, norm_g, w_in]
    if rope:
        in_specs += [pl.BlockSpec((tm, D_QK), lambda i: (i % tiles_per_seq, 0))] * 2
        args += list(rope_tabs)
    half = pl.BlockSpec((tm, D_CONV), lambda i: (i, 0))
    out_specs = [half, half, half, half]
    out_shape = [jax.ShapeDtypeStruct((n, D_CONV), F32)] + [jax.ShapeDtypeStruct((n, D_QK), BF16)] * 3
    aliases = {}
    if cache_out:
        assert tm % seq_len == 0
        reqs = tm // seq_len
        out_specs += [pl.BlockSpec((reqs, None, D_QK, seq_len), lambda i: (i, layer, 0, 0)),
                      pl.BlockSpec((reqs, None, seq_len, N_HEADS, V_DIM), lambda i: (i, layer, 0, 0, 0))]
        out_shape += [jax.ShapeDtypeStruct(c.shape, c.dtype) for c in caches]
        aliases = {len(args): len(out_shape) - 2, len(args) + 1: len(out_shape) - 1}
        in_specs += [pl.BlockSpec(memory_space=pl.ANY)] * 2
        args += list(caches)
    return pl.pallas_call(
        functools.partial(_inproj_kernel, rope=rope, cache_out=cache_out, seq_len=seq_len),
        grid=(n // tm,),
        in_specs=in_specs,
        out_specs=out_specs,
        out_shape=out_shape,
        input_output_aliases=aliases,
        scratch_shapes=[pltpu.VMEM((D_MODEL, D_IN), BF16)],
        compiler_params=_params("arbitrary"),
        name="inproj",
    )(*args)


def _conv_stage(u_ref, w_ref, pad_ref, sh_ref, wb_ref):
    T = u_ref.shape[0]
    tail = CONV_PAD + T
    for k in range(CONV_WIDTH):
        wb_ref[k] = jnp.broadcast_to(w_ref[k:k + 1, :], (8, D_CONV))
    pad_ref[0:CONV_PAD, :] = jnp.zeros((CONV_PAD, D_CONV), F32)
    pad_ref[tail:tail + CONV_PAD, :] = jnp.zeros((CONV_PAD, D_CONV), F32)
    pad_ref[CONV_PAD:tail, :] = u_ref[...]
    rows = sh_ref.shape[1]
    for r in range(8):
        sh_ref[r] = pad_ref[r:r + rows, :]


def _conv_block(sh_ref, w, base, lanes):
    first = CONV_PAD - CONV_WIDTH // 2
    groups = CONV_CHUNK // 8
    acc = jnp.zeros((groups, 8, LANES), F32)
    for r in range(8):
        taps = [(k, (first + k) // 8) for k in range(CONV_WIDTH) if (first + k) % 8 == r]
        m0 = min(m for _, m in taps)
        span = max(m for _, m in taps) - m0 + groups
        x = sh_ref[r, pl.ds(base + 8 * m0, 8 * span), lanes].reshape(span, 8, LANES)
        for k, m in taps:
            acc = acc + w[k] * x[m - m0:m - m0 + groups]
    return acc.reshape(CONV_CHUNK, LANES)


def _conv_norm(acc, b_ref, lg_ref, lb_ref):
    acc = acc + b_ref[...]
    mu = jnp.mean(acc, axis=-1, keepdims=True)
    d = acc - mu
    var = jnp.mean(d * d, axis=-1, keepdims=True)
    un = d * lax.rsqrt(var + EPS) * lg_ref[...] + lb_ref[...]
    return (un * _sigmoid(un)).astype(BF16)


def _conv_kernel(u_ref, w_ref, b_ref, lg_ref, lb_ref, o_ref, pad_ref, sh_ref, wb_ref, acc_ref):
    T = u_ref.shape[0]
    _conv_stage(u_ref, w_ref, pad_ref, sh_ref, wb_ref)

    for q in range(D_CONV // LANES):
        lanes = slice(q * LANES, (q + 1) * LANES)
        w = [wb_ref[k, :, lanes] for k in range(CONV_WIDTH)]

        def taps_chunk(i, carry, lanes=lanes, w=w):
            base = pl.multiple_of(i * CONV_CHUNK, CONV_CHUNK)
            acc_ref[pl.ds(base, CONV_CHUNK), lanes] = _conv_block(sh_ref, w, base, lanes)
            return carry

        lax.fori_loop(0, T // CONV_CHUNK, taps_chunk, 0)

    def norm_chunk(i, carry):
        base = pl.multiple_of(i * NORM_CHUNK, NORM_CHUNK)
        o_ref[pl.ds(base, NORM_CHUNK), :] = _conv_norm(acc_ref[pl.ds(base, NORM_CHUNK), :],
                                                       b_ref, lg_ref, lb_ref)
        return carry

    lax.fori_loop(0, T // NORM_CHUNK, norm_chunk, 0)


def _conv(u, layer, conv_w, conv_b, ln_g, ln_b, seq_len):
    n = u.shape[0]
    T = seq_len
    rows = T + 2 * CONV_PAD - 8
    vec = pl.BlockSpec((None, 1, D_CONV), lambda b: (layer, 0, 0))
    return pl.pallas_call(
        _conv_kernel,
        grid=(n // T,),
        in_specs=[
            pl.BlockSpec((T, D_CONV), lambda b: (b, 0)),
            pl.BlockSpec((None, CONV_WIDTH, D_CONV), lambda b: (layer, 0, 0)),
            vec, vec, vec,
        ],
        out_specs=pl.BlockSpec((T, D_CONV), lambda b: (b, 0)),
        out_shape=jax.ShapeDtypeStruct((n, D_CONV), BF16),
        scratch_shapes=[pltpu.VMEM((T + 2 * CONV_PAD, D_CONV), F32),
                        pltpu.VMEM((8, rows, D_CONV), F32),
                        pltpu.VMEM((CONV_WIDTH, 8, D_CONV), F32),
                        pltpu.VMEM((T, D_CONV), F32)],
        compiler_params=_params("parallel"),
        name="conv",
    )(u, conv_w, conv_b, ln_g, ln_b)


def _attn_kernel(*refs, lam_init, has_ctx, seq_len):
    lam_ref, sg_ref, q_ref, k_ref, v_ref = refs[:5]
    T = seq_len
    reqs = k_ref.shape[0] // T
    if has_ctx:
        ck_ref, cv_ref, o_ref, vals_ref, keys_ref = refs[5:]
    else:
        o_ref, vals_ref = refs[5:]

    @pl.when(pl.program_id(1) == 0)
    def _():
        vals_ref[...] = jnp.ones(vals_ref.shape, BF16)
        for r in range(reqs):
            for h in range(N_HEADS):
                vals_ref[r, 0:T, 2 * h * V_DIM:(2 * h + 1) * V_DIM] = v_ref[r * T:(r + 1) * T,
                                                                            h * V_DIM:(h + 1) * V_DIM]
        if has_ctx:
            keys_ref[0:T, :] = k_ref[...]
            keys_ref[T:, :] = ck_ref[...].astype(BF16)
            for h in range(N_HEADS):
                vals_ref[0, T:, 2 * h * V_DIM:(2 * h + 1) * V_DIM] = cv_ref[:, h, :].astype(BF16)

    l = lam_ref[...]
    lam = (jnp.exp(jnp.sum(l[0:1, :] * l[1:2, :], axis=-1, keepdims=True))
           - jnp.exp(jnp.sum(l[2:3, :] * l[3:4, :], axis=-1, keepdims=True)) + lam_init)
    tq = q_ref.shape[0] // reqs
    lane = lax.broadcasted_iota(jnp.int32, (tq, V_DIM), 1)
    nt = (((1,), (1,)), ((), ()))
    for r in range(reqs):
        q_rows = slice(r * tq, (r + 1) * tq)
        for h in range(N_HEADS):
            sl = slice(h * V_DIM, (h + 1) * V_DIM)
            qh = q_ref[q_rows, sl]
            kh = keys_ref[:, sl] if has_ctx else k_ref[r * T:(r + 1) * T, sl]
            heads = []
            for c in range(2):
                keep = (lane < HEAD_DIM) if c == 0 else (lane >= HEAD_DIM)
                qc = jnp.where(keep, qh, jnp.zeros_like(qh))
                s = lax.dot_general(qc, kh, nt, preferred_element_type=F32)
                e = jnp.exp2(s - jnp.max(s, axis=-1, keepdims=True)).astype(BF16)
                pv = jnp.dot(e, vals_ref[r, :, 2 * h * V_DIM:(2 * h + 2) * V_DIM],
                             preferred_element_type=F32)
                heads.append(pv[:, :V_DIM] / pv[:, V_DIM:])
            o = heads[0] - lam * heads[1]
            o = o * lax.rsqrt(jnp.mean(o * o, axis=-1, keepdims=True) + EPS)
            o = o * sg_ref[...] * (1.0 - lam_init)
            o_ref[q_rows, sl] = o.astype(BF16)


def _attention(q, k, v, lam_vecs, subln_g, layer, seq_len, ctx):
    n = q.shape[0]
    T = seq_len
    has_ctx = ctx is not None
    tq = min(TOKEN_TILE, T)
    nq = T // tq
    reqs = 1 if has_ctx else max(1, TOKEN_TILE // T)
    lam_init = 0.8 - 0.6 * math.exp(-0.3 * layer)
    kv = pl.BlockSpec((reqs * T, D_QK), lambda b, i: (b, 0))
    tile = pl.BlockSpec((reqs * tq, D_QK), lambda b, i: (b * nq + i, 0))
    in_specs = [
        pl.BlockSpec((None, 4, HEAD_DIM), lambda b, i: (layer, 0, 0)),
        pl.BlockSpec((None, 1, V_DIM), lambda b, i: (layer, 0, 0)),
        tile, kv, kv,
    ]
    args = [lam_vecs, subln_g, q, k, v]
    past = ctx[0].shape[2] if has_ctx else 0
    scratch = [pltpu.VMEM((reqs, T + past, 2 * D_ATTN), BF16)]
    if has_ctx:
        in_specs += [pl.BlockSpec((None, None, past, D_QK), lambda b, i: (b, layer, 0, 0)),
                     pl.BlockSpec((None, None, past, N_HEADS, V_DIM), lambda b, i: (b, layer, 0, 0, 0))]
        args += list(ctx)
        scratch.append(pltpu.VMEM((T + past, D_QK), BF16))
    return pl.pallas_call(
        functools.partial(_attn_kernel, lam_init=lam_init, has_ctx=has_ctx, seq_len=T),
        grid=(n // (reqs * T), nq),
        in_specs=in_specs,
        out_specs=tile,
        out_shape=jax.ShapeDtypeStruct((n, D_ATTN), BF16),
        scratch_shapes=scratch,
        compiler_params=_params("parallel", "arbitrary"),
        name="attention",
    )(*args)


def _outproj_kernel(x_ref, mc_ref, ma_ref, mod_ref, w_ref, g_ref, wr_ref, xn_ref, h2_ref, aff_ref, wb_ref,
                    *, seq_len):
    @pl.when(pl.program_id(0) == 0)
    def _():
        wb_ref[...] = w_ref[...].astype(BF16)

    for t in range(x_ref.shape[0] // SUB_ROWS):
        rows = slice(t * SUB_ROWS, (t + 1) * SUB_ROWS)
        req, off = divmod(t * SUB_ROWS, seq_len)
        mix = jnp.concatenate([mc_ref[rows, :], ma_ref[rows, :]], axis=-1)
        acc = jnp.dot(mix, wb_ref[...], preferred_element_type=F32)
        x = x_ref[rows, :] + mod_ref[2:3, :] * acc
        xn_ref[rows, :] = x
        ms = jnp.mean(x * x, axis=-1, keepdims=True)
        y = x * lax.rsqrt(ms + EPS) * g_ref[...]
        h2 = (y * (1.0 + mod_ref[4:5, :]) + mod_ref[3:4, :]).astype(BF16)
        h2_ref[rows, :] = h2
        logits = jnp.dot(h2, wr_ref[...], preferred_element_type=F32)
        lane = lax.broadcasted_iota(jnp.int32, logits.shape, 1)
        logits = jnp.where(lane < N_EXPERTS, logits, -jnp.inf)
        e = jnp.exp(logits - jnp.max(logits, axis=-1, keepdims=True))
        aff = e / jnp.sum(e, axis=-1, keepdims=True)
        aff_ref[req * N_EXPERTS:(req + 1) * N_EXPERTS, off:off + SUB_ROWS] = aff.T[0:N_EXPERTS, :]


def _outproj(x, mix_conv, mix_attn, mods, layer, w_out, norm_g, w_router_b, mod_row, seq_len):
    n = x.shape[0]
    tm = ROW_TILE
    assert tm % seq_len == 0 and seq_len % SUB_ROWS == 0
    reqs = tm // seq_len
    row = lambda i: (i, 0)
    return pl.pallas_call(
        functools.partial(_outproj_kernel, seq_len=seq_len),
        grid=(n // tm,),
        in_specs=[
            pl.BlockSpec((tm, D_MODEL), row),
            pl.BlockSpec((tm, D_CONV), row),
            pl.BlockSpec((tm, D_ATTN), row),
            pl.BlockSpec((None, None, 6, D_MODEL), lambda i: (layer, mod_row(i), 0, 0)),
            pl.BlockSpec((None, D_MIX, D_MODEL), lambda i: (layer, 0, 0), pipeline_mode=pl.Buffered(1)),
            pl.BlockSpec((None, 1, D_MODEL), lambda i: (layer, 0, 0)),
            pl.BlockSpec((None, D_MODEL, LANES), lambda i: (layer, 0, 0)),
        ],
        out_specs=[pl.BlockSpec((tm, D_MODEL), row), pl.BlockSpec((tm, D_MODEL), row),
                   pl.BlockSpec((reqs * N_EXPERTS, seq_len), row)],
        out_shape=[jax.ShapeDtypeStruct((n, D_MODEL), F32), jax.ShapeDtypeStruct((n, D_MODEL), BF16),
                   jax.ShapeDtypeStruct((n // seq_len * N_EXPERTS, seq_len), F32)],
        scratch_shapes=[pltpu.VMEM((D_MIX, D_MODEL), BF16)],
        compiler_params=_params("arbitrary"),
        name="outproj",
    )(x, mix_conv, mix_attn, mods, w_out, norm_g, w_router_b)


def _top_slots(a, tri, cap):
    rows = a.shape[0]
    capf = float(cap)

    def count_ge(v):
        return jnp.sum(jnp.where(a >= v, 1.0, 0.0), axis=-1, keepdims=True)

    hi = jnp.full((rows, 1), 2.0, F32)
    for j in range(6, -1, -1):
        cand = hi * (2.0 ** -(2 ** j))
        hi = jnp.where(count_ge(cand) < capf, cand, hi)
    lo = hi * 0.5
    step = lo * 0.5
    for _ in range(MANTISSA_BITS):
        cand = lo + step
        lo = jnp.where(count_ge(cand) >= capf, cand, lo)
        step = step * 0.5
    upper = jnp.where(step > 0.0, lo + step * 2.0, hi)
    above = jnp.where(a >= upper, 1.0, 0.0)
    tie = jnp.where(a >= lo, 1.0, 0.0) - above
    need = capf - jnp.sum(above, axis=-1, keepdims=True)
    tie_rank = jnp.dot(tie.astype(BF16), tri, preferred_element_type=F32)
    sel = above + tie * jnp.where(tie_rank <= need, 1.0, 0.0)
    slot = jnp.dot(sel.astype(BF16), tri, preferred_element_type=F32) - 1.0
    return jnp.where(sel > 0.5, slot, -1.0)


def _select_kernel(ap_ref, tp_ref, as_ref, ts_ref, pp_ref, pcp_ref, ps_ref, pcs_ref, *, caps):
    pos_p = _top_slots(ap_ref[...], tp_ref[...], caps[0])
    pos_s = _top_slots(as_ref[...], ts_ref[...], caps[1])
    pp_ref[...] = pos_p
    pcp_ref[...] = pos_p.T
    ps_ref[...] = pos_s
    pcs_ref[...] = pos_s.T


def _select(aff_p, tri_p, cap_p, aff_s, tri_s, cap_s):
    whole = lambda a: pl.BlockSpec(a.shape, lambda i: (0, 0))
    outs = [jax.ShapeDtypeStruct(aff_p.shape, F32), jax.ShapeDtypeStruct(aff_p.shape[::-1], F32),
            jax.ShapeDtypeStruct(aff_s.shape, F32), jax.ShapeDtypeStruct(aff_s.shape[::-1], F32)]
    return pl.pallas_call(
        functools.partial(_select_kernel, caps=(cap_p, cap_s)),
        grid=(1,),
        in_specs=[whole(aff_p), whole(tri_p), whole(aff_s), whole(tri_s)],
        out_specs=[whole(o) for o in outs],
        out_shape=outs,
        compiler_params=_params("arbitrary"),
        name="select",
    )(aff_p, tri_p, aff_s, tri_s)


def _gather_kernel(pos_ref, a_ref, h2_ref, xs_ref, gs_ref, oh_ref, *, cap):
    reqs, _, T = oh_ref.shape
    slot = lax.broadcasted_iota(jnp.int32, (cap, T), 0).astype(F32)
    for r in range(reqs):
        for e in range(N_EXPERTS):
            row = r * N_EXPERTS + e
            onehot = jnp.where(pos_ref[row:row + 1, :] == slot, 1.0, 0.0)
            oh_ref[r, e * cap:(e + 1) * cap, :] = onehot.astype(BF16)
            gs_ref[e, r * cap:(r + 1) * cap, :] = jnp.sum(onehot * a_ref[row:row + 1, :], axis=-1,
                                                          keepdims=True)
        xs = jnp.dot(oh_ref[r], h2_ref[r * T:(r + 1) * T, :], preferred_element_type=F32)
        xs_ref[:, r * cap:(r + 1) * cap, :] = xs.reshape(N_EXPERTS, cap, D_MODEL).astype(BF16)


def _gather(pos, aff_t, h2, seq_len):
    T = seq_len
    assert ROW_TILE % T == 0
    reqs = ROW_TILE // T
    nreq = h2.shape[0] // T
    cap = EC_FACTOR * T // N_EXPERTS
    rows = pl.BlockSpec((reqs * N_EXPERTS, T), lambda b: (b, 0))
    return pl.pallas_call(
        functools.partial(_gather_kernel, cap=cap),
        grid=(nreq // reqs,),
        in_specs=[rows, rows, pl.BlockSpec((reqs * T, D_MODEL), lambda b: (b, 0))],
        out_specs=[pl.BlockSpec((N_EXPERTS, reqs * cap, D_MODEL), lambda b: (0, b, 0)),
                   pl.BlockSpec((N_EXPERTS, reqs * cap, 1), lambda b: (0, b, 0))],
        out_shape=[jax.ShapeDtypeStruct((N_EXPERTS, nreq * cap, D_MODEL), BF16),
                   jax.ShapeDtypeStruct((N_EXPERTS, nreq * cap, 1), F32)],
        scratch_shapes=[pltpu.VMEM((reqs, N_EXPERTS * cap, T), BF16)],
        compiler_params=_params("parallel"),
        name="gather",
    )(pos, aff_t, h2)


def _expert_kernel(xp_ref, xs_ref, gp_ref, gsm_ref, wg_ref, wu_ref, wd_ref, yp_ref, ys_ref,
                   wgb_ref, wub_ref, wdb_ref):
    wgb_ref[...] = wg_ref[...].astype(BF16)
    wub_ref[...] = wu_ref[...].astype(BF16)
    wdb_ref[...] = wd_ref[...].astype(BF16)
    for x_ref, g_ref, y_ref in ((xp_ref, gp_ref, yp_ref), (xs_ref, gsm_ref, ys_ref)):
        for t in range(x_ref.shape[0] // EXPERT_ROW_TILE):
            rows = slice(t * EXPERT_ROW_TILE, (t + 1) * EXPERT_ROW_TILE)
            x = x_ref[rows, :]
            hg = jnp.dot(x, wgb_ref[...], preferred_element_type=F32)
            hu = jnp.dot(x, wub_ref[...], preferred_element_type=F32)
            hid = (hg * _sigmoid(hg) * hu).astype(BF16)
            y = jnp.dot(hid, wdb_ref[...], preferred_element_type=F32)
            y_ref[rows, :] = (y * g_ref[rows, :]).astype(BF16)


def _experts(xs_p, xs_s, gs_p, gs_s, layer, w_gate, w_up, w_down):
    np_, ns_ = xs_p.shape[1], xs_s.shape[1]
    slots = lambda n, w: pl.BlockSpec((None, n, w), lambda e: (e, 0, 0))
    weight = lambda rows, cols: pl.BlockSpec((None, None, rows, cols), lambda e: (layer, e, 0, 0))
    return pl.pallas_call(
        _expert_kernel,
        grid=(N_EXPERTS,),
        in_specs=[
            slots(np_, D_MODEL), slots(ns_, D_MODEL), slots(np_, 1), slots(ns_, 1),
            weight(D_MODEL, D_EXPERT), weight(D_MODEL, D_EXPERT), weight(D_EXPERT, D_MODEL),
        ],
        out_specs=[slots(np_, D_MODEL), slots(ns_, D_MODEL)],
        out_shape=[jax.ShapeDtypeStruct(xs_p.shape, BF16), jax.ShapeDtypeStruct(xs_s.shape, BF16)],
        scratch_shapes=[pltpu.VMEM((D_MODEL, D_EXPERT), BF16), pltpu.VMEM((D_MODEL, D_EXPERT), BF16),
                        pltpu.VMEM((D_EXPERT, D_MODEL), BF16)],
        compiler_params=pltpu.CompilerParams(dimension_semantics=("parallel",),
                                             vmem_limit_bytes=EXPERT_VMEM_LIMIT),
        name="experts",
    )(xs_p, xs_s, gs_p, gs_s, w_gate, w_up, w_down)


def _combine_kernel(*refs, cap, final):
    xn_ref, pc_ref, ye_ref, mod_ref, ex_ref = refs[:5]
    if final:
        fg_ref = refs[5]
    o_ref = refs[-1]
    seq_len = pc_ref.shape[0]
    reqs = xn_ref.shape[0] // seq_len
    first_req = pl.program_id(0) * reqs
    for t in range(xn_ref.shape[0] // SUB_ROWS):
        rows = slice(t * SUB_ROWS, (t + 1) * SUB_ROWS)
        req, off = divmod(t * SUB_ROWS, seq_len)
        ye = ye_ref[:, req * cap:(req + 1) * cap, :].reshape(N_EXPERTS * cap, D_MODEL)
        expander = ex_ref[(first_req + req) % ex_ref.shape[0]]
        pos = jnp.dot(pc_ref[off:off + SUB_ROWS, :].astype(BF16), expander, preferred_element_type=F32)
        lane = lax.broadcasted_iota(jnp.int32, pos.shape, 1)
        slot = (lane % cap).astype(F32)
        onehot = jnp.where(pos == slot, 1.0, 0.0).astype(BF16)
        moe = jnp.dot(onehot, ye, preferred_element_type=F32)
        x = xn_ref[rows, :] + mod_ref[5:6, :] * moe
        if final:
            x = x * lax.rsqrt(jnp.mean(x * x, axis=-1, keepdims=True) + EPS) * fg_ref[...]
        o_ref[rows, :] = x


def _combine(xn, pos_col, ye, mods, layer, mod_row, seq_len, final_g):
    n = xn.shape[0]
    T = seq_len
    tm = ROW_TILE
    reqs = tm // T
    cap = EC_FACTOR * T // N_EXPERTS
    final = final_g is not None
    group = LANES // N_EXPERTS
    assert tm % T == 0 and T % SUB_ROWS == 0 and group % reqs == 0
    src = jnp.arange(group)[:, None, None] * N_EXPERTS + (jnp.arange(N_EXPERTS * cap) // cap)[None, None, :]
    expander = (jnp.arange(LANES)[None, :, None] == src).astype(BF16)
    in_specs = [
        pl.BlockSpec((tm, D_MODEL), lambda i: (i, 0)),
        pl.BlockSpec((T, LANES), lambda i: (0, i * reqs // group)),
        pl.BlockSpec((N_EXPERTS, reqs * cap, D_MODEL), lambda i: (0, i, 0)),
        pl.BlockSpec((None, None, 6, D_MODEL), lambda i: (layer, mod_row(i), 0, 0)),
        pl.BlockSpec((group, LANES, N_EXPERTS * cap), lambda i: (0, 0, 0)),
    ]
    args = [xn, pos_col, ye, mods, expander]
    if final:
        in_specs.append(pl.BlockSpec((1, D_MODEL), lambda i: (0, 0)))
        args.append(final_g)
    return pl.pallas_call(
        functools.partial(_combine_kernel, cap=cap, final=final),
        grid=(n // tm,),
        in_specs=in_specs,
        out_specs=pl.BlockSpec((tm, D_MODEL), lambda i: (i, 0)),
        out_shape=jax.ShapeDtypeStruct((n, D_MODEL), F32),
        compiler_params=_params("parallel"),
        name="combine",
    )(*args)


def _rope_tables(T):
    rows = T // GRID_W
    row = jnp.repeat(jnp.arange(rows, dtype=F32), GRID_W)
    col = jnp.tile(jnp.arange(GRID_W, dtype=F32), rows)
    inv = ROPE_BASE ** (-jnp.arange(0, ROPE_AXIS, 2, dtype=F32) / ROPE_AXIS)
    ar = row[:, None] * inv
    ac = col[:, None] * inv
    cos = jnp.concatenate([jnp.cos(ar), jnp.cos(ar), jnp.cos(ac), jnp.cos(ac)], axis=-1)
    sin = jnp.concatenate([-jnp.sin(ar), jnp.sin(ar), -jnp.sin(ac), jnp.sin(ac)], axis=-1)
    reps = D_QK // HEAD_DIM
    return jnp.tile(cos, (1, reps)), jnp.tile(sin, (1, reps))


def kernel(x_prompt, x_sample, c, cache_k, cache_v, c_ctx, norm1_g, w_mod, b_mod, w_in, conv_w, conv_b,
           conv_ln_g, conv_ln_b, lambda_q1, lambda_k1, lambda_q2, lambda_k2, subln_g, w_out, norm2_g,
           w_router, w_gate, w_up, w_down, final_norm_g):
    batch, seq, _ = x_prompt.shape
    dec_batch, dec_seq, _ = x_sample.shape
    past = cache_k.shape[2]
    ctx_k = cache_k.reshape(dec_batch, DEPTH, past, D_QK)
    assert 1 + dec_batch <= MOD_ROWS

    cvec = jnp.concatenate([c_ctx[None, :], c, jnp.zeros((MOD_ROWS - 1 - dec_batch, D_MODEL), F32)], axis=0)
    mods = _modulation(cvec, w_mod, b_mod)

    w_router_b = jnp.pad(w_router, ((0, 0), (0, 0), (0, LANES - N_EXPERTS))).astype(BF16)
    lam_vecs = jnp.stack([lambda_q1, lambda_k1, lambda_q2, lambda_k2], axis=1)
    vec = lambda a: a.reshape(DEPTH, 1, a.shape[-1])
    norm1_v, norm2_v, subln_v = vec(norm1_g), vec(norm2_g), vec(subln_g)
    conv_b_v, ln_g_v, ln_b_v = vec(conv_b), vec(conv_ln_g), vec(conv_ln_b)
    rope_tabs = _rope_tables(dec_seq)
    final_g = final_norm_g.reshape(1, D_MODEL)
    tri_p = (jnp.arange(seq)[:, None] <= jnp.arange(seq)[None, :]).astype(BF16)
    tri_s = (jnp.arange(dec_seq)[:, None] <= jnp.arange(dec_seq)[None, :]).astype(BF16)

    prompt_row = lambda i: 0
    assert dec_seq % ROW_TILE == 0
    tiles_per_sample = dec_seq // ROW_TILE
    sample_row = lambda i: 1 + i // tiles_per_sample

    xp = x_prompt.reshape(batch * seq, D_MODEL)
    xs = x_sample.reshape(dec_batch * dec_seq, D_MODEL)
    caches = (jnp.zeros((batch, DEPTH, D_QK, seq), F32), jnp.zeros((batch, DEPTH, seq, N_HEADS, V_DIM), F32))
    for l in range(DEPTH):
        last = l == DEPTH - 1
        up, qp, kp, vp, *caches = _inproj(xp, mods, l, norm1_v, w_in, prompt_row, None, seq, caches)
        us, qs, ks, vs = _inproj(xs, mods, l, norm1_v, w_in, sample_row, rope_tabs, dec_seq)
        cp = _conv(up, l, conv_w, conv_b_v, ln_g_v, ln_b_v, seq)
        cs = _conv(us, l, conv_w, conv_b_v, ln_g_v, ln_b_v, dec_seq)
        op = _attention(qp, kp, vp, lam_vecs, subln_v, l, seq, None)
        os_ = _attention(qs, ks, vs, lam_vecs, subln_v, l, dec_seq, (ctx_k, cache_v))
        xnp, h2p, affp = _outproj(xp, cp, op, mods, l, w_out, norm2_v, w_router_b, prompt_row, seq)
        xns, h2s, affs = _outproj(xs, cs, os_, mods, l, w_out, norm2_v, w_router_b, sample_row, dec_seq)
        pos_p, rank_p, pos_s, rank_s = _select(affp, tri_p, EC_FACTOR * seq // N_EXPERTS,
                                               affs, tri_s, EC_FACTOR * dec_seq // N_EXPERTS)
        slots_p, gates_p = _gather(pos_p, affp, h2p, seq)
        slots_s, gates_s = _gather(pos_s, affs, h2s, dec_seq)
        ye_p, ye_s = _experts(slots_p, slots_s, gates_p, gates_s, l, w_gate, w_up, w_down)
        xp = _combine(xnp, rank_p, ye_p, mods, l, prompt_row, seq, final_g if last else None)
        xs = _combine(xns, rank_s, ye_s, mods, l, sample_row, dec_seq, final_g if last else None)

    y_prompt = xp.reshape(batch, seq, D_MODEL)
    y_sample = xs.reshape(dec_batch, dec_seq, D_MODEL)
    keys_t, new_v = caches
    new_k = keys_t.reshape(batch, DEPTH, N_HEADS, 2, HEAD_DIM, seq).transpose(0, 1, 5, 2, 3, 4)
    return (y_prompt, y_sample, new_k, new_v)
```

```python
import functools
import math

import jax
import jax.numpy as jnp
from jax import lax
from jax.experimental import pallas as pl
from jax.experimental.pallas import tpu as pltpu

D_MODEL = 1024
DEPTH = 4
GRID_W = 64
D_CONV = 512
CONV_WIDTH = 31
N_HEADS = 4
HEAD_DIM = 64
V_DIM = 2 * HEAD_DIM
D_QK = N_HEADS * 2 * HEAD_DIM
D_ATTN = N_HEADS * V_DIM
D_MIX = D_CONV + D_ATTN
D_IN = 2 * D_CONV + 2 * D_QK + D_ATTN
N_EXPERTS = 16
EC_FACTOR = 2
D_EXPERT = 1024
ROPE_BASE = 10000.0
ROPE_AXIS = HEAD_DIM // 2
EPS = 1e-6
LOG2_E = math.log2(math.e)

LANES = 128
MOD_ROWS = 16
TOKEN_TILE = 1024
ROW_TILE = 1024
SUB_ROWS = 256
CONV_CHUNK = 128
NORM_CHUNK = 512
CONV_PAD = 16
EXPERT_ROW_TILE = 256
GATHER_GROUP = 4
MANTISSA_BITS = 23
VMEM_LIMIT = 56 * 1024 * 1024
EXPERT_VMEM_LIMIT = 60 * 1024 * 1024

F32 = jnp.float32
BF16 = jnp.bfloat16


def _sigmoid(x):
    return 1.0 / (1.0 + jnp.exp(-x))


def _params(*sem):
    return pltpu.CompilerParams(dimension_semantics=sem, vmem_limit_bytes=VMEM_LIMIT)


def _mod_kernel(c_ref, w_ref, b_ref, o_ref):
    c = c_ref[...]
    s = c * _sigmoid(c)
    o_ref[...] = jnp.dot(s.astype(BF16), w_ref[...].astype(BF16),
                         preferred_element_type=F32) + b_ref[...]


def _modulation(cvec, w_mod, b_mod):
    tn = 1536
    out = pl.pallas_call(
        _mod_kernel,
        grid=(DEPTH, 6 * D_MODEL // tn),
        in_specs=[
            pl.BlockSpec((MOD_ROWS, D_MODEL), lambda l, j: (0, 0)),
            pl.BlockSpec((None, D_MODEL, tn), lambda l, j: (l, 0, j)),
            pl.BlockSpec((None, 1, tn), lambda l, j: (l, 0, j)),
        ],
        out_specs=pl.BlockSpec((None, MOD_ROWS, tn), lambda l, j: (l, 0, j)),
        out_shape=jax.ShapeDtypeStruct((DEPTH, MOD_ROWS, 6 * D_MODEL), F32),
        compiler_params=_params("parallel", "parallel"),
        name="modulation",
    )(cvec, w_mod, b_mod.reshape(DEPTH, 1, 6 * D_MODEL))
    return out.reshape(DEPTH, MOD_ROWS, 6, D_MODEL)


def _inproj_kernel(*refs, rope, cache_out, seq_len):
    x_ref, mod_ref, g_ref, w_ref = refs[:4]
    refs = refs[4:]
    if rope:
        cos_ref, sin_ref = refs[:2]
        refs = refs[2:]
    if cache_out:
        refs = refs[2:]
    u_ref, q_ref, k_ref, v_ref = refs[:4]
    wb_ref = refs[-1]

    @pl.when(pl.program_id(0) == 0)
    def _():
        wb_ref[...] = w_ref[...].astype(BF16)

    for t in range(x_ref.shape[0] // SUB_ROWS):
        rows = slice(t * SUB_ROWS, (t + 1) * SUB_ROWS)
        x = x_ref[rows, :]
        ms = jnp.mean(x * x, axis=-1, keepdims=True)
        y = x * lax.rsqrt(ms + EPS) * g_ref[...]
        h = y * (1.0 + mod_ref[1:2, :]) + mod_ref[0:1, :]
        z = jnp.dot(h.astype(BF16), wb_ref[...], preferred_element_type=F32)
        u_ref[rows, :] = z[:, :D_CONV] * _sigmoid(z[:, D_CONV:2 * D_CONV])
        q = z[:, 2 * D_CONV:2 * D_CONV + D_QK]
        k = z[:, 2 * D_CONV + D_QK:2 * D_CONV + 2 * D_QK]
        v = z[:, 2 * D_CONV + 2 * D_QK:]
        if cache_out:
            kf_ref, vf_ref = refs[4:6]
            req, off = divmod(t * SUB_ROWS, seq_len)
            kf_ref[req, :, off:off + SUB_ROWS] = k.T
            for hd in range(N_HEADS):
                vf_ref[req, off:off + SUB_ROWS, hd, :] = v[:, hd * V_DIM:(hd + 1) * V_DIM]
        if rope:
            lane = lax.broadcasted_iota(jnp.int32, q.shape, 1)
            first = (lane % ROPE_AXIS) < (ROPE_AXIS // 2)
            cos = cos_ref[rows, :]
            sin = sin_ref[rows, :]

            def rot(a):
                partner = jnp.where(first, pltpu.roll(a, D_QK - ROPE_AXIS // 2, 1),
                                    pltpu.roll(a, ROPE_AXIS // 2, 1))
                return a * cos + partner * sin

            q = rot(q)
            k = rot(k)
        q_ref[rows, :] = (q * (HEAD_DIM ** -0.5 * LOG2_E)).astype(BF16)
        k_ref[rows, :] = k.astype(BF16)
        v_ref[rows, :] = v.astype(BF16)


def _inproj(x, mods, layer, norm_g, w_in, mod_row, rope_tabs, seq_len, caches=None):
    n = x.shape[0]
    tm = ROW_TILE
    assert rope_tabs is None or seq_len % tm == 0
    tiles_per_seq = max(1, seq_len // tm)
    rope = rope_tabs is not None
    cache_out = caches is not None
    in_specs = [
        pl.BlockSpec((tm, D_MODEL), lambda i: (i, 0)),
        pl.BlockSpec((None, None, 6, D_MODEL), lambda i: (layer, mod_row(i), 0, 0)),
        pl.BlockSpec((None, 1, D_MODEL), lambda i: (layer, 0, 0)),
        pl.BlockSpec((None, D_MODEL, D_IN), lambda i: (layer, 0, 0), pipeline_mode=pl.Buffered(1)),
    ]
    args = [x, mods, norm_g, w_in]
    if rope:
        in_specs += [pl.BlockSpec((tm, D_QK), lambda i: (i % tiles_per_seq, 0))] * 2
        args += list(rope_tabs)
    half = pl.BlockSpec((tm, D_CONV), lambda i: (i, 0))
    out_specs = [half, half, half, half]
    out_shape = [jax.ShapeDtypeStruct((n, D_CONV), F32)] + [jax.ShapeDtypeStruct((n, D_QK), BF16)] * 3
    aliases = {}
    if cache_out:
        assert tm % seq_len == 0
        reqs = tm // seq_len
        out_specs += [pl.BlockSpec((reqs, None, D_QK, seq_len), lambda i: (i, layer, 0, 0)),
                      pl.BlockSpec((reqs, None, seq_len, N_HEADS, V_DIM), lambda i: (i, layer, 0, 0, 0))]
        out_shape += [jax.ShapeDtypeStruct(c.shape, c.dtype) for c in caches]
        aliases = {len(args): len(out_shape) - 2, len(args) + 1: len(out_shape) - 1}
        in_specs += [pl.BlockSpec(memory_space=pl.ANY)] * 2
        args += list(caches)
    return pl.pallas_call(
        functools.partial(_inproj_kernel, rope=rope, cache_out=cache_out, seq_len=seq_len),
        grid=(n // tm,),
        in_specs=in_specs,
        out_specs=out_specs,
        out_shape=out_shape,
        input_output_aliases=aliases,
        scratch_shapes=[pltpu.VMEM((D_MODEL, D_IN), BF16)],
        compiler_params=_params("arbitrary"),
        name="inproj",
    )(*args)


def _conv_stage(u_ref, w_ref, pad_ref, sh_ref, wb_ref):
    T = u_ref.shape[0]
    tail = CONV_PAD + T
    for k in range(CONV_WIDTH):
        wb_ref[k] = jnp.broadcast_to(w_ref[k:k + 1, :], (8, D_CONV))
    pad_ref[0:CONV_PAD, :] = jnp.zeros((CONV_PAD, D_CONV), F32)
    pad_ref[tail:tail + CONV_PAD, :] = jnp.zeros((CONV_PAD, D_CONV), F32)
    pad_ref[CONV_PAD:tail, :] = u_ref[...]
    rows = sh_ref.shape[1]
    for r in range(8):
        sh_ref[r] = pad_ref[r:r + rows, :]


def _conv_block(sh_ref, w, base, lanes):
    first = CONV_PAD - CONV_WIDTH // 2
    groups = CONV_CHUNK // 8
    acc = jnp.zeros((groups, 8, LANES), F32)
    for r in range(8):
        taps = [(k, (first + k) // 8) for k in range(CONV_WIDTH) if (first + k) % 8 == r]
        m0 = min(m for _, m in taps)
        span = max(m for _, m in taps) - m0 + groups
        x = sh_ref[r, pl.ds(base + 8 * m0, 8 * span), lanes].reshape(span, 8, LANES)
        for k, m in taps:
            acc = acc + w[k] * x[m - m0:m - m0 + groups]
    return acc.reshape(CONV_CHUNK, LANES)


def _conv_norm(acc, b_ref, lg_ref, lb_ref):
    acc = acc + b_ref[...]
    mu = jnp.mean(acc, axis=-1, keepdims=True)
    d = acc - mu
    var = jnp.mean(d * d, axis=-1, keepdims=True)
    un = d * lax.rsqrt(var + EPS) * lg_ref[...] + lb_ref[...]
    return (un * _sigmoid(un)).astype(BF16)


def _conv_kernel(u_ref, w_ref, b_ref, lg_ref, lb_ref, o_ref, pad_ref, sh_ref, wb_ref, acc_ref):
    T = u_ref.shape[0]
    _conv_stage(u_ref, w_ref, pad_ref, sh_ref, wb_ref)

    for q in range(D_CONV // LANES):
        lanes = slice(q * LANES, (q + 1) * LANES)
        w = [wb_ref[k, :, lanes] for k in range(CONV_WIDTH)]

        def taps_chunk(i, carry, lanes=lanes, w=w):
            base = pl.multiple_of(i * CONV_CHUNK, CONV_CHUNK)
            acc_ref[pl.ds(base, CONV_CHUNK), lanes] = _conv_block(sh_ref, w, base, lanes)
            return carry

        lax.fori_loop(0, T // CONV_CHUNK, taps_chunk, 0)

    rows = min(NORM_CHUNK, T)

    def norm_chunk(i, carry):
        base = pl.multiple_of(i * rows, rows)
        o_ref[pl.ds(base, rows), :] = _conv_norm(acc_ref[pl.ds(base, rows), :], b_ref, lg_ref, lb_ref)
        return carry

    lax.fori_loop(0, T // rows, norm_chunk, 0)


def _conv(u, layer, conv_w, conv_b, ln_g, ln_b, seq_len):
    n = u.shape[0]
    T = seq_len
    rows = T + 2 * CONV_PAD - 8
    vec = pl.BlockSpec((None, 1, D_CONV), lambda b: (layer, 0, 0))
    return pl.pallas_call(
        _conv_kernel,
        grid=(n // T,),
        in_specs=[
            pl.BlockSpec((T, D_CONV), lambda b: (b, 0)),
            pl.BlockSpec((None, CONV_WIDTH, D_CONV), lambda b: (layer, 0, 0)),
            vec, vec, vec,
        ],
        out_specs=pl.BlockSpec((T, D_CONV), lambda b: (b, 0)),
        out_shape=jax.ShapeDtypeStruct((n, D_CONV), BF16),
        scratch_shapes=[pltpu.VMEM((T + 2 * CONV_PAD, D_CONV), F32),
                        pltpu.VMEM((8, rows, D_CONV), F32),
                        pltpu.VMEM((CONV_WIDTH, 8, D_CONV), F32),
                        pltpu.VMEM((T, D_CONV), F32)],
        compiler_params=_params("parallel"),
        name="conv",
    )(u, conv_w, conv_b, ln_g, ln_b)


def _attn_kernel(*refs, lam_init, has_ctx, seq_len):
    lam_ref, sg_ref, q_ref, k_ref, v_ref = refs[:5]
    T = seq_len
    reqs = k_ref.shape[0] // T
    if has_ctx:
        ck_ref, cv_ref, o_ref, vals_ref, keys_ref = refs[5:]
    else:
        o_ref, vals_ref = refs[5:]

    @pl.when(pl.program_id(1) == 0)
    def _():
        vals_ref[...] = jnp.ones(vals_ref.shape, BF16)
        for r in range(reqs):
            for h in range(N_HEADS):
                vals_ref[r, 0:T, 2 * h * V_DIM:(2 * h + 1) * V_DIM] = v_ref[r * T:(r + 1) * T,
                                                                            h * V_DIM:(h + 1) * V_DIM]
        if has_ctx:
            keys_ref[0:T, :] = k_ref[...]
            keys_ref[T:, :] = ck_ref[...].astype(BF16)
            for h in range(N_HEADS):
                vals_ref[0, T:, 2 * h * V_DIM:(2 * h + 1) * V_DIM] = cv_ref[:, h, :].astype(BF16)

    l = lam_ref[...]
    lam = (jnp.exp(jnp.sum(l[0:1, :] * l[1:2, :], axis=-1, keepdims=True))
           - jnp.exp(jnp.sum(l[2:3, :] * l[3:4, :], axis=-1, keepdims=True)) + lam_init)
    tq = q_ref.shape[0] // reqs
    lane = lax.broadcasted_iota(jnp.int32, (tq, V_DIM), 1)
    nt = (((1,), (1,)), ((), ()))
    for r in range(reqs):
        q_rows = slice(r * tq, (r + 1) * tq)
        for h in range(N_HEADS):
            sl = slice(h * V_DIM, (h + 1) * V_DIM)
            qh = q_ref[q_rows, sl]
            kh = keys_ref[:, sl] if has_ctx else k_ref[r * T:(r + 1) * T, sl]
            heads = []
            for c in range(2):
                keep = (lane < HEAD_DIM) if c == 0 else (lane >= HEAD_DIM)
                qc = jnp.where(keep, qh, jnp.zeros_like(qh))
                s = lax.dot_general(qc, kh, nt, preferred_element_type=F32)
                e = jnp.exp2(s - jnp.max(s, axis=-1, keepdims=True)).astype(BF16)
                pv = jnp.dot(e, vals_ref[r, :, 2 * h * V_DIM:(2 * h + 2) * V_DIM],
                             preferred_element_type=F32)
                heads.append(pv[:, :V_DIM] / pv[:, V_DIM:])
            o = heads[0] - lam * heads[1]
            o = o * lax.rsqrt(jnp.mean(o * o, axis=-1, keepdims=True) + EPS)
            o = o * sg_ref[...] * (1.0 - lam_init)
            o_ref[q_rows, sl] = o.astype(BF16)


def _attention(q, k, v, lam_vecs, subln_g, layer, seq_len, ctx):
    n = q.shape[0]
    T = seq_len
    has_ctx = ctx is not None
    tq = min(TOKEN_TILE, T)
    nq = T // tq
    reqs = 1 if has_ctx else max(1, TOKEN_TILE // T)
    lam_init = 0.8 - 0.6 * math.exp(-0.3 * layer)
    kv = pl.BlockSpec((reqs * T, D_QK), lambda b, i: (b, 0))
    tile = pl.BlockSpec((reqs * tq, D_QK), lambda b, i: (b * nq + i, 0))
    in_specs = [
        pl.BlockSpec((None, 4, HEAD_DIM), lambda b, i: (layer, 0, 0)),
        pl.BlockSpec((None, 1, V_DIM), lambda b, i: (layer, 0, 0)),
        tile, kv, kv,
    ]
    args = [lam_vecs, subln_g, q, k, v]
    past = ctx[0].shape[2] if has_ctx else 0
    scratch = [pltpu.VMEM((reqs, T + past, 2 * D_ATTN), BF16)]
    if has_ctx:
        in_specs += [pl.BlockSpec((None, None, past, D_QK), lambda b, i: (b, layer, 0, 0)),
                     pl.BlockSpec((None, None, past, N_HEADS, V_DIM), lambda b, i: (b, layer, 0, 0, 0))]
        args += list(ctx)
        scratch.append(pltpu.VMEM((T + past, D_QK), BF16))
    return pl.pallas_call(
        functools.partial(_attn_kernel, lam_init=lam_init, has_ctx=has_ctx, seq_len=T),
        grid=(n // (reqs * T), nq),
        in_specs=in_specs,
        out_specs=tile,
        out_shape=jax.ShapeDtypeStruct((n, D_ATTN), BF16),
        scratch_shapes=scratch,
        compiler_params=_params("parallel", "arbitrary"),
        name="attention",
    )(*args)


def _outproj_kernel(x_ref, mc_ref, ma_ref, mod_ref, w_ref, g_ref, wr_ref, xn_ref, h2_ref, aff_ref, wb_ref,
                    *, seq_len):
    @pl.when(pl.program_id(0) == 0)
    def _():
        wb_ref[...] = w_ref[...].astype(BF16)

    for t in range(x_ref.shape[0] // SUB_ROWS):
        rows = slice(t * SUB_ROWS, (t + 1) * SUB_ROWS)
        req, off = divmod(t * SUB_ROWS, seq_len)
        mix = jnp.concatenate([mc_ref[rows, :], ma_ref[rows, :]], axis=-1)
        acc = jnp.dot(mix, wb_ref[...], preferred_element_type=F32)
        x = x_ref[rows, :] + mod_ref[2:3, :] * acc
        xn_ref[rows, :] = x
        ms = jnp.mean(x * x, axis=-1, keepdims=True)
        y = x * lax.rsqrt(ms + EPS) * g_ref[...]
        h2 = (y * (1.0 + mod_ref[4:5, :]) + mod_ref[3:4, :]).astype(BF16)
        h2_ref[rows, :] = h2
        logits = jnp.dot(h2, wr_ref[...], preferred_element_type=F32)
        lane = lax.broadcasted_iota(jnp.int32, logits.shape, 1)
        logits = jnp.where(lane < N_EXPERTS, logits, -jnp.inf)
        e = jnp.exp(logits - jnp.max(logits, axis=-1, keepdims=True))
        aff = e / jnp.sum(e, axis=-1, keepdims=True)
        aff_ref[req * N_EXPERTS:(req + 1) * N_EXPERTS, off:off + SUB_ROWS] = aff.T[0:N_EXPERTS, :]


def _outproj(x, mix_conv, mix_attn, mods, layer, w_out, norm_g, w_router_b, mod_row, seq_len):
    n = x.shape[0]
    tm = ROW_TILE
    assert tm % seq_len == 0 and seq_len % SUB_ROWS == 0
    reqs = tm // seq_len
    row = lambda i: (i, 0)
    return pl.pallas_call(
        functools.partial(_outproj_kernel, seq_len=seq_len),
        grid=(n // tm,),
        in_specs=[
            pl.BlockSpec((tm, D_MODEL), row),
            pl.BlockSpec((tm, D_CONV), row),
            pl.BlockSpec((tm, D_ATTN), row),
            pl.BlockSpec((None, None, 6, D_MODEL), lambda i: (layer, mod_row(i), 0, 0)),
            pl.BlockSpec((None, D_MIX, D_MODEL), lambda i: (layer, 0, 0), pipeline_mode=pl.Buffered(1)),
            pl.BlockSpec((None, 1, D_MODEL), lambda i: (layer, 0, 0)),
            pl.BlockSpec((None, D_MODEL, LANES), lambda i: (layer, 0, 0)),
        ],
        out_specs=[pl.BlockSpec((tm, D_MODEL), row), pl.BlockSpec((tm, D_MODEL), row),
                   pl.BlockSpec((reqs * N_EXPERTS, seq_len), row)],
        out_shape=[jax.ShapeDtypeStruct((n, D_MODEL), F32), jax.ShapeDtypeStruct((n, D_MODEL), BF16),
                   jax.ShapeDtypeStruct((n // seq_len * N_EXPERTS, seq_len), F32)],
        scratch_shapes=[pltpu.VMEM((D_MIX, D_MODEL), BF16)],
        compiler_params=_params("arbitrary"),
        name="outproj",
    )(x, mix_conv, mix_attn, mods, w_out, norm_g, w_router_b)


def _top_slots(a, tri, cap):
    rows = a.shape[0]
    capf = float(cap)

    def count_ge(v):
        return jnp.sum(jnp.where(a >= v, 1.0, 0.0), axis=-1, keepdims=True)

    hi = jnp.full((rows, 1), 2.0, F32)
    for j in range(6, -1, -1):
        cand = hi * (2.0 ** -(2 ** j))
        hi = jnp.where(count_ge(cand) < capf, cand, hi)
    lo = hi * 0.5
    step = lo * 0.5
    for _ in range(MANTISSA_BITS):
        cand = lo + step
        lo = jnp.where(count_ge(cand) >= capf, cand, lo)
        step = step * 0.5
    upper = jnp.where(step > 0.0, lo + step * 2.0, hi)
    above = jnp.where(a >= upper, 1.0, 0.0)
    tie = jnp.where(a >= lo, 1.0, 0.0) - above
    need = capf - jnp.sum(above, axis=-1, keepdims=True)
    tie_rank = jnp.dot(tie.astype(BF16), tri, preferred_element_type=F32)
    sel = above + tie * jnp.where(tie_rank <= need, 1.0, 0.0)
    slot = jnp.dot(sel.astype(BF16), tri, preferred_element_type=F32) - 1.0
    return jnp.where(sel > 0.5, slot, -1.0)


def _select_kernel(ap_ref, tp_ref, as_ref, ts_ref, pp_ref, pcp_ref, ps_ref, pcs_ref, *, caps):
    pos_p = _top_slots(ap_ref[...], tp_ref[...], caps[0])
    pos_s = _top_slots(as_ref[...], ts_ref[...], caps[1])
    pp_ref[...] = pos_p
    pcp_ref[...] = pos_p.T
    ps_ref[...] = pos_s
    pcs_ref[...] = pos_s.T


def _select(aff_p, tri_p, cap_p, aff_s, tri_s, cap_s):
    whole = lambda a: pl.BlockSpec(a.shape, lambda i: (0, 0))
    outs = [jax.ShapeDtypeStruct(aff_p.shape, F32), jax.ShapeDtypeStruct(aff_p.shape[::-1], F32),
            jax.ShapeDtypeStruct(aff_s.shape, F32), jax.ShapeDtypeStruct(aff_s.shape[::-1], F32)]
    return pl.pallas_call(
        functools.partial(_select_kernel, caps=(cap_p, cap_s)),
        grid=(1,),
        in_specs=[whole(aff_p), whole(tri_p), whole(aff_s), whole(tri_s)],
        out_specs=[whole(o) for o in outs],
        out_shape=outs,
        compiler_params=_params("arbitrary"),
        name="select",
    )(aff_p, tri_p, aff_s, tri_s)


def _gather_kernel(pos_ref, a_ref, h2_ref, xs_ref, gs_ref, oh_ref, *, cap):
    reqs, _, T = oh_ref.shape
    slot = lax.broadcasted_iota(jnp.int32, (cap, T), 0).astype(F32)
    for r in range(reqs):
        for g in range(0, N_EXPERTS, GATHER_GROUP):
            for e in range(g, g + GATHER_GROUP):
                row = r * N_EXPERTS + e
                onehot = jnp.where(pos_ref[row:row + 1, :] == slot, 1.0, 0.0)
                oh_ref[r, e * cap:(e + 1) * cap, :] = onehot.astype(BF16)
                gs_ref[e, r * cap:(r + 1) * cap, :] = jnp.sum(onehot * a_ref[row:row + 1, :], axis=-1,
                                                              keepdims=True)
            xs = jnp.dot(oh_ref[r, g * cap:(g + GATHER_GROUP) * cap, :], h2_ref[r * T:(r + 1) * T, :],
                         preferred_element_type=F32)
            xs_ref[g:g + GATHER_GROUP, r * cap:(r + 1) * cap, :] = xs.reshape(
                GATHER_GROUP, cap, D_MODEL).astype(BF16)


def _gather(pos, aff_t, h2, seq_len):
    T = seq_len
    assert ROW_TILE % T == 0
    reqs = ROW_TILE // T
    nreq = h2.shape[0] // T
    cap = EC_FACTOR * T // N_EXPERTS
    rows = pl.BlockSpec((reqs * N_EXPERTS, T), lambda b: (b, 0))
    return pl.pallas_call(
        functools.partial(_gather_kernel, cap=cap),
        grid=(nreq // reqs,),
        in_specs=[rows, rows, pl.BlockSpec((reqs * T, D_MODEL), lambda b: (b, 0))],
        out_specs=[pl.BlockSpec((N_EXPERTS, reqs * cap, D_MODEL), lambda b: (0, b, 0)),
                   pl.BlockSpec((N_EXPERTS, reqs * cap, 1), lambda b: (0, b, 0))],
        out_shape=[jax.ShapeDtypeStruct((N_EXPERTS, nreq * cap, D_MODEL), BF16),
                   jax.ShapeDtypeStruct((N_EXPERTS, nreq * cap, 1), F32)],
        scratch_shapes=[pltpu.VMEM((reqs, N_EXPERTS * cap, T), BF16)],
        compiler_params=_params("parallel"),
        name="gather",
    )(pos, aff_t, h2)


def _expert_kernel(xp_ref, xs_ref, gp_ref, gsm_ref, wg_ref, wu_ref, wd_ref, yp_ref, ys_ref,
                   wgb_ref, wub_ref, wdb_ref):
    wgb_ref[...] = wg_ref[...].astype(BF16)
    wub_ref[...] = wu_ref[...].astype(BF16)
    wdb_ref[...] = wd_ref[...].astype(BF16)
    for x_ref, g_ref, y_ref in ((xp_ref, gp_ref, yp_ref), (xs_ref, gsm_ref, ys_ref)):
        for t in range(x_ref.shape[0] // EXPERT_ROW_TILE):
            rows = slice(t * EXPERT_ROW_TILE, (t + 1) * EXPERT_ROW_TILE)
            x = x_ref[rows, :]
            hg = jnp.dot(x, wgb_ref[...], preferred_element_type=F32)
            hu = jnp.dot(x, wub_ref[...], preferred_element_type=F32)
            hid = (hg * _sigmoid(hg) * hu).astype(BF16)
            y = jnp.dot(hid, wdb_ref[...], preferred_element_type=F32)
            y_ref[rows, :] = (y * g_ref[rows, :]).astype(BF16)


def _experts(xs_p, xs_s, gs_p, gs_s, layer, w_gate, w_up, w_down):
    np_, ns_ = xs_p.shape[1], xs_s.shape[1]
    slots = lambda n, w: pl.BlockSpec((None, n, w), lambda e: (e, 0, 0))
    weight = lambda rows, cols: pl.BlockSpec((None, None, rows, cols), lambda e: (layer, e, 0, 0))
    return pl.pallas_call(
        _expert_kernel,
        grid=(N_EXPERTS,),
        in_specs=[
            slots(np_, D_MODEL), slots(ns_, D_MODEL), slots(np_, 1), slots(ns_, 1),
            weight(D_MODEL, D_EXPERT), weight(D_MODEL, D_EXPERT), weight(D_EXPERT, D_MODEL),
        ],
        out_specs=[slots(np_, D_MODEL), slots(ns_, D_MODEL)],
        out_shape=[jax.ShapeDtypeStruct(xs_p.shape, BF16), jax.ShapeDtypeStruct(xs_s.shape, BF16)],
        scratch_shapes=[pltpu.VMEM((D_MODEL, D_EXPERT), BF16), pltpu.VMEM((D_MODEL, D_EXPERT), BF16),
                        pltpu.VMEM((D_EXPERT, D_MODEL), BF16)],
        compiler_params=pltpu.CompilerParams(dimension_semantics=("parallel",),
                                             vmem_limit_bytes=EXPERT_VMEM_LIMIT),
        name="experts",
    )(xs_p, xs_s, gs_p, gs_s, w_gate, w_up, w_down)


def _combine_kernel(*refs, cap, final):
    xn_ref, pc_ref, ye_ref, mod_ref, ex_ref = refs[:5]
    if final:
        fg_ref = refs[5]
    o_ref = refs[-1]
    seq_len = pc_ref.shape[0]
    reqs = xn_ref.shape[0] // seq_len
    first_req = pl.program_id(0) * reqs
    for t in range(xn_ref.shape[0] // SUB_ROWS):
        rows = slice(t * SUB_ROWS, (t + 1) * SUB_ROWS)
        req, off = divmod(t * SUB_ROWS, seq_len)
        ye = ye_ref[:, req * cap:(req + 1) * cap, :].reshape(N_EXPERTS * cap, D_MODEL)
        expander = ex_ref[(first_req + req) % ex_ref.shape[0]]
        pos = jnp.dot(pc_ref[off:off + SUB_ROWS, :].astype(BF16), expander, preferred_element_type=F32)
        lane = lax.broadcasted_iota(jnp.int32, pos.shape, 1)
        slot = (lane % cap).astype(F32)
        onehot = jnp.where(pos == slot, 1.0, 0.0).astype(BF16)
        moe = jnp.dot(onehot, ye, preferred_element_type=F32)
        x = xn_ref[rows, :] + mod_ref[5:6, :] * moe
        if final:
            x = x * lax.rsqrt(jnp.mean(x * x, axis=-1, keepdims=True) + EPS) * fg_ref[...]
        o_ref[rows, :] = x


def _combine(xn, pos_col, ye, mods, layer, mod_row, seq_len, final_g):
    n = xn.shape[0]
    T = seq_len
    tm = ROW_TILE
    reqs = tm // T
    cap = EC_FACTOR * T // N_EXPERTS
    final = final_g is not None
    group = LANES // N_EXPERTS
    assert tm % T == 0 and T % SUB_ROWS == 0 and group % reqs == 0
    src = jnp.arange(group)[:, None, None] * N_EXPERTS + (jnp.arange(N_EXPERTS * cap) // cap)[None, None, :]
    expander = (jnp.arange(LANES)[None, :, None] == src).astype(BF16)
    in_specs = [
        pl.BlockSpec((tm, D_MODEL), lambda i: (i, 0)),
        pl.BlockSpec((T, LANES), lambda i: (0, i * reqs // group)),
        pl.BlockSpec((N_EXPERTS, reqs * cap, D_MODEL), lambda i: (0, i, 0)),
        pl.BlockSpec((None, None, 6, D_MODEL), lambda i: (layer, mod_row(i), 0, 0)),
        pl.BlockSpec((group, LANES, N_EXPERTS * cap), lambda i: (0, 0, 0)),
    ]
    args = [xn, pos_col, ye, mods, expander]
    if final:
        in_specs.append(pl.BlockSpec((1, D_MODEL), lambda i: (0, 0)))
        args.append(final_g)
    return pl.pallas_call(
        functools.partial(_combine_kernel, cap=cap, final=final),
        grid=(n // tm,),
        in_specs=in_specs,
        out_specs=pl.BlockSpec((tm, D_MODEL), lambda i: (i, 0)),
        out_shape=jax.ShapeDtypeStruct((n, D_MODEL), F32),
        compiler_params=_params("parallel"),
        name="combine",
    )(*args)


def _rope_tables(T):
    rows = T // GRID_W
    row = jnp.repeat(jnp.arange(rows, dtype=F32), GRID_W)
    col = jnp.tile(jnp.arange(GRID_W, dtype=F32), rows)
    inv = ROPE_BASE ** (-jnp.arange(0, ROPE_AXIS, 2, dtype=F32) / ROPE_AXIS)
    ar = row[:, None] * inv
    ac = col[:, None] * inv
    cos = jnp.concatenate([jnp.cos(ar), jnp.cos(ar), jnp.cos(ac), jnp.cos(ac)], axis=-1)
    sin = jnp.concatenate([-jnp.sin(ar), jnp.sin(ar), -jnp.sin(ac), jnp.sin(ac)], axis=-1)
    reps = D_QK // HEAD_DIM
    return jnp.tile(cos, (1, reps)), jnp.tile(sin, (1, reps))


def kernel(x_prompt, x_sample, c, cache_k, cache_v, c_ctx, norm1_g, w_mod, b_mod, w_in, conv_w, conv_b,
           conv_ln_g, conv_ln_b, lambda_q1, lambda_k1, lambda_q2, lambda_k2, subln_g, w_out, norm2_g,
           w_router, w_gate, w_up, w_down, final_norm_g):
    batch, seq, _ = x_prompt.shape
    dec_batch, dec_seq, _ = x_sample.shape
    past = cache_k.shape[2]
    ctx_k = cache_k.reshape(dec_batch, DEPTH, past, D_QK)
    assert 1 + dec_batch <= MOD_ROWS

    cvec = jnp.concatenate([c_ctx[None, :], c, jnp.zeros((MOD_ROWS - 1 - dec_batch, D_MODEL), F32)], axis=0)
    mods = _modulation(cvec, w_mod, b_mod)

    w_router_b = jnp.pad(w_router, ((0, 0), (0, 0), (0, LANES - N_EXPERTS))).astype(BF16)
    lam_vecs = jnp.stack([lambda_q1, lambda_k1, lambda_q2, lambda_k2], axis=1)
    vec = lambda a: a.reshape(DEPTH, 1, a.shape[-1])
    norm1_v, norm2_v, subln_v = vec(norm1_g), vec(norm2_g), vec(subln_g)
    conv_b_v, ln_g_v, ln_b_v = vec(conv_b), vec(conv_ln_g), vec(conv_ln_b)
    rope_tabs = _rope_tables(dec_seq)
    final_g = final_norm_g.reshape(1, D_MODEL)
    tri_p = (jnp.arange(seq)[:, None] <= jnp.arange(seq)[None, :]).astype(BF16)
    tri_s = (jnp.arange(dec_seq)[:, None] <= jnp.arange(dec_seq)[None, :]).astype(BF16)

    prompt_row = lambda i: 0
    assert dec_seq % ROW_TILE == 0
    tiles_per_sample = dec_seq // ROW_TILE
    sample_row = lambda i: 1 + i // tiles_per_sample

    xp = x_prompt.reshape(batch * seq, D_MODEL)
    xs = x_sample.reshape(dec_batch * dec_seq, D_MODEL)
    caches = (jnp.zeros((batch, DEPTH, D_QK, seq), F32), jnp.zeros((batch, DEPTH, seq, N_HEADS, V_DIM), F32))
    for l in range(DEPTH):
        last = l == DEPTH - 1
        up, qp, kp, vp, *caches = _inproj(xp, mods, l, norm1_v, w_in, prompt_row, None, seq, caches)
        us, qs, ks, vs = _inproj(xs, mods, l, norm1_v, w_in, sample_row, rope_tabs, dec_seq)
        cp = _conv(up, l, conv_w, conv_b_v, ln_g_v, ln_b_v, seq)
        cs = _conv(us, l, conv_w, conv_b_v, ln_g_v, ln_b_v, dec_seq)
        op = _attention(qp, kp, vp, lam_vecs, subln_v, l, seq, None)
        os_ = _attention(qs, ks, vs, lam_vecs, subln_v, l, dec_seq, (ctx_k, cache_v))
        xnp, h2p, affp = _outproj(xp, cp, op, mods, l, w_out, norm2_v, w_router_b, prompt_row, seq)
        xns, h2s, affs = _outproj(xs, cs, os_, mods, l, w_out, norm2_v, w_router_b, sample_row, dec_seq)
        pos_p, rank_p, pos_s, rank_s = _select(affp, tri_p, EC_FACTOR * seq // N_EXPERTS,
                                               affs, tri_s, EC_FACTOR * dec_seq // N_EXPERTS)
        slots_p, gates_p = _gather(pos_p, affp, h2p, seq)
        slots_s, gates_s = _gather(pos_s, affs, h2s, dec_seq)
        ye_p, ye_s = _experts(slots_p, slots_s, gates_p, gates_s, l, w_gate, w_up, w_down)
        xp = _combine(xnp, rank_p, ye_p, mods, l, prompt_row, seq, final_g if last else None)
        xs = _combine(xns, rank_s, ye_s, mods, l, sample_row, dec_seq, final_g if last else None)

    y_prompt = xp.reshape(batch, seq, D_MODEL)
    y_sample = xs.reshape(dec_batch, dec_seq, D_MODEL)
    keys_t, new_v = caches
    new_k = keys_t.reshape(batch, DEPTH, N_HEADS, 2, HEAD_DIM, seq).transpose(0, 1, 5, 2, 3, 4)
    return (y_prompt, y_sample, new_k, new_v)
```

```python
import functools
import math

import jax
import jax.numpy as jnp
from jax import lax
from jax.experimental import pallas as pl
from jax.experimental.pallas import tpu as pltpu

D_MODEL = 1024
DEPTH = 4
GRID_W = 64
D_CONV = 512
CONV_WIDTH = 31
N_HEADS = 4
HEAD_DIM = 64
V_DIM = 2 * HEAD_DIM
D_QK = N_HEADS * 2 * HEAD_DIM
D_ATTN = N_HEADS * V_DIM
D_MIX = D_CONV + D_ATTN
D_IN = 2 * D_CONV + 2 * D_QK + D_ATTN
N_EXPERTS = 16
EC_FACTOR = 2
D_EXPERT = 1024
ROPE_BASE = 10000.0
ROPE_AXIS = HEAD_DIM // 2
EPS = 1e-6
LOG2_E = math.log2(math.e)

LANES = 128
MOD_ROWS = 16
TOKEN_TILE = 1024
ROW_TILE = 1024
SUB_ROWS = 256
CONV_CHUNK = 128
NORM_CHUNK = 512
CONV_PAD = 16
EXPERT_ROW_TILE = 256
GATHER_GROUP = 4
MANTISSA_BITS = 23
VMEM_LIMIT = 56 * 1024 * 1024
EXPERT_VMEM_LIMIT = 60 * 1024 * 1024

F32 = jnp.float32
BF16 = jnp.bfloat16


def _sigmoid(x):
    return 1.0 / (1.0 + jnp.exp(-x))


def _params(*sem):
    return pltpu.CompilerParams(dimension_semantics=sem, vmem_limit_bytes=VMEM_LIMIT)


def _mod_kernel(c_ref, w_ref, b_ref, o_ref):
    c = c_ref[...]
    s = c * _sigmoid(c)
    o_ref[...] = jnp.dot(s.astype(BF16), w_ref[...].astype(BF16),
                         preferred_element_type=F32) + b_ref[...]


def _modulation(cvec, w_mod, b_mod):
    tn = 1536
    out = pl.pallas_call(
        _mod_kernel,
        grid=(DEPTH, 6 * D_MODEL // tn),
        in_specs=[
            pl.BlockSpec((MOD_ROWS, D_MODEL), lambda l, j: (0, 0)),
            pl.BlockSpec((None, D_MODEL, tn), lambda l, j: (l, 0, j)),
            pl.BlockSpec((None, 1, tn), lambda l, j: (l, 0, j)),
        ],
        out_specs=pl.BlockSpec((None, MOD_ROWS, tn), lambda l, j: (l, 0, j)),
        out_shape=jax.ShapeDtypeStruct((DEPTH, MOD_ROWS, 6 * D_MODEL), F32),
        compiler_params=_params("parallel", "parallel"),
        name="modulation",
    )(cvec, w_mod, b_mod.reshape(DEPTH, 1, 6 * D_MODEL))
    return out.reshape(DEPTH, MOD_ROWS, 6, D_MODEL)


def _inproj_kernel(*refs, rope, cache_out, seq_len):
    x_ref, mod_ref, g_ref, w_ref = refs[:4]
    refs = refs[4:]
    if rope:
        cos_ref, sin_ref = refs[:2]
        refs = refs[2:]
    if cache_out:
        refs = refs[2:]
    u_ref, q_ref, k_ref, v_ref = refs[:4]
    wb_ref = refs[-1]

    @pl.when(pl.program_id(0) == 0)
    def _():
        wb_ref[...] = w_ref[...].astype(BF16)

    for r in range(u_ref.shape[0]):
        u_ref[r, 0:CONV_PAD, :] = jnp.zeros((CONV_PAD, D_CONV), F32)
        u_ref[r, CONV_PAD + seq_len:, :] = jnp.zeros((CONV_PAD, D_CONV), F32)

    for t in range(x_ref.shape[0] // SUB_ROWS):
        rows = slice(t * SUB_ROWS, (t + 1) * SUB_ROWS)
        req, off = divmod(t * SUB_ROWS, seq_len)
        x = x_ref[rows, :]
        ms = jnp.mean(x * x, axis=-1, keepdims=True)
        y = x * lax.rsqrt(ms + EPS) * g_ref[...]
        h = y * (1.0 + mod_ref[1:2, :]) + mod_ref[0:1, :]
        z = jnp.dot(h.astype(BF16), wb_ref[...], preferred_element_type=F32)
        u_ref[req, CONV_PAD + off:CONV_PAD + off + SUB_ROWS, :] = (
            z[:, :D_CONV] * _sigmoid(z[:, D_CONV:2 * D_CONV]))
        q = z[:, 2 * D_CONV:2 * D_CONV + D_QK]
        k = z[:, 2 * D_CONV + D_QK:2 * D_CONV + 2 * D_QK]
        v = z[:, 2 * D_CONV + 2 * D_QK:]
        if cache_out:
            kf_ref, vf_ref = refs[4:6]
            kf_ref[req, :, off:off + SUB_ROWS] = k.T
            for hd in range(N_HEADS):
                vf_ref[req, off:off + SUB_ROWS, hd, :] = v[:, hd * V_DIM:(hd + 1) * V_DIM]
        if rope:
            lane = lax.broadcasted_iota(jnp.int32, q.shape, 1)
            first = (lane % ROPE_AXIS) < (ROPE_AXIS // 2)
            cos = cos_ref[rows, :]
            sin = sin_ref[rows, :]

            def rot(a):
                partner = jnp.where(first, pltpu.roll(a, D_QK - ROPE_AXIS // 2, 1),
                                    pltpu.roll(a, ROPE_AXIS // 2, 1))
                return a * cos + partner * sin

            q = rot(q)
            k = rot(k)
        q_ref[rows, :] = (q * (HEAD_DIM ** -0.5 * LOG2_E)).astype(BF16)
        k_ref[rows, :] = k.astype(BF16)
        v_ref[rows, :] = v.astype(BF16)


def _inproj(x, mods, layer, norm_g, w_in, mod_row, rope_tabs, seq_len, caches=None):
    n = x.shape[0]
    tm = ROW_TILE
    assert rope_tabs is None or seq_len % tm == 0
    tiles_per_seq = max(1, seq_len // tm)
    rope = rope_tabs is not None
    cache_out = caches is not None
    in_specs = [
        pl.BlockSpec((tm, D_MODEL), lambda i: (i, 0)),
        pl.BlockSpec((None, None, 6, D_MODEL), lambda i: (layer, mod_row(i), 0, 0)),
        pl.BlockSpec((None, 1, D_MODEL), lambda i: (layer, 0, 0)),
        pl.BlockSpec((None, D_MODEL, D_IN), lambda i: (layer, 0, 0), pipeline_mode=pl.Buffered(1)),
    ]
    args = [x, mods, norm_g, w_in]
    if rope:
        in_specs += [pl.BlockSpec((tm, D_QK), lambda i: (i % tiles_per_seq, 0))] * 2
        args += list(rope_tabs)
    assert tm % seq_len == 0 and seq_len % SUB_ROWS == 0
    reqs = tm // seq_len
    padded = seq_len + 2 * CONV_PAD
    half = pl.BlockSpec((tm, D_CONV), lambda i: (i, 0))
    out_specs = [pl.BlockSpec((reqs, padded, D_CONV), lambda i: (i, 0, 0)), half, half, half]
    out_shape = ([jax.ShapeDtypeStruct((n // seq_len, padded, D_CONV), F32)]
                 + [jax.ShapeDtypeStruct((n, D_QK), BF16)] * 3)
    aliases = {}
    if cache_out:
        out_specs += [pl.BlockSpec((reqs, None, D_QK, seq_len), lambda i: (i, layer, 0, 0)),
                      pl.BlockSpec((reqs, None, seq_len, N_HEADS, V_DIM), lambda i: (i, layer, 0, 0, 0))]
        out_shape += [jax.ShapeDtypeStruct(c.shape, c.dtype) for c in caches]
        aliases = {len(args): len(out_shape) - 2, len(args) + 1: len(out_shape) - 1}
        in_specs += [pl.BlockSpec(memory_space=pl.ANY)] * 2
        args += list(caches)
    return pl.pallas_call(
        functools.partial(_inproj_kernel, rope=rope, cache_out=cache_out, seq_len=seq_len),
        grid=(n // tm,),
        in_specs=in_specs,
        out_specs=out_specs,
        out_shape=out_shape,
        input_output_aliases=aliases,
        scratch_shapes=[pltpu.VMEM((D_MODEL, D_IN), BF16)],
        compiler_params=_params("arbitrary"),
        name="inproj",
    )(*args)


def _conv_stage(pad_ref, w_ref, sh_ref, wb_ref):
    for k in range(CONV_WIDTH):
        wb_ref[k] = jnp.broadcast_to(w_ref[k:k + 1, :], (8, D_CONV))
    rows = sh_ref.shape[1]
    for r in range(1, 8):
        sh_ref[r - 1] = pad_ref[r:r + rows, :]


def _conv_block(pad_ref, sh_ref, w, base, lanes):
    first = CONV_PAD - CONV_WIDTH // 2
    groups = CONV_CHUNK // 8
    acc = jnp.zeros((groups, 8, LANES), F32)
    for r in range(8):
        taps = [(k, (first + k) // 8) for k in range(CONV_WIDTH) if (first + k) % 8 == r]
        m0 = min(m for _, m in taps)
        span = max(m for _, m in taps) - m0 + groups
        window = pl.ds(base + 8 * m0, 8 * span)
        x = (pad_ref[window, lanes] if r == 0 else sh_ref[r - 1, window, lanes]).reshape(span, 8, LANES)
        for k, m in taps:
            acc = acc + w[k] * x[m - m0:m - m0 + groups]
    return acc.reshape(CONV_CHUNK, LANES)


def _conv_norm(acc, b_ref, lg_ref, lb_ref):
    acc = acc + b_ref[...]
    mu = jnp.mean(acc, axis=-1, keepdims=True)
    d = acc - mu
    var = jnp.mean(d * d, axis=-1, keepdims=True)
    un = d * lax.rsqrt(var + EPS) * lg_ref[...] + lb_ref[...]
    return (un * _sigmoid(un)).astype(BF16)


def _conv_kernel(pad_ref, w_ref, b_ref, lg_ref, lb_ref, o_ref, sh_ref, wb_ref, acc_ref):
    T = o_ref.shape[0]
    _conv_stage(pad_ref, w_ref, sh_ref, wb_ref)

    for q in range(D_CONV // LANES):
        lanes = slice(q * LANES, (q + 1) * LANES)
        w = [wb_ref[k, :, lanes] for k in range(CONV_WIDTH)]

        def taps_chunk(i, carry, lanes=lanes, w=w):
            base = pl.multiple_of(i * CONV_CHUNK, CONV_CHUNK)
            acc_ref[pl.ds(base, CONV_CHUNK), lanes] = _conv_block(pad_ref, sh_ref, w, base, lanes)
            return carry

        lax.fori_loop(0, T // CONV_CHUNK, taps_chunk, 0)

    rows = min(NORM_CHUNK, T)

    def norm_chunk(i, carry):
        base = pl.multiple_of(i * rows, rows)
        o_ref[pl.ds(base, rows), :] = _conv_norm(acc_ref[pl.ds(base, rows), :], b_ref, lg_ref, lb_ref)
        return carry

    lax.fori_loop(0, T // rows, norm_chunk, 0)


def _conv(u_pad, layer, conv_w, conv_b, ln_g, ln_b):
    nreq, padded, _ = u_pad.shape
    T = padded - 2 * CONV_PAD
    rows = padded - 8
    vec = pl.BlockSpec((None, 1, D_CONV), lambda b: (layer, 0, 0))
    return pl.pallas_call(
        _conv_kernel,
        grid=(nreq,),
        in_specs=[
            pl.BlockSpec((None, padded, D_CONV), lambda b: (b, 0, 0)),
            pl.BlockSpec((None, CONV_WIDTH, D_CONV), lambda b: (layer, 0, 0)),
            vec, vec, vec,
        ],
        out_specs=pl.BlockSpec((T, D_CONV), lambda b: (b, 0)),
        out_shape=jax.ShapeDtypeStruct((nreq * T, D_CONV), BF16),
        scratch_shapes=[pltpu.VMEM((7, rows, D_CONV), F32),
                        pltpu.VMEM((CONV_WIDTH, 8, D_CONV), F32),
                        pltpu.VMEM((T, D_CONV), F32)],
        compiler_params=_params("parallel"),
        name="conv",
    )(u_pad, conv_w, conv_b, ln_g, ln_b)


def _attn_kernel(*refs, lam_init, has_ctx, seq_len):
    lam_ref, sg_ref, q_ref, k_ref, v_ref = refs[:5]
    T = seq_len
    reqs = k_ref.shape[0] // T
    if has_ctx:
        ck_ref, cv_ref, o_ref, vals_ref, keys_ref = refs[5:]
    else:
        o_ref, vals_ref = refs[5:]

    @pl.when(pl.program_id(1) == 0)
    def _():
        vals_ref[...] = jnp.ones(vals_ref.shape, BF16)
        for r in range(reqs):
            for h in range(N_HEADS):
                vals_ref[r, 0:T, 2 * h * V_DIM:(2 * h + 1) * V_DIM] = v_ref[r * T:(r + 1) * T,
                                                                            h * V_DIM:(h + 1) * V_DIM]
        if has_ctx:
            keys_ref[0:T, :] = k_ref[...]
            keys_ref[T:, :] = ck_ref[...].astype(BF16)
            for h in range(N_HEADS):
                vals_ref[0, T:, 2 * h * V_DIM:(2 * h + 1) * V_DIM] = cv_ref[:, h, :].astype(BF16)

    l = lam_ref[...]
    lam = (jnp.exp(jnp.sum(l[0:1, :] * l[1:2, :], axis=-1, keepdims=True))
           - jnp.exp(jnp.sum(l[2:3, :] * l[3:4, :], axis=-1, keepdims=True)) + lam_init)
    tq = q_ref.shape[0] // reqs
    lane = lax.broadcasted_iota(jnp.int32, (tq, V_DIM), 1)
    nt = (((1,), (1,)), ((), ()))
    for r in range(reqs):
        q_rows = slice(r * tq, (r + 1) * tq)
        for h in range(N_HEADS):
            sl = slice(h * V_DIM, (h + 1) * V_DIM)
            qh = q_ref[q_rows, sl]
            kh = keys_ref[:, sl] if has_ctx else k_ref[r * T:(r + 1) * T, sl]
            heads = []
            for c in range(2):
                keep = (lane < HEAD_DIM) if c == 0 else (lane >= HEAD_DIM)
                qc = jnp.where(keep, qh, jnp.zeros_like(qh))
                s = lax.dot_general(qc, kh, nt, preferred_element_type=F32)
                e = jnp.exp2(s - jnp.max(s, axis=-1, keepdims=True)).astype(BF16)
                pv = jnp.dot(e, vals_ref[r, :, 2 * h * V_DIM:(2 * h + 2) * V_DIM],
                             preferred_element_type=F32)
                heads.append(pv[:, :V_DIM] / pv[:, V_DIM:])
            o = heads[0] - lam * heads[1]
            o = o * lax.rsqrt(jnp.mean(o * o, axis=-1, keepdims=True) + EPS)
            o = o * sg_ref[...] * (1.0 - lam_init)
            o_ref[q_rows, sl] = o.astype(BF16)


def _attention(q, k, v, lam_vecs, subln_g, layer, seq_len, ctx):
    n = q.shape[0]
    T = seq_len
    has_ctx = ctx is not None
    tq = min(TOKEN_TILE, T)
    nq = T // tq
    reqs = 1 if has_ctx else max(1, TOKEN_TILE // T)
    lam_init = 0.8 - 0.6 * math.exp(-0.3 * layer)
    kv = pl.BlockSpec((reqs * T, D_QK), lambda b, i: (b, 0))
    tile = pl.BlockSpec((reqs * tq, D_QK), lambda b, i: (b * nq + i, 0))
    in_specs = [
        pl.BlockSpec((None, 4, HEAD_DIM), lambda b, i: (layer, 0, 0)),
        pl.BlockSpec((None, 1, V_DIM), lambda b, i: (layer, 0, 0)),
        tile, kv, kv,
    ]
    args = [lam_vecs, subln_g, q, k, v]
    past = ctx[0].shape[2] if has_ctx else 0
    scratch = [pltpu.VMEM((reqs, T + past, 2 * D_ATTN), BF16)]
    if has_ctx:
        in_specs += [pl.BlockSpec((None, None, past, D_QK), lambda b, i: (b, layer, 0, 0)),
                     pl.BlockSpec((None, None, past, N_HEADS, V_DIM), lambda b, i: (b, layer, 0, 0, 0))]
        args += list(ctx)
        scratch.append(pltpu.VMEM((T + past, D_QK), BF16))
    return pl.pallas_call(
        functools.partial(_attn_kernel, lam_init=lam_init, has_ctx=has_ctx, seq_len=T),
        grid=(n // (reqs * T), nq),
        in_specs=in_specs,
        out_specs=tile,
        out_shape=jax.ShapeDtypeStruct((n, D_ATTN), BF16),
        scratch_shapes=scratch,
        compiler_params=_params("parallel", "arbitrary"),
        name="attention",
    )(*args)


def _outproj_kernel(x_ref, mc_ref, ma_ref, mod_ref, w_ref, g_ref, wr_ref, xn_ref, h2_ref, aff_ref, wb_ref,
                    *, seq_len):
    @pl.when(pl.program_id(0) == 0)
    def _():
        wb_ref[...] = w_ref[...].astype(BF16)

    for t in range(x_ref.shape[0] // SUB_ROWS):
        rows = slice(t * SUB_ROWS, (t + 1) * SUB_ROWS)
        req, off = divmod(t * SUB_ROWS, seq_len)
        mix = jnp.concatenate([mc_ref[rows, :], ma_ref[rows, :]], axis=-1)
        acc = jnp.dot(mix, wb_ref[...], preferred_element_type=F32)
        x = x_ref[rows, :] + mod_ref[2:3, :] * acc
        xn_ref[rows, :] = x
        ms = jnp.mean(x * x, axis=-1, keepdims=True)
        y = x * lax.rsqrt(ms + EPS) * g_ref[...]
        h2 = (y * (1.0 + mod_ref[4:5, :]) + mod_ref[3:4, :]).astype(BF16)
        h2_ref[rows, :] = h2
        logits = jnp.dot(h2, wr_ref[...], preferred_element_type=F32)
        lane = lax.broadcasted_iota(jnp.int32, logits.shape, 1)
        logits = jnp.where(lane < N_EXPERTS, logits, -jnp.inf)
        e = jnp.exp(logits - jnp.max(logits, axis=-1, keepdims=True))
        aff = e / jnp.sum(e, axis=-1, keepdims=True)
        aff_ref[req * N_EXPERTS:(req + 1) * N_EXPERTS, off:off + SUB_ROWS] = aff.T[0:N_EXPERTS, :]


def _outproj(x, mix_conv, mix_attn, mods, layer, w_out, norm_g, w_router_b, mod_row, seq_len):
    n = x.shape[0]
    tm = ROW_TILE
    assert tm % seq_len == 0 and seq_len % SUB_ROWS == 0
    reqs = tm // seq_len
    row = lambda i: (i, 0)
    return pl.pallas_call(
        functools.partial(_outproj_kernel, seq_len=seq_len),
        grid=(n // tm,),
        in_specs=[
            pl.BlockSpec((tm, D_MODEL), row),
            pl.BlockSpec((tm, D_CONV), row),
            pl.BlockSpec((tm, D_ATTN), row),
            pl.BlockSpec((None, None, 6, D_MODEL), lambda i: (layer, mod_row(i), 0, 0)),
            pl.BlockSpec((None, D_MIX, D_MODEL), lambda i: (layer, 0, 0), pipeline_mode=pl.Buffered(1)),
            pl.BlockSpec((None, 1, D_MODEL), lambda i: (layer, 0, 0)),
            pl.BlockSpec((None, D_MODEL, LANES), lambda i: (layer, 0, 0)),
        ],
        out_specs=[pl.BlockSpec((tm, D_MODEL), row), pl.BlockSpec((tm, D_MODEL), row),
                   pl.BlockSpec((reqs * N_EXPERTS, seq_len), row)],
        out_shape=[jax.ShapeDtypeStruct((n, D_MODEL), F32), jax.ShapeDtypeStruct((n, D_MODEL), BF16),
                   jax.ShapeDtypeStruct((n // seq_len * N_EXPERTS, seq_len), F32)],
        scratch_shapes=[pltpu.VMEM((D_MIX, D_MODEL), BF16)],
        compiler_params=_params("arbitrary"),
        name="outproj",
    )(x, mix_conv, mix_attn, mods, w_out, norm_g, w_router_b)


def _top_slots(a, tri, cap):
    rows = a.shape[0]
    capf = float(cap)

    def count_ge(v):
        return jnp.sum(jnp.where(a >= v, 1.0, 0.0), axis=-1, keepdims=True)

    hi = jnp.full((rows, 1), 2.0, F32)
    for j in range(6, -1, -1):
        cand = hi * (2.0 ** -(2 ** j))
        hi = jnp.where(count_ge(cand) < capf, cand, hi)
    lo = hi * 0.5
    step = lo * 0.5
    for _ in range(MANTISSA_BITS):
        cand = lo + step
        lo = jnp.where(count_ge(cand) >= capf, cand, lo)
        step = step * 0.5
    upper = jnp.where(step > 0.0, lo + step * 2.0, hi)
    above = jnp.where(a >= upper, 1.0, 0.0)
    tie = jnp.where(a >= lo, 1.0, 0.0) - above
    need = capf - jnp.sum(above, axis=-1, keepdims=True)
    tie_rank = jnp.dot(tie.astype(BF16), tri, preferred_element_type=F32)
    sel = above + tie * jnp.where(tie_rank <= need, 1.0, 0.0)
    slot = jnp.dot(sel.astype(BF16), tri, preferred_element_type=F32) - 1.0
    return jnp.where(sel > 0.5, slot, -1.0)


def _select_kernel(ap_ref, tp_ref, as_ref, ts_ref, pp_ref, pcp_ref, ps_ref, pcs_ref, *, caps):
    pos_p = _top_slots(ap_ref[...], tp_ref[...], caps[0])
    pos_s = _top_slots(as_ref[...], ts_ref[...], caps[1])
    pp_ref[...] = pos_p
    pcp_ref[...] = pos_p.T
    ps_ref[...] = pos_s
    pcs_ref[...] = pos_s.T


def _select(aff_p, tri_p, cap_p, aff_s, tri_s, cap_s):
    whole = lambda a: pl.BlockSpec(a.shape, lambda i: (0, 0))
    outs = [jax.ShapeDtypeStruct(aff_p.shape, F32), jax.ShapeDtypeStruct(aff_p.shape[::-1], F32),
            jax.ShapeDtypeStruct(aff_s.shape, F32), jax.ShapeDtypeStruct(aff_s.shape[::-1], F32)]
    return pl.pallas_call(
        functools.partial(_select_kernel, caps=(cap_p, cap_s)),
        grid=(1,),
        in_specs=[whole(aff_p), whole(tri_p), whole(aff_s), whole(tri_s)],
        out_specs=[whole(o) for o in outs],
        out_shape=outs,
        compiler_params=_params("arbitrary"),
        name="select",
    )(aff_p, tri_p, aff_s, tri_s)


def _gather_kernel(pos_ref, a_ref, h2_ref, xs_ref, gs_ref, oh_ref, *, cap):
    reqs, _, T = oh_ref.shape
    slot = lax.broadcasted_iota(jnp.int32, (cap, T), 0).astype(F32)
    for r in range(reqs):
        for g in range(0, N_EXPERTS, GATHER_GROUP):
            for e in range(g, g + GATHER_GROUP):
                row = r * N_EXPERTS + e
                onehot = jnp.where(pos_ref[row:row + 1, :] == slot, 1.0, 0.0)
                oh_ref[r, e * cap:(e + 1) * cap, :] = onehot.astype(BF16)
                gs_ref[e, r * cap:(r + 1) * cap, :] = jnp.sum(onehot * a_ref[row:row + 1, :], axis=-1,
                                                              keepdims=True)
            xs = jnp.dot(oh_ref[r, g * cap:(g + GATHER_GROUP) * cap, :], h2_ref[r * T:(r + 1) * T, :],
                         preferred_element_type=F32)
            xs_ref[g:g + GATHER_GROUP, r * cap:(r + 1) * cap, :] = xs.reshape(
                GATHER_GROUP, cap, D_MODEL).astype(BF16)


def _gather(pos, aff_t, h2, seq_len):
    T = seq_len
    assert ROW_TILE % T == 0
    reqs = ROW_TILE // T
    nreq = h2.shape[0] // T
    cap = EC_FACTOR * T // N_EXPERTS
    rows = pl.BlockSpec((reqs * N_EXPERTS, T), lambda b: (b, 0))
    return pl.pallas_call(
        functools.partial(_gather_kernel, cap=cap),
        grid=(nreq // reqs,),
        in_specs=[rows, rows, pl.BlockSpec((reqs * T, D_MODEL), lambda b: (b, 0))],
        out_specs=[pl.BlockSpec((N_EXPERTS, reqs * cap, D_MODEL), lambda b: (0, b, 0)),
                   pl.BlockSpec((N_EXPERTS, reqs * cap, 1), lambda b: (0, b, 0))],
        out_shape=[jax.ShapeDtypeStruct((N_EXPERTS, nreq * cap, D_MODEL), BF16),
                   jax.ShapeDtypeStruct((N_EXPERTS, nreq * cap, 1), F32)],
        scratch_shapes=[pltpu.VMEM((reqs, N_EXPERTS * cap, T), BF16)],
        compiler_params=_params("parallel"),
        name="gather",
    )(pos, aff_t, h2)


def _expert_kernel(xp_ref, xs_ref, gp_ref, gsm_ref, wg_ref, wu_ref, wd_ref, yp_ref, ys_ref,
                   wgb_ref, wub_ref, wdb_ref):
    wgb_ref[...] = wg_ref[...].astype(BF16)
    wub_ref[...] = wu_ref[...].astype(BF16)
    wdb_ref[...] = wd_ref[...].astype(BF16)
    for x_ref, g_ref, y_ref in ((xp_ref, gp_ref, yp_ref), (xs_ref, gsm_ref, ys_ref)):
        for t in range(x_ref.shape[0] // EXPERT_ROW_TILE):
            rows = slice(t * EXPERT_ROW_TILE, (t + 1) * EXPERT_ROW_TILE)
            x = x_ref[rows, :]
            hg = jnp.dot(x, wgb_ref[...], preferred_element_type=F32)
            hu = jnp.dot(x, wub_ref[...], preferred_element_type=F32)
            hid = (hg * _sigmoid(hg) * hu).astype(BF16)
            y = jnp.dot(hid, wdb_ref[...], preferred_element_type=F32)
            y_ref[rows, :] = (y * g_ref[rows, :]).astype(BF16)


def _experts(xs_p, xs_s, gs_p, gs_s, layer, w_gate, w_up, w_down):
    np_, ns_ = xs_p.shape[1], xs_s.shape[1]
    slots = lambda n, w: pl.BlockSpec((None, n, w), lambda e: (e, 0, 0))
    weight = lambda rows, cols: pl.BlockSpec((None, None, rows, cols), lambda e: (layer, e, 0, 0))
    return pl.pallas_call(
        _expert_kernel,
        grid=(N_EXPERTS,),
        in_specs=[
            slots(np_, D_MODEL), slots(ns_, D_MODEL), slots(np_, 1), slots(ns_, 1),
            weight(D_MODEL, D_EXPERT), weight(D_MODEL, D_EXPERT), weight(D_EXPERT, D_MODEL),
        ],
        out_specs=[slots(np_, D_MODEL), slots(ns_, D_MODEL)],
        out_shape=[jax.ShapeDtypeStruct(xs_p.shape, BF16), jax.ShapeDtypeStruct(xs_s.shape, BF16)],
        scratch_shapes=[pltpu.VMEM((D_MODEL, D_EXPERT), BF16), pltpu.VMEM((D_MODEL, D_EXPERT), BF16),
                        pltpu.VMEM((D_EXPERT, D_MODEL), BF16)],
        compiler_params=pltpu.CompilerParams(dimension_semantics=("parallel",),
                                             vmem_limit_bytes=EXPERT_VMEM_LIMIT),
        name="experts",
    )(xs_p, xs_s, gs_p, gs_s, w_gate, w_up, w_down)


def _combine_kernel(*refs, cap, final):
    xn_ref, pc_ref, ye_ref, mod_ref, ex_ref = refs[:5]
    if final:
        fg_ref = refs[5]
    o_ref = refs[-1]
    seq_len = pc_ref.shape[0]
    reqs = xn_ref.shape[0] // seq_len
    first_req = pl.program_id(0) * reqs
    for t in range(xn_ref.shape[0] // SUB_ROWS):
        rows = slice(t * SUB_ROWS, (t + 1) * SUB_ROWS)
        req, off = divmod(t * SUB_ROWS, seq_len)
        ye = ye_ref[:, req * cap:(req + 1) * cap, :].reshape(N_EXPERTS * cap, D_MODEL)
        expander = ex_ref[(first_req + req) % ex_ref.shape[0]]
        pos = jnp.dot(pc_ref[off:off + SUB_ROWS, :].astype(BF16), expander, preferred_element_type=F32)
        lane = lax.broadcasted_iota(jnp.int32, pos.shape, 1)
        slot = (lane % cap).astype(F32)
        onehot = jnp.where(pos == slot, 1.0, 0.0).astype(BF16)
        moe = jnp.dot(onehot, ye, preferred_element_type=F32)
        x = xn_ref[rows, :] + mod_ref[5:6, :] * moe
        if final:
            x = x * lax.rsqrt(jnp.mean(x * x, axis=-1, keepdims=True) + EPS) * fg_ref[...]
        o_ref[rows, :] = x


def _combine(xn, pos_col, ye, mods, layer, mod_row, seq_len, final_g):
    n = xn.shape[0]
    T = seq_len
    tm = ROW_TILE
    reqs = tm // T
    cap = EC_FACTOR * T // N_EXPERTS
    final = final_g is not None
    group = LANES // N_EXPERTS
    assert tm % T == 0 and T % SUB_ROWS == 0 and group % reqs == 0
    src = jnp.arange(group)[:, None, None] * N_EXPERTS + (jnp.arange(N_EXPERTS * cap) // cap)[None, None, :]
    expander = (jnp.arange(LANES)[None, :, None] == src).astype(BF16)
    in_specs = [
        pl.BlockSpec((tm, D_MODEL), lambda i: (i, 0)),
        pl.BlockSpec((T, LANES), lambda i: (0, i * reqs // group)),
        pl.BlockSpec((N_EXPERTS, reqs * cap, D_MODEL), lambda i: (0, i, 0)),
        pl.BlockSpec((None, None, 6, D_MODEL), lambda i: (layer, mod_row(i), 0, 0)),
        pl.BlockSpec((group, LANES, N_EXPERTS * cap), lambda i: (0, 0, 0)),
    ]
    args = [xn, pos_col, ye, mods, expander]
    if final:
        in_specs.append(pl.BlockSpec((1, D_MODEL), lambda i: (0, 0)))
        args.append(final_g)
    return pl.pallas_call(
        functools.partial(_combine_kernel, cap=cap, final=final),
        grid=(n // tm,),
        in_specs=in_specs,
        out_specs=pl.BlockSpec((tm, D_MODEL), lambda i: (i, 0)),
        out_shape=jax.ShapeDtypeStruct((n, D_MODEL), F32),
        compiler_params=_params("parallel"),
        name="combine",
    )(*args)


def _rope_tables(T):
    rows = T // GRID_W
    row = jnp.repeat(jnp.arange(rows, dtype=F32), GRID_W)
    col = jnp.tile(jnp.arange(GRID_W, dtype=F32), rows)
    inv = ROPE_BASE ** (-jnp.arange(0, ROPE_AXIS, 2, dtype=F32) / ROPE_AXIS)
    ar = row[:, None] * inv
    ac = col[:, None] * inv
    cos = jnp.concatenate([jnp.cos(ar), jnp.cos(ar), jnp.cos(ac), jnp.cos(ac)], axis=-1)
    sin = jnp.concatenate([-jnp.sin(ar), jnp.sin(ar), -jnp.sin(ac), jnp.sin(ac)], axis=-1)
    reps = D_QK // HEAD_DIM
    return jnp.tile(cos, (1, reps)), jnp.tile(sin, (1, reps))


def kernel(x_prompt, x_sample, c, cache_k, cache_v, c_ctx, norm1_g, w_mod, b_mod, w_in, conv_w, conv_b,
           conv_ln_g, conv_ln_b, lambda_q1, lambda_k1, lambda_q2, lambda_k2, subln_g, w_out, norm2_g,
           w_router, w_gate, w_up, w_down, final_norm_g):
    batch, seq, _ = x_prompt.shape
    dec_batch, dec_seq, _ = x_sample.shape
    past = cache_k.shape[2]
    ctx_k = cache_k.reshape(dec_batch, DEPTH, past, D_QK)
    assert 1 + dec_batch <= MOD_ROWS

    cvec = jnp.concatenate([c_ctx[None, :], c, jnp.zeros((MOD_ROWS - 1 - dec_batch, D_MODEL), F32)], axis=0)
    mods = _modulation(cvec, w_mod, b_mod)

    w_router_b = jnp.pad(w_router, ((0, 0), (0, 0), (0, LANES - N_EXPERTS))).astype(BF16)
    lam_vecs = jnp.stack([lambda_q1, lambda_k1, lambda_q2, lambda_k2], axis=1)
    vec = lambda a: a.reshape(DEPTH, 1, a.shape[-1])
    norm1_v, norm2_v, subln_v = vec(norm1_g), vec(norm2_g), vec(subln_g)
    conv_b_v, ln_g_v, ln_b_v = vec(conv_b), vec(conv_ln_g), vec(conv_ln_b)
    rope_tabs = _rope_tables(dec_seq)
    final_g = final_norm_g.reshape(1, D_MODEL)
    tri_p = (jnp.arange(seq)[:, None] <= jnp.arange(seq)[None, :]).astype(BF16)
    tri_s = (jnp.arange(dec_seq)[:, None] <= jnp.arange(dec_seq)[None, :]).astype(BF16)

    prompt_row = lambda i: 0
    assert dec_seq % ROW_TILE == 0
    tiles_per_sample = dec_seq // ROW_TILE
    sample_row = lambda i: 1 + i // tiles_per_sample

    xp = x_prompt.reshape(batch * seq, D_MODEL)
    xs = x_sample.reshape(dec_batch * dec_seq, D_MODEL)
    caches = (jnp.zeros((batch, DEPTH, D_QK, seq), F32), jnp.zeros((batch, DEPTH, seq, N_HEADS, V_DIM), F32))
    for l in range(DEPTH):
        last = l == DEPTH - 1
        up, qp, kp, vp, *caches = _inproj(xp, mods, l, norm1_v, w_in, prompt_row, None, seq, caches)
        us, qs, ks, vs = _inproj(xs, mods, l, norm1_v, w_in, sample_row, rope_tabs, dec_seq)
        cp = _conv(up, l, conv_w, conv_b_v, ln_g_v, ln_b_v)
        cs = _conv(us, l, conv_w, conv_b_v, ln_g_v, ln_b_v)
        op = _attention(qp, kp, vp, lam_vecs, subln_v, l, seq, None)
        os_ = _attention(qs, ks, vs, lam_vecs, subln_v, l, dec_seq, (ctx_k, cache_v))
        xnp, h2p, affp = _outproj(xp, cp, op, mods, l, w_out, norm2_v, w_router_b, prompt_row, seq)
        xns, h2s, affs = _outproj(xs, cs, os_, mods, l, w_out, norm2_v, w_router_b, sample_row, dec_seq)
        pos_p, rank_p, pos_s, rank_s = _select(affp, tri_p, EC_FACTOR * seq // N_EXPERTS,
                                               affs, tri_s, EC_FACTOR * dec_seq // N_EXPERTS)
        slots_p, gates_p = _gather(pos_p, affp, h2p, seq)
        slots_s, gates_s = _gather(pos_s, affs, h2s, dec_seq)
        ye_p, ye_s = _experts(slots_p, slots_s, gates_p, gates_s, l, w_gate, w_up, w_down)
        xp = _combine(xnp, rank_p, ye_p, mods, l, prompt_row, seq, final_g if last else None)
        xs = _combine(xns, rank_s, ye_s, mods, l, sample_row, dec_seq, final_g if last else None)

    y_prompt = xp.reshape(batch, seq, D_MODEL)
    y_sample = xs.reshape(dec_batch, dec_seq, D_MODEL)
    keys_t, new_v = caches
    new_k = keys_t.reshape(batch, DEPTH, N_HEADS, 2, HEAD_DIM, seq).transpose(0, 1, 5, 2, 3, 4)
    return (y_prompt, y_sample, new_k, new_v)
```

```python
import functools
import math

import jax
import jax.numpy as jnp
from jax import lax
from jax.experimental import pallas as pl
from jax.experimental.pallas import tpu as pltpu

D_MODEL = 1024
DEPTH = 4
GRID_W = 64
D_CONV = 512
CONV_WIDTH = 31
N_HEADS = 4
HEAD_DIM = 64
V_DIM = 2 * HEAD_DIM
D_QK = N_HEADS * 2 * HEAD_DIM
D_ATTN = N_HEADS * V_DIM
D_MIX = D_CONV + D_ATTN
D_IN = 2 * D_CONV + 2 * D_QK + D_ATTN
N_EXPERTS = 16
EC_FACTOR = 2
D_EXPERT = 1024
ROPE_BASE = 10000.0
ROPE_AXIS = HEAD_DIM // 2
EPS = 1e-6
LOG2_E = math.log2(math.e)

LANES = 128
MOD_ROWS = 16
TOKEN_TILE = 1024
ROW_TILE = 1024
SUB_ROWS = 256
CONV_CHUNK = 128
NORM_CHUNK = 512
CONV_PAD = 16
EXPERT_ROW_TILE = 256
GATHER_GROUP = 4
MANTISSA_BITS = 23
VMEM_LIMIT = 56 * 1024 * 1024
EXPERT_VMEM_LIMIT = 60 * 1024 * 1024

F32 = jnp.float32
BF16 = jnp.bfloat16


def _sigmoid(x):
    return 1.0 / (1.0 + jnp.exp(-x))


def _params(*sem):
    return pltpu.CompilerParams(dimension_semantics=sem, vmem_limit_bytes=VMEM_LIMIT)


def _mod_kernel(c_ref, w_ref, b_ref, o_ref):
    c = c_ref[...]
    s = c * _sigmoid(c)
    o_ref[...] = jnp.dot(s.astype(BF16), w_ref[...].astype(BF16),
                         preferred_element_type=F32) + b_ref[...]


def _modulation(cvec, w_mod, b_mod):
    tn = 1536
    out = pl.pallas_call(
        _mod_kernel,
        grid=(DEPTH, 6 * D_MODEL // tn),
        in_specs=[
            pl.BlockSpec((MOD_ROWS, D_MODEL), lambda l, j: (0, 0)),
            pl.BlockSpec((None, D_MODEL, tn), lambda l, j: (l, 0, j)),
            pl.BlockSpec((None, 1, tn), lambda l, j: (l, 0, j)),
        ],
        out_specs=pl.BlockSpec((None, MOD_ROWS, tn), lambda l, j: (l, 0, j)),
        out_shape=jax.ShapeDtypeStruct((DEPTH, MOD_ROWS, 6 * D_MODEL), F32),
        compiler_params=_params("parallel", "parallel"),
        name="modulation",
    )(cvec, w_mod, b_mod.reshape(DEPTH, 1, 6 * D_MODEL))
    return out.reshape(DEPTH, MOD_ROWS, 6, D_MODEL)


def _inproj_kernel(*refs, rope, cache_out, seq_len):
    x_ref, mod_ref, g_ref, w_ref = refs[:4]
    refs = refs[4:]
    if rope:
        cos_ref, sin_ref = refs[:2]
        refs = refs[2:]
    if cache_out:
        refs = refs[2:]
    u_ref, q_ref, k_ref, v_ref = refs[:4]
    wb_ref = refs[-1]

    @pl.when(pl.program_id(0) == 0)
    def _():
        wb_ref[...] = w_ref[...].astype(BF16)

    for r in range(u_ref.shape[0]):
        u_ref[r, 0:CONV_PAD, :] = jnp.zeros((CONV_PAD, D_CONV), F32)
        u_ref[r, CONV_PAD + seq_len:, :] = jnp.zeros((CONV_PAD, D_CONV), F32)

    for t in range(x_ref.shape[0] // SUB_ROWS):
        rows = slice(t * SUB_ROWS, (t + 1) * SUB_ROWS)
        req, off = divmod(t * SUB_ROWS, seq_len)
        x = x_ref[rows, :]
        ms = jnp.mean(x * x, axis=-1, keepdims=True)
        y = x * lax.rsqrt(ms + EPS) * g_ref[...]
        h = y * (1.0 + mod_ref[1:2, :]) + mod_ref[0:1, :]
        z = jnp.dot(h.astype(BF16), wb_ref[...], preferred_element_type=F32)
        u_ref[req, CONV_PAD + off:CONV_PAD + off + SUB_ROWS, :] = (
            z[:, :D_CONV] * _sigmoid(z[:, D_CONV:2 * D_CONV]))
        q = z[:, 2 * D_CONV:2 * D_CONV + D_QK]
        k = z[:, 2 * D_CONV + D_QK:2 * D_CONV + 2 * D_QK]
        v = z[:, 2 * D_CONV + 2 * D_QK:]
        if cache_out:
            kf_ref, vf_ref = refs[4:6]
            kf_ref[req, :, off:off + SUB_ROWS] = k.T
            for hd in range(N_HEADS):
                vf_ref[req, off:off + SUB_ROWS, hd, :] = v[:, hd * V_DIM:(hd + 1) * V_DIM]
        if rope:
            lane = lax.broadcasted_iota(jnp.int32, q.shape, 1)
            first = (lane % ROPE_AXIS) < (ROPE_AXIS // 2)
            cos = cos_ref[rows, :]
            sin = sin_ref[rows, :]

            def rot(a):
                partner = jnp.where(first, pltpu.roll(a, D_QK - ROPE_AXIS // 2, 1),
                                    pltpu.roll(a, ROPE_AXIS // 2, 1))
                return a * cos + partner * sin

            q = rot(q)
            k = rot(k)
        q_ref[rows, :] = (q * (HEAD_DIM ** -0.5 * LOG2_E)).astype(BF16)
        k_ref[rows, :] = k.astype(BF16)
        v_ref[rows, :] = v.astype(BF16)


def _inproj(x, mods, layer, norm_g, w_in, mod_row, rope_tabs, seq_len, caches=None):
    n = x.shape[0]
    tm = ROW_TILE
    assert rope_tabs is None or seq_len % tm == 0
    tiles_per_seq = max(1, seq_len // tm)
    rope = rope_tabs is not None
    cache_out = caches is not None
    in_specs = [
        pl.BlockSpec((tm, D_MODEL), lambda i: (i, 0)),
        pl.BlockSpec((None, None, 6, D_MODEL), lambda i: (layer, mod_row(i), 0, 0)),
        pl.BlockSpec((None, 1, D_MODEL), lambda i: (layer, 0, 0)),
        pl.BlockSpec((None, D_MODEL, D_IN), lambda i: (layer, 0, 0), pipeline_mode=pl.Buffered(1)),
    ]
    args = [x, mods, norm_g, w_in]
    if rope:
        in_specs += [pl.BlockSpec((tm, D_QK), lambda i: (i % tiles_per_seq, 0))] * 2
        args += list(rope_tabs)
    assert tm % seq_len == 0 and seq_len % SUB_ROWS == 0
    reqs = tm // seq_len
    padded = seq_len + 2 * CONV_PAD
    half = pl.BlockSpec((tm, D_CONV), lambda i: (i, 0))
    out_specs = [pl.BlockSpec((reqs, padded, D_CONV), lambda i: (i, 0, 0)), half, half, half]
    out_shape = ([jax.ShapeDtypeStruct((n // seq_len, padded, D_CONV), F32)]
                 + [jax.ShapeDtypeStruct((n, D_QK), BF16)] * 3)
    aliases = {}
    if cache_out:
        out_specs += [pl.BlockSpec((reqs, None, D_QK, seq_len), lambda i: (i, layer, 0, 0)),
                      pl.BlockSpec((reqs, None, seq_len, N_HEADS, V_DIM), lambda i: (i, layer, 0, 0, 0))]
        out_shape += [jax.ShapeDtypeStruct(c.shape, c.dtype) for c in caches]
        aliases = {len(args): len(out_shape) - 2, len(args) + 1: len(out_shape) - 1}
        in_specs += [pl.BlockSpec(memory_space=pl.ANY)] * 2
        args += list(caches)
    return pl.pallas_call(
        functools.partial(_inproj_kernel, rope=rope, cache_out=cache_out, seq_len=seq_len),
        grid=(n // tm,),
        in_specs=in_specs,
        out_specs=out_specs,
        out_shape=out_shape,
        input_output_aliases=aliases,
        scratch_shapes=[pltpu.VMEM((D_MODEL, D_IN), BF16)],
        compiler_params=_params("arbitrary"),
        name="inproj",
    )(*args)


def _conv_stage(pad_ref, w_ref, sh_ref, wb_ref):
    for k in range(CONV_WIDTH):
        wb_ref[k] = jnp.broadcast_to(w_ref[k:k + 1, :], (8, D_CONV))
    rows = sh_ref.shape[1]
    for r in range(1, 8):
        sh_ref[r - 1] = pad_ref[r:r + rows, :]


def _conv_block(pad_ref, sh_ref, w, base, lanes):
    first = CONV_PAD - CONV_WIDTH // 2
    groups = CONV_CHUNK // 8
    acc = jnp.zeros((groups, 8, LANES), F32)
    for r in range(8):
        taps = [(k, (first + k) // 8) for k in range(CONV_WIDTH) if (first + k) % 8 == r]
        m0 = min(m for _, m in taps)
        span = max(m for _, m in taps) - m0 + groups
        window = pl.ds(base + 8 * m0, 8 * span)
        x = (pad_ref[window, lanes] if r == 0 else sh_ref[r - 1, window, lanes]).reshape(span, 8, LANES)
        for k, m in taps:
            acc = acc + w[k] * x[m - m0:m - m0 + groups]
    return acc.reshape(CONV_CHUNK, LANES)


def _conv_norm(acc, b_ref, lg_ref, lb_ref):
    acc = acc + b_ref[...]
    mu = jnp.mean(acc, axis=-1, keepdims=True)
    d = acc - mu
    var = jnp.mean(d * d, axis=-1, keepdims=True)
    un = d * lax.rsqrt(var + EPS) * lg_ref[...] + lb_ref[...]
    return (un * _sigmoid(un)).astype(BF16)


def _conv_kernel(pad_ref, w_ref, b_ref, lg_ref, lb_ref, o_ref, sh_ref, wb_ref, acc_ref):
    T = o_ref.shape[0]
    _conv_stage(pad_ref, w_ref, sh_ref, wb_ref)

    for q in range(D_CONV // LANES):
        lanes = slice(q * LANES, (q + 1) * LANES)
        w = [wb_ref[k, :, lanes] for k in range(CONV_WIDTH)]

        def taps_chunk(i, carry, lanes=lanes, w=w):
            base = pl.multiple_of(i * CONV_CHUNK, CONV_CHUNK)
            acc_ref[pl.ds(base, CONV_CHUNK), lanes] = _conv_block(pad_ref, sh_ref, w, base, lanes)
            return carry

        lax.fori_loop(0, T // CONV_CHUNK, taps_chunk, 0)

    rows = min(NORM_CHUNK, T)

    def norm_chunk(i, carry):
        base = pl.multiple_of(i * rows, rows)
        o_ref[pl.ds(base, rows), :] = _conv_norm(acc_ref[pl.ds(base, rows), :], b_ref, lg_ref, lb_ref)
        return carry

    lax.fori_loop(0, T // rows, norm_chunk, 0)


def _conv(u_pad, layer, conv_w, conv_b, ln_g, ln_b):
    nreq, padded, _ = u_pad.shape
    T = padded - 2 * CONV_PAD
    rows = padded - 8
    vec = pl.BlockSpec((None, 1, D_CONV), lambda b: (layer, 0, 0))
    return pl.pallas_call(
        _conv_kernel,
        grid=(nreq,),
        in_specs=[
            pl.BlockSpec((None, padded, D_CONV), lambda b: (b, 0, 0)),
            pl.BlockSpec((None, CONV_WIDTH, D_CONV), lambda b: (layer, 0, 0)),
            vec, vec, vec,
        ],
        out_specs=pl.BlockSpec((T, D_CONV), lambda b: (b, 0)),
        out_shape=jax.ShapeDtypeStruct((nreq * T, D_CONV), BF16),
        scratch_shapes=[pltpu.VMEM((7, rows, D_CONV), F32),
                        pltpu.VMEM((CONV_WIDTH, 8, D_CONV), F32),
                        pltpu.VMEM((T, D_CONV), F32)],
        compiler_params=_params("parallel"),
        name="conv",
    )(u_pad, conv_w, conv_b, ln_g, ln_b)


def _attn_kernel(*refs, lam_init, has_ctx, seq_len):
    lam_ref, sg_ref, q_ref, k_ref, v_ref = refs[:5]
    T = seq_len
    reqs = k_ref.shape[0] // T
    if has_ctx:
        ck_ref, cv_ref, o_ref, vals_ref, keys_ref = refs[5:]
    else:
        o_ref, vals_ref = refs[5:]

    @pl.when(pl.program_id(1) == 0)
    def _():
        vals_ref[...] = jnp.ones(vals_ref.shape, BF16)
        for r in range(reqs):
            for h in range(N_HEADS):
                vals_ref[r, 0:T, 2 * h * V_DIM:(2 * h + 1) * V_DIM] = v_ref[r * T:(r + 1) * T,
                                                                            h * V_DIM:(h + 1) * V_DIM]
        if has_ctx:
            keys_ref[0:T, :] = k_ref[...]
            keys_ref[T:, :] = ck_ref[...].astype(BF16)
            for h in range(N_HEADS):
                vals_ref[0, T:, 2 * h * V_DIM:(2 * h + 1) * V_DIM] = cv_ref[:, h, :].astype(BF16)

    l = lam_ref[...]
    lam = (jnp.exp(jnp.sum(l[0:1, :] * l[1:2, :], axis=-1, keepdims=True))
           - jnp.exp(jnp.sum(l[2:3, :] * l[3:4, :], axis=-1, keepdims=True)) + lam_init)
    tq = q_ref.shape[0] // reqs
    lane = lax.broadcasted_iota(jnp.int32, (tq, V_DIM), 1)
    nt = (((1,), (1,)), ((), ()))
    for r in range(reqs):
        q_rows = slice(r * tq, (r + 1) * tq)
        for h in range(N_HEADS):
            sl = slice(h * V_DIM, (h + 1) * V_DIM)
            qh = q_ref[q_rows, sl]
            kh = keys_ref[:, sl] if has_ctx else k_ref[r * T:(r + 1) * T, sl]
            heads = []
            for c in range(2):
                keep = (lane < HEAD_DIM) if c == 0 else (lane >= HEAD_DIM)
                qc = jnp.where(keep, qh, jnp.zeros_like(qh))
                s = lax.dot_general(qc, kh, nt, preferred_element_type=F32)
                e = jnp.exp2(s - jnp.max(s, axis=-1, keepdims=True)).astype(BF16)
                pv = jnp.dot(e, vals_ref[r, :, 2 * h * V_DIM:(2 * h + 2) * V_DIM],
                             preferred_element_type=F32)
                heads.append(pv[:, :V_DIM] / pv[:, V_DIM:])
            o = heads[0] - lam * heads[1]
            o = o * lax.rsqrt(jnp.mean(o * o, axis=-1, keepdims=True) + EPS)
            o = o * sg_ref[...] * (1.0 - lam_init)
            o_ref[q_rows, sl] = o.astype(BF16)


def _attention(q, k, v, lam_vecs, subln_g, layer, seq_len, ctx):
    n = q.shape[0]
    T = seq_len
    has_ctx = ctx is not None
    tq = min(TOKEN_TILE, T)
    nq = T // tq
    reqs = 1 if has_ctx else max(1, TOKEN_TILE // T)
    lam_init = 0.8 - 0.6 * math.exp(-0.3 * layer)
    kv = pl.BlockSpec((reqs * T, D_QK), lambda b, i: (b, 0))
    tile = pl.BlockSpec((reqs * tq, D_QK), lambda b, i: (b * nq + i, 0))
    in_specs = [
        pl.BlockSpec((None, 4, HEAD_DIM), lambda b, i: (layer, 0, 0)),
        pl.BlockSpec((None, 1, V_DIM), lambda b, i: (layer, 0, 0)),
        tile, kv, kv,
    ]
    args = [lam_vecs, subln_g, q, k, v]
    past = ctx[0].shape[2] if has_ctx else 0
    scratch = [pltpu.VMEM((reqs, T + past, 2 * D_ATTN), BF16)]
    if has_ctx:
        in_specs += [pl.BlockSpec((None, None, past, D_QK), lambda b, i: (b, layer, 0, 0)),
                     pl.BlockSpec((None, None, past, N_HEADS, V_DIM), lambda b, i: (b, layer, 0, 0, 0))]
        args += list(ctx)
        scratch.append(pltpu.VMEM((T + past, D_QK), BF16))
    return pl.pallas_call(
        functools.partial(_attn_kernel, lam_init=lam_init, has_ctx=has_ctx, seq_len=T),
        grid=(n // (reqs * T), nq),
        in_specs=in_specs,
        out_specs=tile,
        out_shape=jax.ShapeDtypeStruct((n, D_ATTN), BF16),
        scratch_shapes=scratch,
        compiler_params=_params("parallel", "arbitrary"),
        name="attention",
    )(*args)


def _outproj_kernel(x_ref, mc_ref, ma_ref, mod_ref, w_ref, g_ref, wr_ref, xn_ref, h2_ref, aff_ref, wb_ref,
                    *, seq_len):
    @pl.when(pl.program_id(0) == 0)
    def _():
        wb_ref[...] = w_ref[...].astype(BF16)

    for t in range(x_ref.shape[0] // SUB_ROWS):
        rows = slice(t * SUB_ROWS, (t + 1) * SUB_ROWS)
        req, off = divmod(t * SUB_ROWS, seq_len)
        mix = jnp.concatenate([mc_ref[rows, :], ma_ref[rows, :]], axis=-1)
        acc = jnp.dot(mix, wb_ref[...], preferred_element_type=F32)
        x = x_ref[rows, :] + mod_ref[2:3, :] * acc
        xn_ref[rows, :] = x
        ms = jnp.mean(x * x, axis=-1, keepdims=True)
        y = x * lax.rsqrt(ms + EPS) * g_ref[...]
        h2 = (y * (1.0 + mod_ref[4:5, :]) + mod_ref[3:4, :]).astype(BF16)
        h2_ref[rows, :] = h2
        logits = jnp.dot(h2, wr_ref[...], preferred_element_type=F32)
        lane = lax.broadcasted_iota(jnp.int32, logits.shape, 1)
        logits = jnp.where(lane < N_EXPERTS, logits, -jnp.inf)
        e = jnp.exp(logits - jnp.max(logits, axis=-1, keepdims=True))
        aff = e / jnp.sum(e, axis=-1, keepdims=True)
        aff_ref[req * N_EXPERTS:(req + 1) * N_EXPERTS, off:off + SUB_ROWS] = aff.T[0:N_EXPERTS, :]


def _outproj(x, mix_conv, mix_attn, mods, layer, w_out, norm_g, w_router_b, mod_row, seq_len):
    n = x.shape[0]
    tm = ROW_TILE
    assert tm % seq_len == 0 and seq_len % SUB_ROWS == 0
    reqs = tm // seq_len
    row = lambda i: (i, 0)
    return pl.pallas_call(
        functools.partial(_outproj_kernel, seq_len=seq_len),
        grid=(n // tm,),
        in_specs=[
            pl.BlockSpec((tm, D_MODEL), row),
            pl.BlockSpec((tm, D_CONV), row),
            pl.BlockSpec((tm, D_ATTN), row),
            pl.BlockSpec((None, None, 6, D_MODEL), lambda i: (layer, mod_row(i), 0, 0)),
            pl.BlockSpec((None, D_MIX, D_MODEL), lambda i: (layer, 0, 0), pipeline_mode=pl.Buffered(1)),
            pl.BlockSpec((None, 1, D_MODEL), lambda i: (layer, 0, 0)),
            pl.BlockSpec((None, D_MODEL, LANES), lambda i: (layer, 0, 0)),
        ],
        out_specs=[pl.BlockSpec((tm, D_MODEL), row), pl.BlockSpec((tm, D_MODEL), row),
                   pl.BlockSpec((reqs * N_EXPERTS, seq_len), row)],
        out_shape=[jax.ShapeDtypeStruct((n, D_MODEL), F32), jax.ShapeDtypeStruct((n, D_MODEL), BF16),
                   jax.ShapeDtypeStruct((n // seq_len * N_EXPERTS, seq_len), F32)],
        scratch_shapes=[pltpu.VMEM((D_MIX, D_MODEL), BF16)],
        compiler_params=_params("arbitrary"),
        name="outproj",
    )(x, mix_conv, mix_attn, mods, w_out, norm_g, w_router_b)


def _top_slots(a, tri, cap):
    rows = a.shape[0]
    capf = float(cap)

    def count_ge(v):
        return jnp.sum(jnp.where(a >= v, 1.0, 0.0), axis=-1, keepdims=True)

    hi = jnp.full((rows, 1), 2.0, F32)
    for j in range(6, -1, -1):
        cand = hi * (2.0 ** -(2 ** j))
        hi = jnp.where(count_ge(cand) < capf, cand, hi)
    lo = hi * 0.5
    step = lo * 0.5
    for _ in range(MANTISSA_BITS):
        cand = lo + step
        lo = jnp.where(count_ge(cand) >= capf, cand, lo)
        step = step * 0.5
    upper = jnp.where(step > 0.0, lo + step * 2.0, hi)
    above = jnp.where(a >= upper, 1.0, 0.0)
    tie = jnp.where(a >= lo, 1.0, 0.0) - above
    need = capf - jnp.sum(above, axis=-1, keepdims=True)
    tie_rank = jnp.dot(tie.astype(BF16), tri, preferred_element_type=F32)
    sel = above + tie * jnp.where(tie_rank <= need, 1.0, 0.0)
    slot = jnp.dot(sel.astype(BF16), tri, preferred_element_type=F32) - 1.0
    return jnp.where(sel > 0.5, slot, -1.0)


def _select_kernel(ap_ref, tp_ref, as_ref, ts_ref, pp_ref, ps_ref, *, caps):
    pp_ref[...] = _top_slots(ap_ref[...], tp_ref[...], caps[0])
    ps_ref[...] = _top_slots(as_ref[...], ts_ref[...], caps[1])


def _select(aff_p, tri_p, cap_p, aff_s, tri_s, cap_s):
    whole = lambda a: pl.BlockSpec(a.shape, lambda i: (0, 0))
    outs = [jax.ShapeDtypeStruct(aff_p.shape, F32), jax.ShapeDtypeStruct(aff_s.shape, F32)]
    return pl.pallas_call(
        functools.partial(_select_kernel, caps=(cap_p, cap_s)),
        grid=(1,),
        in_specs=[whole(aff_p), whole(tri_p), whole(aff_s), whole(tri_s)],
        out_specs=[whole(o) for o in outs],
        out_shape=outs,
        compiler_params=_params("arbitrary"),
        name="select",
    )(aff_p, tri_p, aff_s, tri_s)


def _gather_kernel(pos_ref, a_ref, h2_ref, xs_ref, gs_ref, oh_ref, *, cap):
    reqs, _, T = oh_ref.shape
    slot = lax.broadcasted_iota(jnp.int32, (cap, T), 0).astype(F32)
    for r in range(reqs):
        for g in range(0, N_EXPERTS, GATHER_GROUP):
            for e in range(g, g + GATHER_GROUP):
                row = r * N_EXPERTS + e
                onehot = jnp.where(pos_ref[row:row + 1, :] == slot, 1.0, 0.0)
                oh_ref[r, e * cap:(e + 1) * cap, :] = onehot.astype(BF16)
                gs_ref[e, r * cap:(r + 1) * cap, :] = jnp.sum(onehot * a_ref[row:row + 1, :], axis=-1,
                                                              keepdims=True)
            xs = jnp.dot(oh_ref[r, g * cap:(g + GATHER_GROUP) * cap, :], h2_ref[r * T:(r + 1) * T, :],
                         preferred_element_type=F32)
            xs_ref[g:g + GATHER_GROUP, r * cap:(r + 1) * cap, :] = xs.reshape(
                GATHER_GROUP, cap, D_MODEL).astype(BF16)


def _gather(pos, aff_t, h2, seq_len):
    T = seq_len
    assert ROW_TILE % T == 0
    reqs = ROW_TILE // T
    nreq = h2.shape[0] // T
    cap = EC_FACTOR * T // N_EXPERTS
    rows = pl.BlockSpec((reqs * N_EXPERTS, T), lambda b: (b, 0))
    return pl.pallas_call(
        functools.partial(_gather_kernel, cap=cap),
        grid=(nreq // reqs,),
        in_specs=[rows, rows, pl.BlockSpec((reqs * T, D_MODEL), lambda b: (b, 0))],
        out_specs=[pl.BlockSpec((N_EXPERTS, reqs * cap, D_MODEL), lambda b: (0, b, 0)),
                   pl.BlockSpec((N_EXPERTS, reqs * cap, 1), lambda b: (0, b, 0))],
        out_shape=[jax.ShapeDtypeStruct((N_EXPERTS, nreq * cap, D_MODEL), BF16),
                   jax.ShapeDtypeStruct((N_EXPERTS, nreq * cap, 1), F32)],
        scratch_shapes=[pltpu.VMEM((reqs, N_EXPERTS * cap, T), BF16)],
        compiler_params=_params("parallel"),
        name="gather",
    )(pos, aff_t, h2)


def _expert_kernel(xp_ref, xs_ref, gp_ref, gsm_ref, wg_ref, wu_ref, wd_ref, yp_ref, ys_ref,
                   wgb_ref, wub_ref, wdb_ref):
    wgb_ref[...] = wg_ref[...].astype(BF16)
    wub_ref[...] = wu_ref[...].astype(BF16)
    wdb_ref[...] = wd_ref[...].astype(BF16)
    for x_ref, g_ref, y_ref in ((xp_ref, gp_ref, yp_ref), (xs_ref, gsm_ref, ys_ref)):
        for t in range(x_ref.shape[0] // EXPERT_ROW_TILE):
            rows = slice(t * EXPERT_ROW_TILE, (t + 1) * EXPERT_ROW_TILE)
            x = x_ref[rows, :]
            hg = jnp.dot(x, wgb_ref[...], preferred_element_type=F32)
            hu = jnp.dot(x, wub_ref[...], preferred_element_type=F32)
            hid = (hg * _sigmoid(hg) * hu).astype(BF16)
            y = jnp.dot(hid, wdb_ref[...], preferred_element_type=F32)
            y_ref[rows, :] = (y * g_ref[rows, :]).astype(BF16)


def _experts(xs_p, xs_s, gs_p, gs_s, layer, w_gate, w_up, w_down):
    np_, ns_ = xs_p.shape[1], xs_s.shape[1]
    slots = lambda n, w: pl.BlockSpec((None, n, w), lambda e: (e, 0, 0))
    weight = lambda rows, cols: pl.BlockSpec((None, None, rows, cols), lambda e: (layer, e, 0, 0))
    return pl.pallas_call(
        _expert_kernel,
        grid=(N_EXPERTS,),
        in_specs=[
            slots(np_, D_MODEL), slots(ns_, D_MODEL), slots(np_, 1), slots(ns_, 1),
            weight(D_MODEL, D_EXPERT), weight(D_MODEL, D_EXPERT), weight(D_EXPERT, D_MODEL),
        ],
        out_specs=[slots(np_, D_MODEL), slots(ns_, D_MODEL)],
        out_shape=[jax.ShapeDtypeStruct(xs_p.shape, BF16), jax.ShapeDtypeStruct(xs_s.shape, BF16)],
        scratch_shapes=[pltpu.VMEM((D_MODEL, D_EXPERT), BF16), pltpu.VMEM((D_MODEL, D_EXPERT), BF16),
                        pltpu.VMEM((D_EXPERT, D_MODEL), BF16)],
        compiler_params=pltpu.CompilerParams(dimension_semantics=("parallel",),
                                             vmem_limit_bytes=EXPERT_VMEM_LIMIT),
        name="experts",
    )(xs_p, xs_s, gs_p, gs_s, w_gate, w_up, w_down)


def _combine_kernel(*refs, cap, final):
    xn_ref, pos_ref, ye_ref, mod_ref = refs[:4]
    if final:
        fg_ref = refs[4]
    o_ref = refs[-1]
    seq_len = pos_ref.shape[1]
    slot = lax.broadcasted_iota(jnp.int32, (cap, SUB_ROWS), 0).astype(F32)
    for t in range(xn_ref.shape[0] // SUB_ROWS):
        rows = slice(t * SUB_ROWS, (t + 1) * SUB_ROWS)
        req, off = divmod(t * SUB_ROWS, seq_len)
        ye = ye_ref[:, req * cap:(req + 1) * cap, :].reshape(N_EXPERTS * cap, D_MODEL)
        onehot = jnp.concatenate(
            [jnp.where(pos_ref[req * N_EXPERTS + e:req * N_EXPERTS + e + 1, off:off + SUB_ROWS] == slot,
                       1.0, 0.0).astype(BF16) for e in range(N_EXPERTS)], axis=0)
        moe = lax.dot_general(onehot, ye, (((0,), (0,)), ((), ())), preferred_element_type=F32)
        x = xn_ref[rows, :] + mod_ref[5:6, :] * moe
        if final:
            x = x * lax.rsqrt(jnp.mean(x * x, axis=-1, keepdims=True) + EPS) * fg_ref[...]
        o_ref[rows, :] = x


def _combine(xn, pos, ye, mods, layer, mod_row, seq_len, final_g):
    n = xn.shape[0]
    T = seq_len
    tm = ROW_TILE
    reqs = tm // T
    cap = EC_FACTOR * T // N_EXPERTS
    final = final_g is not None
    assert tm % T == 0 and T % SUB_ROWS == 0
    in_specs = [
        pl.BlockSpec((tm, D_MODEL), lambda i: (i, 0)),
        pl.BlockSpec((reqs * N_EXPERTS, T), lambda i: (i, 0)),
        pl.BlockSpec((N_EXPERTS, reqs * cap, D_MODEL), lambda i: (0, i, 0)),
        pl.BlockSpec((None, None, 6, D_MODEL), lambda i: (layer, mod_row(i), 0, 0)),
    ]
    args = [xn, pos, ye, mods]
    if final:
        in_specs.append(pl.BlockSpec((1, D_MODEL), lambda i: (0, 0)))
        args.append(final_g)
    return pl.pallas_call(
        functools.partial(_combine_kernel, cap=cap, final=final),
        grid=(n // tm,),
        in_specs=in_specs,
        out_specs=pl.BlockSpec((tm, D_MODEL), lambda i: (i, 0)),
        out_shape=jax.ShapeDtypeStruct((n, D_MODEL), F32),
        compiler_params=_params("parallel"),
        name="combine",
    )(*args)


def _rope_tables(T):
    rows = T // GRID_W
    row = jnp.repeat(jnp.arange(rows, dtype=F32), GRID_W)
    col = jnp.tile(jnp.arange(GRID_W, dtype=F32), rows)
    inv = ROPE_BASE ** (-jnp.arange(0, ROPE_AXIS, 2, dtype=F32) / ROPE_AXIS)
    ar = row[:, None] * inv
    ac = col[:, None] * inv
    cos = jnp.concatenate([jnp.cos(ar), jnp.cos(ar), jnp.cos(ac), jnp.cos(ac)], axis=-1)
    sin = jnp.concatenate([-jnp.sin(ar), jnp.sin(ar), -jnp.sin(ac), jnp.sin(ac)], axis=-1)
    reps = D_QK // HEAD_DIM
    return jnp.tile(cos, (1, reps)), jnp.tile(sin, (1, reps))


def kernel(x_prompt, x_sample, c, cache_k, cache_v, c_ctx, norm1_g, w_mod, b_mod, w_in, conv_w, conv_b,
           conv_ln_g, conv_ln_b, lambda_q1, lambda_k1, lambda_q2, lambda_k2, subln_g, w_out, norm2_g,
           w_router, w_gate, w_up, w_down, final_norm_g):
    batch, seq, _ = x_prompt.shape
    dec_batch, dec_seq, _ = x_sample.shape
    past = cache_k.shape[2]
    ctx_k = cache_k.reshape(dec_batch, DEPTH, past, D_QK)
    assert 1 + dec_batch <= MOD_ROWS

    cvec = jnp.concatenate([c_ctx[None, :], c, jnp.zeros((MOD_ROWS - 1 - dec_batch, D_MODEL), F32)], axis=0)
    mods = _modulation(cvec, w_mod, b_mod)

    w_router_b = jnp.pad(w_router, ((0, 0), (0, 0), (0, LANES - N_EXPERTS))).astype(BF16)
    lam_vecs = jnp.stack([lambda_q1, lambda_k1, lambda_q2, lambda_k2], axis=1)
    vec = lambda a: a.reshape(DEPTH, 1, a.shape[-1])
    norm1_v, norm2_v, subln_v = vec(norm1_g), vec(norm2_g), vec(subln_g)
    conv_b_v, ln_g_v, ln_b_v = vec(conv_b), vec(conv_ln_g), vec(conv_ln_b)
    rope_tabs = _rope_tables(dec_seq)
    final_g = final_norm_g.reshape(1, D_MODEL)
    tri_p = (jnp.arange(seq)[:, None] <= jnp.arange(seq)[None, :]).astype(BF16)
    tri_s = (jnp.arange(dec_seq)[:, None] <= jnp.arange(dec_seq)[None, :]).astype(BF16)

    prompt_row = lambda i: 0
    assert dec_seq % ROW_TILE == 0
    tiles_per_sample = dec_seq // ROW_TILE
    sample_row = lambda i: 1 + i // tiles_per_sample

    xp = x_prompt.reshape(batch * seq, D_MODEL)
    xs = x_sample.reshape(dec_batch * dec_seq, D_MODEL)
    caches = (jnp.zeros((batch, DEPTH, D_QK, seq), F32), jnp.zeros((batch, DEPTH, seq, N_HEADS, V_DIM), F32))
    for l in range(DEPTH):
        last = l == DEPTH - 1
        up, qp, kp, vp, *caches = _inproj(xp, mods, l, norm1_v, w_in, prompt_row, None, seq, caches)
        us, qs, ks, vs = _inproj(xs, mods, l, norm1_v, w_in, sample_row, rope_tabs, dec_seq)
        cp = _conv(up, l, conv_w, conv_b_v, ln_g_v, ln_b_v)
        cs = _conv(us, l, conv_w, conv_b_v, ln_g_v, ln_b_v)
        op = _attention(qp, kp, vp, lam_vecs, subln_v, l, seq, None)
        os_ = _attention(qs, ks, vs, lam_vecs, subln_v, l, dec_seq, (ctx_k, cache_v))
        xnp, h2p, affp = _outproj(xp, cp, op, mods, l, w_out, norm2_v, w_router_b, prompt_row, seq)
        xns, h2s, affs = _outproj(xs, cs, os_, mods, l, w_out, norm2_v, w_router_b, sample_row, dec_seq)
        pos_p, pos_s = _select(affp, tri_p, EC_FACTOR * seq // N_EXPERTS,
                               affs, tri_s, EC_FACTOR * dec_seq // N_EXPERTS)
        slots_p, gates_p = _gather(pos_p, affp, h2p, seq)
        slots_s, gates_s = _gather(pos_s, affs, h2s, dec_seq)
        ye_p, ye_s = _experts(slots_p, slots_s, gates_p, gates_s, l, w_gate, w_up, w_down)
        xp = _combine(xnp, pos_p, ye_p, mods, l, prompt_row, seq, final_g if last else None)
        xs = _combine(xns, pos_s, ye_s, mods, l, sample_row, dec_seq, final_g if last else None)

    y_prompt = xp.reshape(batch, seq, D_MODEL)
    y_sample = xs.reshape(dec_batch, dec_seq, D_MODEL)
    keys_t, new_v = caches
    new_k = keys_t.reshape(batch, DEPTH, N_HEADS, 2, HEAD_DIM, seq).transpose(0, 1, 5, 2, 3, 4)
    return (y_prompt, y_sample, new_k, new_v)
```

```python
import functools
import math

import jax
import jax.numpy as jnp
from jax import lax
from jax.experimental import pallas as pl
from jax.experimental.pallas import tpu as pltpu

D_MODEL = 1024
DEPTH = 4
GRID_W = 64
D_CONV = 512
CONV_WIDTH = 31
N_HEADS = 4
HEAD_DIM = 64
V_DIM = 2 * HEAD_DIM
D_QK = N_HEADS * 2 * HEAD_DIM
D_ATTN = N_HEADS * V_DIM
D_MIX = D_CONV + D_ATTN
D_IN = 2 * D_CONV + 2 * D_QK + D_ATTN
N_EXPERTS = 16
EC_FACTOR = 2
D_EXPERT = 1024
ROPE_BASE = 10000.0
ROPE_AXIS = HEAD_DIM // 2
EPS = 1e-6
LOG2_E = math.log2(math.e)

LANES = 128
MOD_ROWS = 16
TOKEN_TILE = 1024
ROW_TILE = 1024
SUB_ROWS = 256
CONV_CHUNK = 128
NORM_CHUNK = 512
CONV_PAD = 16
EXPERT_ROW_TILE = 256
GATHER_GROUP = 4
MANTISSA_BITS = 23
VMEM_LIMIT = 56 * 1024 * 1024
EXPERT_VMEM_LIMIT = 60 * 1024 * 1024

F32 = jnp.float32
BF16 = jnp.bfloat16


def _sigmoid(x):
    return 1.0 / (1.0 + jnp.exp(-x))


def _params(*sem):
    return pltpu.CompilerParams(dimension_semantics=sem, vmem_limit_bytes=VMEM_LIMIT)


def _mod_kernel(c_ref, w_ref, b_ref, o_ref):
    c = c_ref[...]
    s = c * _sigmoid(c)
    o_ref[...] = jnp.dot(s.astype(BF16), w_ref[...].astype(BF16),
                         preferred_element_type=F32) + b_ref[...]


def _modulation(cvec, w_mod, b_mod):
    tn = 3072
    out = pl.pallas_call(
        _mod_kernel,
        grid=(DEPTH, 6 * D_MODEL // tn),
        in_specs=[
            pl.BlockSpec((MOD_ROWS, D_MODEL), lambda l, j: (0, 0)),
            pl.BlockSpec((None, D_MODEL, tn), lambda l, j: (l, 0, j)),
            pl.BlockSpec((None, 1, tn), lambda l, j: (l, 0, j)),
        ],
        out_specs=pl.BlockSpec((None, MOD_ROWS, tn), lambda l, j: (l, 0, j)),
        out_shape=jax.ShapeDtypeStruct((DEPTH, MOD_ROWS, 6 * D_MODEL), F32),
        compiler_params=_params("parallel", "parallel"),
        name="modulation",
    )(cvec, w_mod, b_mod.reshape(DEPTH, 1, 6 * D_MODEL))
    return out.reshape(DEPTH, MOD_ROWS, 6, D_MODEL)


def _inproj_kernel(*refs, rope, cache_out, seq_len):
    x_ref, mod_ref, g_ref, w_ref = refs[:4]
    refs = refs[4:]
    if rope:
        cos_ref, sin_ref = refs[:2]
        refs = refs[2:]
    if cache_out:
        refs = refs[2:]
    u_ref, q_ref, k_ref, v_ref = refs[:4]
    wb_ref = refs[-1]

    @pl.when(pl.program_id(0) == 0)
    def _():
        wb_ref[...] = w_ref[...].astype(BF16)

    for r in range(u_ref.shape[0]):
        u_ref[r, 0:CONV_PAD, :] = jnp.zeros((CONV_PAD, D_CONV), F32)
        u_ref[r, CONV_PAD + seq_len:, :] = jnp.zeros((CONV_PAD, D_CONV), F32)

    for t in range(x_ref.shape[0] // SUB_ROWS):
        rows = slice(t * SUB_ROWS, (t + 1) * SUB_ROWS)
        req, off = divmod(t * SUB_ROWS, seq_len)
        x = x_ref[rows, :]
        ms = jnp.mean(x * x, axis=-1, keepdims=True)
        y = x * lax.rsqrt(ms + EPS) * g_ref[...]
        h = y * (1.0 + mod_ref[1:2, :]) + mod_ref[0:1, :]
        z = jnp.dot(h.astype(BF16), wb_ref[...], preferred_element_type=F32)
        u_ref[req, CONV_PAD + off:CONV_PAD + off + SUB_ROWS, :] = (
            z[:, :D_CONV] * _sigmoid(z[:, D_CONV:2 * D_CONV]))
        q = z[:, 2 * D_CONV:2 * D_CONV + D_QK]
        k = z[:, 2 * D_CONV + D_QK:2 * D_CONV + 2 * D_QK]
        v = z[:, 2 * D_CONV + 2 * D_QK:]
        if cache_out:
            kf_ref, vf_ref = refs[4:6]
            kf_ref[req, :, off:off + SUB_ROWS] = k.T
            for hd in range(N_HEADS):
                vf_ref[req, off:off + SUB_ROWS, hd, :] = v[:, hd * V_DIM:(hd + 1) * V_DIM]
        if rope:
            lane = lax.broadcasted_iota(jnp.int32, q.shape, 1)
            first = (lane % ROPE_AXIS) < (ROPE_AXIS // 2)
            cos = cos_ref[rows, :]
            sin = sin_ref[rows, :]

            def rot(a):
                partner = jnp.where(first, pltpu.roll(a, D_QK - ROPE_AXIS // 2, 1),
                                    pltpu.roll(a, ROPE_AXIS // 2, 1))
                return a * cos + partner * sin

            q = rot(q)
            k = rot(k)
        q_ref[rows, :] = (q * (HEAD_DIM ** -0.5 * LOG2_E)).astype(BF16)
        k_ref[rows, :] = k.astype(BF16)
        v_ref[rows, :] = v.astype(BF16)


def _inproj(x, mods, layer, norm_g, w_in, mod_row, rope_tabs, seq_len, caches=None):
    n = x.shape[0]
    tm = ROW_TILE
    assert rope_tabs is None or seq_len % tm == 0
    tiles_per_seq = max(1, seq_len // tm)
    rope = rope_tabs is not None
    cache_out = caches is not None
    in_specs = [
        pl.BlockSpec((tm, D_MODEL), lambda i: (i, 0)),
        pl.BlockSpec((None, None, 6, D_MODEL), lambda i: (layer, mod_row(i), 0, 0)),
        pl.BlockSpec((None, 1, D_MODEL), lambda i: (layer, 0, 0)),
        pl.BlockSpec((None, D_MODEL, D_IN), lambda i: (layer, 0, 0), pipeline_mode=pl.Buffered(1)),
    ]
    args = [x, mods, norm_g, w_in]
    if rope:
        in_specs += [pl.BlockSpec((tm, D_QK), lambda i: (i % tiles_per_seq, 0))] * 2
        args += list(rope_tabs)
    assert tm % seq_len == 0 and seq_len % SUB_ROWS == 0
    reqs = tm // seq_len
    padded = seq_len + 2 * CONV_PAD
    half = pl.BlockSpec((tm, D_CONV), lambda i: (i, 0))
    out_specs = [pl.BlockSpec((reqs, padded, D_CONV), lambda i: (i, 0, 0)), half, half, half]
    out_shape = ([jax.ShapeDtypeStruct((n // seq_len, padded, D_CONV), F32)]
                 + [jax.ShapeDtypeStruct((n, D_QK), BF16)] * 3)
    aliases = {}
    if cache_out:
        out_specs += [pl.BlockSpec((reqs, None, D_QK, seq_len), lambda i: (i, layer, 0, 0)),
                      pl.BlockSpec((reqs, None, seq_len, N_HEADS, V_DIM), lambda i: (i, layer, 0, 0, 0))]
        out_shape += [jax.ShapeDtypeStruct(c.shape, c.dtype) for c in caches]
        aliases = {len(args): len(out_shape) - 2, len(args) + 1: len(out_shape) - 1}
        in_specs += [pl.BlockSpec(memory_space=pl.ANY)] * 2
        args += list(caches)
    return pl.pallas_call(
        functools.partial(_inproj_kernel, rope=rope, cache_out=cache_out, seq_len=seq_len),
        grid=(n // tm,),
        in_specs=in_specs,
        out_specs=out_specs,
        out_shape=out_shape,
        input_output_aliases=aliases,
        scratch_shapes=[pltpu.VMEM((D_MODEL, D_IN), BF16)],
        compiler_params=_params("arbitrary"),
        name="inproj",
    )(*args)


def _conv_stage(pad_ref, w_ref, sh_ref, wb_ref):
    for k in range(CONV_WIDTH):
        wb_ref[k] = jnp.broadcast_to(w_ref[k:k + 1, :], (8, D_CONV))
    rows = sh_ref.shape[1]
    for r in range(1, 8):
        sh_ref[r - 1] = pad_ref[r:r + rows, :]


def _conv_block(pad_ref, sh_ref, w, base, lanes):
    first = CONV_PAD - CONV_WIDTH // 2
    groups = CONV_CHUNK // 8
    acc = jnp.zeros((groups, 8, LANES), F32)
    for r in range(8):
        taps = [(k, (first + k) // 8) for k in range(CONV_WIDTH) if (first + k) % 8 == r]
        m0 = min(m for _, m in taps)
        span = max(m for _, m in taps) - m0 + groups
        window = pl.ds(base + 8 * m0, 8 * span)
        x = (pad_ref[window, lanes] if r == 0 else sh_ref[r - 1, window, lanes]).reshape(span, 8, LANES)
        for k, m in taps:
            acc = acc + w[k] * x[m - m0:m - m0 + groups]
    return acc.reshape(CONV_CHUNK, LANES)


def _conv_norm(acc, b_ref, lg_ref, lb_ref):
    acc = acc + b_ref[...]
    mu = jnp.mean(acc, axis=-1, keepdims=True)
    d = acc - mu
    var = jnp.mean(d * d, axis=-1, keepdims=True)
    un = d * lax.rsqrt(var + EPS) * lg_ref[...] + lb_ref[...]
    return (un * _sigmoid(un)).astype(BF16)


def _conv_kernel(pad_ref, w_ref, b_ref, lg_ref, lb_ref, o_ref, sh_ref, wb_ref, acc_ref):
    T = o_ref.shape[0]
    _conv_stage(pad_ref, w_ref, sh_ref, wb_ref)

    for q in range(D_CONV // LANES):
        lanes = slice(q * LANES, (q + 1) * LANES)
        w = [wb_ref[k, :, lanes] for k in range(CONV_WIDTH)]

        def taps_chunk(i, carry, lanes=lanes, w=w):
            base = pl.multiple_of(i * CONV_CHUNK, CONV_CHUNK)
            acc_ref[pl.ds(base, CONV_CHUNK), lanes] = _conv_block(pad_ref, sh_ref, w, base, lanes)
            return carry

        lax.fori_loop(0, T // CONV_CHUNK, taps_chunk, 0)

    rows = min(NORM_CHUNK, T)

    def norm_chunk(i, carry):
        base = pl.multiple_of(i * rows, rows)
        o_ref[pl.ds(base, rows), :] = _conv_norm(acc_ref[pl.ds(base, rows), :], b_ref, lg_ref, lb_ref)
        return carry

    lax.fori_loop(0, T // rows, norm_chunk, 0)


def _conv(u_pad, layer, conv_w, conv_b, ln_g, ln_b):
    nreq, padded, _ = u_pad.shape
    T = padded - 2 * CONV_PAD
    rows = padded - 8
    vec = pl.BlockSpec((None, 1, D_CONV), lambda b: (layer, 0, 0))
    return pl.pallas_call(
        _conv_kernel,
        grid=(nreq,),
        in_specs=[
            pl.BlockSpec((None, padded, D_CONV), lambda b: (b, 0, 0)),
            pl.BlockSpec((None, CONV_WIDTH, D_CONV), lambda b: (layer, 0, 0)),
            vec, vec, vec,
        ],
        out_specs=pl.BlockSpec((T, D_CONV), lambda b: (b, 0)),
        out_shape=jax.ShapeDtypeStruct((nreq * T, D_CONV), BF16),
        scratch_shapes=[pltpu.VMEM((7, rows, D_CONV), F32),
                        pltpu.VMEM((CONV_WIDTH, 8, D_CONV), F32),
                        pltpu.VMEM((T, D_CONV), F32)],
        compiler_params=_params("parallel"),
        name="conv",
    )(u_pad, conv_w, conv_b, ln_g, ln_b)


def _attn_kernel(*refs, lam_init, has_ctx, seq_len):
    lam_ref, sg_ref, q_ref, k_ref, v_ref = refs[:5]
    T = seq_len
    reqs = k_ref.shape[0] // T
    if has_ctx:
        ck_ref, cv_ref, o_ref, vals_ref, keys_ref = refs[5:]
    else:
        o_ref, vals_ref = refs[5:]

    @pl.when(pl.program_id(1) == 0)
    def _():
        vals_ref[...] = jnp.ones(vals_ref.shape, BF16)
        for r in range(reqs):
            for h in range(N_HEADS):
                vals_ref[r, 0:T, 2 * h * V_DIM:(2 * h + 1) * V_DIM] = v_ref[r * T:(r + 1) * T,
                                                                            h * V_DIM:(h + 1) * V_DIM]
        if has_ctx:
            keys_ref[0:T, :] = k_ref[...]
            keys_ref[T:, :] = ck_ref[...].astype(BF16)
            for h in range(N_HEADS):
                vals_ref[0, T:, 2 * h * V_DIM:(2 * h + 1) * V_DIM] = cv_ref[:, h, :].astype(BF16)

    l = lam_ref[...]
    lam = (jnp.exp(jnp.sum(l[0:1, :] * l[1:2, :], axis=-1, keepdims=True))
           - jnp.exp(jnp.sum(l[2:3, :] * l[3:4, :], axis=-1, keepdims=True)) + lam_init)
    tq = q_ref.shape[0] // reqs
    lane = lax.broadcasted_iota(jnp.int32, (tq, V_DIM), 1)
    nt = (((1,), (1,)), ((), ()))
    for r in range(reqs):
        q_rows = slice(r * tq, (r + 1) * tq)
        for h in range(N_HEADS):
            sl = slice(h * V_DIM, (h + 1) * V_DIM)
            qh = q_ref[q_rows, sl]
            kh = keys_ref[:, sl] if has_ctx else k_ref[r * T:(r + 1) * T, sl]
            heads = []
            for c in range(2):
                keep = (lane < HEAD_DIM) if c == 0 else (lane >= HEAD_DIM)
                qc = jnp.where(keep, qh, jnp.zeros_like(qh))
                s = lax.dot_general(qc, kh, nt, preferred_element_type=F32)
                e = jnp.exp2(s - jnp.max(s, axis=-1, keepdims=True)).astype(BF16)
                pv = jnp.dot(e, vals_ref[r, :, 2 * h * V_DIM:(2 * h + 2) * V_DIM],
                             preferred_element_type=F32)
                heads.append(pv[:, :V_DIM] / pv[:, V_DIM:])
            o = heads[0] - lam * heads[1]
            o = o * lax.rsqrt(jnp.mean(o * o, axis=-1, keepdims=True) + EPS)
            o = o * sg_ref[...] * (1.0 - lam_init)
            o_ref[q_rows, sl] = o.astype(BF16)


def _attention(q, k, v, lam_vecs, subln_g, layer, seq_len, ctx):
    n = q.shape[0]
    T = seq_len
    has_ctx = ctx is not None
    tq = min(TOKEN_TILE, T)
    nq = T // tq
    reqs = 1 if has_ctx else max(1, TOKEN_TILE // T)
    lam_init = 0.8 - 0.6 * math.exp(-0.3 * layer)
    kv = pl.BlockSpec((reqs * T, D_QK), lambda b, i: (b, 0))
    tile = pl.BlockSpec((reqs * tq, D_QK), lambda b, i: (b * nq + i, 0))
    in_specs = [
        pl.BlockSpec((None, 4, HEAD_DIM), lambda b, i: (layer, 0, 0)),
        pl.BlockSpec((None, 1, V_DIM), lambda b, i: (layer, 0, 0)),
        tile, kv, kv,
    ]
    args = [lam_vecs, subln_g, q, k, v]
    past = ctx[0].shape[2] if has_ctx else 0
    scratch = [pltpu.VMEM((reqs, T + past, 2 * D_ATTN), BF16)]
    if has_ctx:
        in_specs += [pl.BlockSpec((None, None, past, D_QK), lambda b, i: (b, layer, 0, 0)),
                     pl.BlockSpec((None, None, past, N_HEADS, V_DIM), lambda b, i: (b, layer, 0, 0, 0))]
        args += list(ctx)
        scratch.append(pltpu.VMEM((T + past, D_QK), BF16))
    return pl.pallas_call(
        functools.partial(_attn_kernel, lam_init=lam_init, has_ctx=has_ctx, seq_len=T),
        grid=(n // (reqs * T), nq),
        in_specs=in_specs,
        out_specs=tile,
        out_shape=jax.ShapeDtypeStruct((n, D_ATTN), BF16),
        scratch_shapes=scratch,
        compiler_params=_params("parallel", "arbitrary"),
        name="attention",
    )(*args)


def _outproj_kernel(x_ref, mc_ref, ma_ref, mod_ref, w_ref, g_ref, wr_ref, xn_ref, h2_ref, aff_ref, wb_ref,
                    *, seq_len):
    @pl.when(pl.program_id(0) == 0)
    def _():
        wb_ref[...] = w_ref[...].astype(BF16)

    for t in range(x_ref.shape[0] // SUB_ROWS):
        rows = slice(t * SUB_ROWS, (t + 1) * SUB_ROWS)
        req, off = divmod(t * SUB_ROWS, seq_len)
        mix = jnp.concatenate([mc_ref[rows, :], ma_ref[rows, :]], axis=-1)
        acc = jnp.dot(mix, wb_ref[...], preferred_element_type=F32)
        x = x_ref[rows, :] + mod_ref[2:3, :] * acc
        xn_ref[rows, :] = x
        ms = jnp.mean(x * x, axis=-1, keepdims=True)
        y = x * lax.rsqrt(ms + EPS) * g_ref[...]
        h2 = (y * (1.0 + mod_ref[4:5, :]) + mod_ref[3:4, :]).astype(BF16)
        h2_ref[rows, :] = h2
        logits = jnp.dot(h2, wr_ref[...], preferred_element_type=F32)
        lane = lax.broadcasted_iota(jnp.int32, logits.shape, 1)
        logits = jnp.where(lane < N_EXPERTS, logits, -jnp.inf)
        e = jnp.exp(logits - jnp.max(logits, axis=-1, keepdims=True))
        aff = e / jnp.sum(e, axis=-1, keepdims=True)
        aff_ref[req * N_EXPERTS:(req + 1) * N_EXPERTS, off:off + SUB_ROWS] = aff.T[0:N_EXPERTS, :]


def _outproj(x, mix_conv, mix_attn, mods, layer, w_out, norm_g, w_router_b, mod_row, seq_len):
    n = x.shape[0]
    tm = ROW_TILE
    assert tm % seq_len == 0 and seq_len % SUB_ROWS == 0
    reqs = tm // seq_len
    row = lambda i: (i, 0)
    return pl.pallas_call(
        functools.partial(_outproj_kernel, seq_len=seq_len),
        grid=(n // tm,),
        in_specs=[
            pl.BlockSpec((tm, D_MODEL), row),
            pl.BlockSpec((tm, D_CONV), row),
            pl.BlockSpec((tm, D_ATTN), row),
            pl.BlockSpec((None, None, 6, D_MODEL), lambda i: (layer, mod_row(i), 0, 0)),
            pl.BlockSpec((None, D_MIX, D_MODEL), lambda i: (layer, 0, 0), pipeline_mode=pl.Buffered(1)),
            pl.BlockSpec((None, 1, D_MODEL), lambda i: (layer, 0, 0)),
            pl.BlockSpec((None, D_MODEL, LANES), lambda i: (layer, 0, 0)),
        ],
        out_specs=[pl.BlockSpec((tm, D_MODEL), row), pl.BlockSpec((tm, D_MODEL), row),
                   pl.BlockSpec((reqs * N_EXPERTS, seq_len), row)],
        out_shape=[jax.ShapeDtypeStruct((n, D_MODEL), F32), jax.ShapeDtypeStruct((n, D_MODEL), BF16),
                   jax.ShapeDtypeStruct((n // seq_len * N_EXPERTS, seq_len), F32)],
        scratch_shapes=[pltpu.VMEM((D_MIX, D_MODEL), BF16)],
        compiler_params=_params("arbitrary"),
        name="outproj",
    )(x, mix_conv, mix_attn, mods, w_out, norm_g, w_router_b)


def _top_slots(a, tri, cap):
    rows = a.shape[0]
    capf = float(cap)

    def count_ge(v):
        return jnp.sum(jnp.where(a >= v, 1.0, 0.0), axis=-1, keepdims=True)

    hi = jnp.full((rows, 1), 2.0, F32)
    for j in range(6, -1, -1):
        cand = hi * (2.0 ** -(2 ** j))
        hi = jnp.where(count_ge(cand) < capf, cand, hi)
    lo = hi * 0.5
    step = lo * 0.5
    for _ in range(MANTISSA_BITS):
        cand = lo + step
        lo = jnp.where(count_ge(cand) >= capf, cand, lo)
        step = step * 0.5
    upper = jnp.where(step > 0.0, lo + step * 2.0, hi)
    above = jnp.where(a >= upper, 1.0, 0.0)
    tie = jnp.where(a >= lo, 1.0, 0.0) - above
    need = capf - jnp.sum(above, axis=-1, keepdims=True)
    tie_rank = jnp.dot(tie.astype(BF16), tri, preferred_element_type=F32)
    sel = above + tie * jnp.where(tie_rank <= need, 1.0, 0.0)
    slot = jnp.dot(sel.astype(BF16), tri, preferred_element_type=F32) - 1.0
    return jnp.where(sel > 0.5, slot, -1.0)


def _select_kernel(ap_ref, tp_ref, as_ref, ts_ref, pp_ref, ps_ref, *, caps):
    pp_ref[...] = _top_slots(ap_ref[...], tp_ref[...], caps[0])
    ps_ref[...] = _top_slots(as_ref[...], ts_ref[...], caps[1])


def _select(aff_p, tri_p, cap_p, aff_s, tri_s, cap_s):
    whole = lambda a: pl.BlockSpec(a.shape, lambda i: (0, 0))
    outs = [jax.ShapeDtypeStruct(aff_p.shape, F32), jax.ShapeDtypeStruct(aff_s.shape, F32)]
    return pl.pallas_call(
        functools.partial(_select_kernel, caps=(cap_p, cap_s)),
        grid=(1,),
        in_specs=[whole(aff_p), whole(tri_p), whole(aff_s), whole(tri_s)],
        out_specs=[whole(o) for o in outs],
        out_shape=outs,
        compiler_params=_params("arbitrary"),
        name="select",
    )(aff_p, tri_p, aff_s, tri_s)


def _gather_kernel(pos_ref, a_ref, h2_ref, xs_ref, gs_ref, oh_ref, *, cap):
    reqs, _, T = oh_ref.shape
    slot = lax.broadcasted_iota(jnp.int32, (cap, T), 0).astype(F32)
    for r in range(reqs):
        for g in range(0, N_EXPERTS, GATHER_GROUP):
            for e in range(g, g + GATHER_GROUP):
                row = r * N_EXPERTS + e
                onehot = jnp.where(pos_ref[row:row + 1, :] == slot, 1.0, 0.0)
                oh_ref[r, e * cap:(e + 1) * cap, :] = onehot.astype(BF16)
                gs_ref[e, r * cap:(r + 1) * cap, :] = jnp.sum(onehot * a_ref[row:row + 1, :], axis=-1,
                                                              keepdims=True)
            xs = jnp.dot(oh_ref[r, g * cap:(g + GATHER_GROUP) * cap, :], h2_ref[r * T:(r + 1) * T, :],
                         preferred_element_type=F32)
            xs_ref[g:g + GATHER_GROUP, r * cap:(r + 1) * cap, :] = xs.reshape(
                GATHER_GROUP, cap, D_MODEL).astype(BF16)


def _gather(pos, aff_t, h2, seq_len):
    T = seq_len
    assert ROW_TILE % T == 0
    reqs = ROW_TILE // T
    nreq = h2.shape[0] // T
    cap = EC_FACTOR * T // N_EXPERTS
    rows = pl.BlockSpec((reqs * N_EXPERTS, T), lambda b: (b, 0))
    return pl.pallas_call(
        functools.partial(_gather_kernel, cap=cap),
        grid=(nreq // reqs,),
        in_specs=[rows, rows, pl.BlockSpec((reqs * T, D_MODEL), lambda b: (b, 0))],
        out_specs=[pl.BlockSpec((N_EXPERTS, reqs * cap, D_MODEL), lambda b: (0, b, 0)),
                   pl.BlockSpec((N_EXPERTS, reqs * cap, 1), lambda b: (0, b, 0))],
        out_shape=[jax.ShapeDtypeStruct((N_EXPERTS, nreq * cap, D_MODEL), BF16),
                   jax.ShapeDtypeStruct((N_EXPERTS, nreq * cap, 1), F32)],
        scratch_shapes=[pltpu.VMEM((reqs, N_EXPERTS * cap, T), BF16)],
        compiler_params=_params("parallel"),
        name="gather",
    )(pos, aff_t, h2)


def _expert_kernel(xp_ref, xs_ref, gp_ref, gsm_ref, wg_ref, wu_ref, wd_ref, yp_ref, ys_ref,
                   wgb_ref, wub_ref, wdb_ref):
    wgb_ref[...] = wg_ref[...].astype(BF16)
    wub_ref[...] = wu_ref[...].astype(BF16)
    wdb_ref[...] = wd_ref[...].astype(BF16)
    for x_ref, g_ref, y_ref in ((xp_ref, gp_ref, yp_ref), (xs_ref, gsm_ref, ys_ref)):
        for t in range(x_ref.shape[0] // EXPERT_ROW_TILE):
            rows = slice(t * EXPERT_ROW_TILE, (t + 1) * EXPERT_ROW_TILE)
            x = x_ref[rows, :]
            hg = jnp.dot(x, wgb_ref[...], preferred_element_type=F32)
            hu = jnp.dot(x, wub_ref[...], preferred_element_type=F32)
            hid = (hg * _sigmoid(hg) * hu).astype(BF16)
            y = jnp.dot(hid, wdb_ref[...], preferred_element_type=F32)
            y_ref[rows, :] = (y * g_ref[rows, :]).astype(BF16)


def _experts(xs_p, xs_s, gs_p, gs_s, layer, w_gate, w_up, w_down):
    np_, ns_ = xs_p.shape[1], xs_s.shape[1]
    slots = lambda n, w: pl.BlockSpec((None, n, w), lambda e: (e, 0, 0))
    weight = lambda rows, cols: pl.BlockSpec((None, None, rows, cols), lambda e: (layer, e, 0, 0))
    return pl.pallas_call(
        _expert_kernel,
        grid=(N_EXPERTS,),
        in_specs=[
            slots(np_, D_MODEL), slots(ns_, D_MODEL), slots(np_, 1), slots(ns_, 1),
            weight(D_MODEL, D_EXPERT), weight(D_MODEL, D_EXPERT), weight(D_EXPERT, D_MODEL),
        ],
        out_specs=[slots(np_, D_MODEL), slots(ns_, D_MODEL)],
        out_shape=[jax.ShapeDtypeStruct(xs_p.shape, BF16), jax.ShapeDtypeStruct(xs_s.shape, BF16)],
        scratch_shapes=[pltpu.VMEM((D_MODEL, D_EXPERT), BF16), pltpu.VMEM((D_MODEL, D_EXPERT), BF16),
                        pltpu.VMEM((D_EXPERT, D_MODEL), BF16)],
        compiler_params=pltpu.CompilerParams(dimension_semantics=("parallel",),
                                             vmem_limit_bytes=EXPERT_VMEM_LIMIT),
        name="experts",
    )(xs_p, xs_s, gs_p, gs_s, w_gate, w_up, w_down)


def _combine_kernel(*refs, cap, final):
    xn_ref, pos_ref, ye_ref, mod_ref = refs[:4]
    if final:
        fg_ref = refs[4]
    o_ref = refs[-1]
    seq_len = pos_ref.shape[1]
    slot = lax.broadcasted_iota(jnp.int32, (cap, SUB_ROWS), 0).astype(F32)
    for t in range(xn_ref.shape[0] // SUB_ROWS):
        rows = slice(t * SUB_ROWS, (t + 1) * SUB_ROWS)
        req, off = divmod(t * SUB_ROWS, seq_len)
        ye = ye_ref[:, req * cap:(req + 1) * cap, :].reshape(N_EXPERTS * cap, D_MODEL)
        onehot = jnp.concatenate(
            [jnp.where(pos_ref[req * N_EXPERTS + e:req * N_EXPERTS + e + 1, off:off + SUB_ROWS] == slot,
                       1.0, 0.0).astype(BF16) for e in range(N_EXPERTS)], axis=0)
        moe = lax.dot_general(onehot, ye, (((0,), (0,)), ((), ())), preferred_element_type=F32)
        x = xn_ref[rows, :] + mod_ref[5:6, :] * moe
        if final:
            x = x * lax.rsqrt(jnp.mean(x * x, axis=-1, keepdims=True) + EPS) * fg_ref[...]
        o_ref[rows, :] = x


def _combine(xn, pos, ye, mods, layer, mod_row, seq_len, final_g):
    n = xn.shape[0]
    T = seq_len
    tm = ROW_TILE
    reqs = tm // T
    cap = EC_FACTOR * T // N_EXPERTS
    final = final_g is not None
    assert tm % T == 0 and T % SUB_ROWS == 0
    in_specs = [
        pl.BlockSpec((tm, D_MODEL), lambda i: (i, 0)),
        pl.BlockSpec((reqs * N_EXPERTS, T), lambda i: (i, 0)),
        pl.BlockSpec((N_EXPERTS, reqs * cap, D_MODEL), lambda i: (0, i, 0)),
        pl.BlockSpec((None, None, 6, D_MODEL), lambda i: (layer, mod_row(i), 0, 0)),
    ]
    args = [xn, pos, ye, mods]
    if final:
        in_specs.append(pl.BlockSpec((1, D_MODEL), lambda i: (0, 0)))
        args.append(final_g)
    return pl.pallas_call(
        functools.partial(_combine_kernel, cap=cap, final=final),
        grid=(n // tm,),
        in_specs=in_specs,
        out_specs=pl.BlockSpec((tm, D_MODEL), lambda i: (i, 0)),
        out_shape=jax.ShapeDtypeStruct((n, D_MODEL), F32),
        compiler_params=_params("parallel"),
        name="combine",
    )(*args)


def _rope_tables(T):
    rows = T // GRID_W
    row = jnp.repeat(jnp.arange(rows, dtype=F32), GRID_W)
    col = jnp.tile(jnp.arange(GRID_W, dtype=F32), rows)
    inv = ROPE_BASE ** (-jnp.arange(0, ROPE_AXIS, 2, dtype=F32) / ROPE_AXIS)
    ar = row[:, None] * inv
    ac = col[:, None] * inv
    cos = jnp.concatenate([jnp.cos(ar), jnp.cos(ar), jnp.cos(ac), jnp.cos(ac)], axis=-1)
    sin = jnp.concatenate([-jnp.sin(ar), jnp.sin(ar), -jnp.sin(ac), jnp.sin(ac)], axis=-1)
    reps = D_QK // HEAD_DIM
    return jnp.tile(cos, (1, reps)), jnp.tile(sin, (1, reps))


def kernel(x_prompt, x_sample, c, cache_k, cache_v, c_ctx, norm1_g, w_mod, b_mod, w_in, conv_w, conv_b,
           conv_ln_g, conv_ln_b, lambda_q1, lambda_k1, lambda_q2, lambda_k2, subln_g, w_out, norm2_g,
           w_router, w_gate, w_up, w_down, final_norm_g):
    batch, seq, _ = x_prompt.shape
    dec_batch, dec_seq, _ = x_sample.shape
    past = cache_k.shape[2]
    ctx_k = cache_k.reshape(dec_batch, DEPTH, past, D_QK)
    assert 1 + dec_batch <= MOD_ROWS

    cvec = jnp.concatenate([c_ctx[None, :], c, jnp.zeros((MOD_ROWS - 1 - dec_batch, D_MODEL), F32)], axis=0)
    mods = _modulation(cvec, w_mod, b_mod)

    w_router_b = jnp.pad(w_router, ((0, 0), (0, 0), (0, LANES - N_EXPERTS))).astype(BF16)
    lam_vecs = jnp.stack([lambda_q1, lambda_k1, lambda_q2, lambda_k2], axis=1)
    vec = lambda a: a.reshape(DEPTH, 1, a.shape[-1])
    norm1_v, norm2_v, subln_v = vec(norm1_g), vec(norm2_g), vec(subln_g)
    conv_b_v, ln_g_v, ln_b_v = vec(conv_b), vec(conv_ln_g), vec(conv_ln_b)
    rope_tabs = _rope_tables(dec_seq)
    final_g = final_norm_g.reshape(1, D_MODEL)
    tri_p = (jnp.arange(seq)[:, None] <= jnp.arange(seq)[None, :]).astype(BF16)
    tri_s = (jnp.arange(dec_seq)[:, None] <= jnp.arange(dec_seq)[None, :]).astype(BF16)

    prompt_row = lambda i: 0
    assert dec_seq % ROW_TILE == 0
    tiles_per_sample = dec_seq // ROW_TILE
    sample_row = lambda i: 1 + i // tiles_per_sample

    xp = x_prompt.reshape(batch * seq, D_MODEL)
    xs = x_sample.reshape(dec_batch * dec_seq, D_MODEL)
    caches = (jnp.zeros((batch, DEPTH, D_QK, seq), F32), jnp.zeros((batch, DEPTH, seq, N_HEADS, V_DIM), F32))
    for l in range(DEPTH):
        last = l == DEPTH - 1
        up, qp, kp, vp, *caches = _inproj(xp, mods, l, norm1_v, w_in, prompt_row, None, seq, caches)
        us, qs, ks, vs = _inproj(xs, mods, l, norm1_v, w_in, sample_row, rope_tabs, dec_seq)
        cp = _conv(up, l, conv_w, conv_b_v, ln_g_v, ln_b_v)
        cs = _conv(us, l, conv_w, conv_b_v, ln_g_v, ln_b_v)
        op = _attention(qp, kp, vp, lam_vecs, subln_v, l, seq, None)
        os_ = _attention(qs, ks, vs, lam_vecs, subln_v, l, dec_seq, (ctx_k, cache_v))
        xnp, h2p, affp = _outproj(xp, cp, op, mods, l, w_out, norm2_v, w_router_b, prompt_row, seq)
        xns, h2s, affs = _outproj(xs, cs, os_, mods, l, w_out, norm2_v, w_router_b, sample_row, dec_seq)
        pos_p, pos_s = _select(affp, tri_p, EC_FACTOR * seq // N_EXPERTS,
                               affs, tri_s, EC_FACTOR * dec_seq // N_EXPERTS)
        slots_p, gates_p = _gather(pos_p, affp, h2p, seq)
        slots_s, gates_s = _gather(pos_s, affs, h2s, dec_seq)
        ye_p, ye_s = _experts(slots_p, slots_s, gates_p, gates_s, l, w_gate, w_up, w_down)
        xp = _combine(xnp, pos_p, ye_p, mods, l, prompt_row, seq, final_g if last else None)
        xs = _combine(xns, pos_s, ye_s, mods, l, sample_row, dec_seq, final_g if last else None)

    y_prompt = xp.reshape(batch, seq, D_MODEL)
    y_sample = xs.reshape(dec_batch, dec_seq, D_MODEL)
    keys_t, new_v = caches
    new_k = keys_t.reshape(batch, DEPTH, N_HEADS, 2, HEAD_DIM, seq).transpose(0, 1, 5, 2, 3, 4)
    return (y_prompt, y_sample, new_k, new_v)
```
